```python
import math
import jax, jax.numpy as jnp
from jax import lax
import numpy as np

D_MODEL = 2048
BATCH = 2
SEQ = 4096
DEPTH = 1
DEC_BATCH = 8
DEC_SEQ = 4
PAST_LEN = 16384
PAGE_SIZE = 128

N_HEADS = 16
HEAD_DIM = 64
ATTN_WIDTH = N_HEADS * HEAD_DIM
POOL_WINDOWS = (2, 4, 8, 16)
POOL_GROUPS = len(POOL_WINDOWS)
POOL_WIDTH = D_MODEL // 2
POOL_GROUP_CH = POOL_WIDTH // POOL_GROUPS
POOL_OUT_CH = D_MODEL // POOL_GROUPS
POOL_STATE = max(POOL_WINDOWS) - 1
D_FF = ((8 * D_MODEL // 3 + 255) // 256) * 256
IN_COLS = 3 * ATTN_WIDTH + POOL_WIDTH + 2 * D_MODEL
Q_BLOCK = 128
RMS_EPS = 1e-6
SB_BIAS_HI = -5.0
SB_BIAS_LO = -10.0

kernel_name = "stickbreak_pool_gated_hybrid_step"


def rms_norm(x, g):
    xf = x.astype(jnp.float32)
    xf = xf * lax.rsqrt(jnp.mean(xf * xf, axis=-1, keepdims=True) + RMS_EPS)
    return xf.astype(x.dtype) * g


def in_proj(x, norm_g, w_in, q_g, k_g):
    b, t = x.shape[:2]
    h = rms_norm(x, norm_g) @ w_in
    cuts = [ATTN_WIDTH, 2 * ATTN_WIDTH, 3 * ATTN_WIDTH, 3 * ATTN_WIDTH + POOL_WIDTH,
            3 * ATTN_WIDTH + POOL_WIDTH + D_MODEL]
    q, k, v, u, g_attn, g_pool = jnp.split(h, cuts, axis=-1)
    q = rms_norm(q.reshape(b, t, N_HEADS, HEAD_DIM), q_g)
    k = rms_norm(k.reshape(b, t, N_HEADS, HEAD_DIM), k_g)
    v = v.reshape(b, t, N_HEADS, HEAD_DIM)
    return q, k, v, u, g_attn, g_pool


def stick_breaking(q, k, v, sb_bias, q_pos, k_pos):
    z = jnp.einsum('bthd,bshd->bhts', q.astype(jnp.float32), k.astype(jnp.float32)) * (HEAD_DIM ** -0.5)
    z = z + sb_bias.astype(jnp.float32)[None, :, None, None]
    causal = k_pos[None, :] < q_pos[:, None]
    log_keep = jnp.where(causal, jax.nn.log_sigmoid(-z), 0.0)
    after = lax.cumsum(log_keep, axis=3, reverse=True) - log_keep
    log_a = jnp.where(causal, jax.nn.log_sigmoid(z) + after, -jnp.inf)
    a = jnp.exp(log_a).astype(v.dtype)
    return jnp.einsum('bhts,bshd->bthd', a, v)


def stick_breaking_blocked(q, k, v, sb_bias):
    b, s = q.shape[:2]
    nb = s // Q_BLOCK
    pos = jnp.arange(s, dtype=jnp.int32)
    q_blocks = q.reshape(b, nb, Q_BLOCK, N_HEADS, HEAD_DIM).swapaxes(0, 1)
    p_blocks = pos.reshape(nb, Q_BLOCK)
    out = lax.map(lambda qp: stick_breaking(qp[0], k, v, sb_bias, qp[1], pos), (q_blocks, p_blocks))
    return out.swapaxes(0, 1).reshape(b, s, N_HEADS, HEAD_DIM)


def pool_branch(u_ext, pos, w_pool, pool_scale):
    b = u_ext.shape[0]
    t = pos.shape[0]
    uf = u_ext.astype(jnp.float32)
    cs = jnp.concatenate([jnp.zeros((b, 1, POOL_WIDTH), jnp.float32), jnp.cumsum(uf, axis=1)], axis=1)
    u_new = uf[:, POOL_STATE:]
    groups = []
    for g, w in enumerate(POOL_WINDOWS):
        lo, hi = g * POOL_GROUP_CH, (g + 1) * POOL_GROUP_CH
        win_sum = cs[:, POOL_STATE + 1:, lo:hi] - cs[:, POOL_STATE + 1 - w:POOL_STATE + 1 - w + t, lo:hi]
        cnt = jnp.minimum(w, pos + 1).astype(jnp.float32)[None, :, None]
        groups.append(win_sum / cnt - u_new[..., lo:hi])
    p = jnp.stack(groups, axis=2).astype(u_ext.dtype)
    out = jnp.einsum('btgc,gce->btge', p, w_pool).reshape(b, t, D_MODEL)
    return out * pool_scale


def merge_and_ffn(x, o_attn, o_pool, g_attn, g_pool, w_attn_proj, w_out, norm2_g, w_gate_up, w_down):
    b, t = x.shape[:2]
    a = o_attn.reshape(b, t, ATTN_WIDTH) @ w_attn_proj
    mixed = jax.nn.sigmoid(g_attn) * a + jax.nn.sigmoid(g_pool) * o_pool
    h = x + mixed @ w_out
    gate, up = jnp.split(rms_norm(h, norm2_g) @ w_gate_up, 2, axis=-1)
    return h + (jax.nn.silu(gate) * up) @ w_down


def setup_inputs(seed: int = 0) -> dict:
    key = jax.random.key(seed)
    ks = jax.random.split(key, 20)
    n_pages = PAST_LEN // PAGE_SIZE
    n_used = DEC_BATCH * n_pages
    n_pool = n_used + (n_used + 3) // 4
    f32 = jnp.float32

    def nrm(k, shape, scale=1.0):
        return jax.random.normal(k, shape, f32) * scale

    page_table = jax.random.permutation(ks[5], n_pool)[:n_used].reshape(DEC_BATCH, n_pages).astype(jnp.int32)
    sb_bias = jnp.broadcast_to(jnp.linspace(SB_BIAS_HI, SB_BIAS_LO, N_HEADS, dtype=f32), (DEPTH, N_HEADS)) \
        + nrm(ks[17], (DEPTH, N_HEADS), 0.1)
    return {
        "x_prompt": nrm(ks[0], (BATCH, SEQ, D_MODEL)),
        "x_sample": nrm(ks[1], (DEC_BATCH, DEC_SEQ, D_MODEL)),
        "cache_k": nrm(ks[2], (DEPTH, n_pool, PAGE_SIZE, N_HEADS, HEAD_DIM)),
        "cache_v": nrm(ks[3], (DEPTH, n_pool, PAGE_SIZE, N_HEADS, HEAD_DIM)),
        "state_pool": nrm(ks[4], (DEPTH, DEC_BATCH, POOL_STATE, POOL_WIDTH)),
        "page_table": page_table,
        "norm1_g": 1.0 + nrm(ks[6], (DEPTH, D_MODEL), 0.1),
        "w_in": nrm(ks[7], (DEPTH, D_MODEL, IN_COLS), D_MODEL ** -0.5),
        "q_norm_g": 1.0 + nrm(ks[8], (DEPTH, HEAD_DIM), 0.1),
        "k_norm_g": 1.0 + nrm(ks[9], (DEPTH, HEAD_DIM), 0.1),
        "sb_bias": sb_bias,
        "w_attn_proj": nrm(ks[10], (DEPTH, ATTN_WIDTH, D_MODEL), ATTN_WIDTH ** -0.5),
        "w_pool": nrm(ks[11], (DEPTH, POOL_GROUPS, POOL_GROUP_CH, POOL_OUT_CH), POOL_GROUP_CH ** -0.5),
        "pool_scale": 1.0 + nrm(ks[12], (DEPTH, D_MODEL), 0.1),
        "w_out": nrm(ks[13], (DEPTH, D_MODEL, D_MODEL), D_MODEL ** -0.5),
        "norm2_g": 1.0 + nrm(ks[14], (DEPTH, D_MODEL), 0.1),
        "w_gate_up": nrm(ks[15], (DEPTH, D_MODEL, 2 * D_FF), D_MODEL ** -0.5),
        "w_down": nrm(ks[16], (DEPTH, D_FF, D_MODEL), D_FF ** -0.5),
    }


def reference(x_prompt, x_sample, cache_k, cache_v, state_pool, page_table, norm1_g, w_in,
              q_norm_g, k_norm_g, sb_bias, w_attn_proj, w_pool, pool_scale, w_out, norm2_g, w_gate_up, w_down):
    b_p, seq = x_prompt.shape[:2]
    b_s, t_s = x_sample.shape[:2]
    n_pages = page_table.shape[1]
    past_len = n_pages * cache_k.shape[2]
    pos_p = jnp.arange(seq, dtype=jnp.int32)
    pos_s = past_len + jnp.arange(t_s, dtype=jnp.int32)
    k_pos_s = jnp.arange(past_len + t_s, dtype=jnp.int32)

    xp, xs = x_prompt, x_sample
    kp_l, vp_l, up_l, ks_l, vs_l, us_l = [], [], [], [], [], []
    for l in range(DEPTH):
        q, k, v, u, ga, gb = in_proj(xp, norm1_g[l], w_in[l], q_norm_g[l], k_norm_g[l])
        o_attn = stick_breaking_blocked(q, k, v, sb_bias[l])
        u_ext = jnp.concatenate([jnp.zeros((b_p, POOL_STATE, POOL_WIDTH), u.dtype), u], axis=1)
        o_pool = pool_branch(u_ext, pos_p, w_pool[l], pool_scale[l])
        kp_l.append(k)
        vp_l.append(v)
        up_l.append(u_ext[:, -POOL_STATE:])
        xp = merge_and_ffn(xp, o_attn, o_pool, ga, gb, w_attn_proj[l], w_out[l], norm2_g[l],
                           w_gate_up[l], w_down[l])

        q, k, v, u, ga, gb = in_proj(xs, norm1_g[l], w_in[l], q_norm_g[l], k_norm_g[l])
        k_past = cache_k[l][page_table].reshape(b_s, past_len, N_HEADS, HEAD_DIM)
        v_past = cache_v[l][page_table].reshape(b_s, past_len, N_HEADS, HEAD_DIM)
        k_all = jnp.concatenate([k_past, k], axis=1)
        v_all = jnp.concatenate([v_past, v], axis=1)
        o_attn = stick_breaking(q, k_all, v_all, sb_bias[l], pos_s, k_pos_s)
        u_ext = jnp.concatenate([state_pool[l], u], axis=1)
        o_pool = pool_branch(u_ext, pos_s, w_pool[l], pool_scale[l])
        ks_l.append(k)
        vs_l.append(v)
        us_l.append(u_ext[:, -POOL_STATE:])
        xs = merge_and_ffn(xs, o_attn, o_pool, ga, gb, w_attn_proj[l], w_out[l], norm2_g[l],
                           w_gate_up[l], w_down[l])

    k_prompt_new = jnp.stack(kp_l, axis=0)
    v_prompt_new = jnp.stack(vp_l, axis=0)
    pool_prompt_new = jnp.stack(up_l, axis=0)
    k_sample_new = jnp.stack(ks_l, axis=0)
    v_sample_new = jnp.stack(vs_l, axis=0)
    pool_sample_new = jnp.stack(us_l, axis=0)
    return (xp, xs, k_prompt_new, v_prompt_new, pool_prompt_new, k_sample_new, v_sample_new, pool_sample_new)
```

```python
import functools

import jax
import jax.numpy as jnp
from jax import lax
from jax.experimental import pallas as pl
from jax.experimental.pallas import tpu as pltpu

F32 = jnp.float32
BF16 = jnp.bfloat16

N_HEADS = 16
HEAD_DIM = 64
ATTN_WIDTH = N_HEADS * HEAD_DIM
POOL_WINDOWS = (2, 4, 8, 16)
POOL_GROUPS = len(POOL_WINDOWS)
POOL_STATE = max(POOL_WINDOWS) - 1
POOL_HALO = 16
RMS_EPS = 1e-6

LANES = 128
HEADS_PER_BLOCK = LANES // HEAD_DIM
VMEM_LIMIT = 52 * 1024 * 1024


def _cparams(*sem):
    return pltpu.CompilerParams(dimension_semantics=sem, vmem_limit_bytes=VMEM_LIMIT)


def _row_tile(m, cap):
    t = min(m, cap)
    assert m % t == 0, (m, t)
    return t


def _rms_to_scratch(x_ref, g_ref, xn_ref):
    x = x_ref[...]
    ms = jnp.mean(x * x, axis=-1, keepdims=True)
    xn_ref[...] = ((x * lax.rsqrt(ms + RMS_EPS)) * g_ref[...]).astype(BF16)


def _inproj_kernel(x_ref, g_ref, w_ref, *rest, mode):
    if mode == "headnorm":
        bd_ref, hg_ref, o_ref, xn_ref = rest
    else:
        o_ref, xn_ref = rest

    @pl.when(pl.program_id(1) == 0)
    def _():
        _rms_to_scratch(x_ref, g_ref, xn_ref)

    h = jnp.dot(xn_ref[...], w_ref[...], preferred_element_type=F32)
    if mode == "headnorm":
        ss = jnp.dot((h * h).astype(BF16), bd_ref[...], preferred_element_type=F32)
        h = (h * lax.rsqrt(ss * (1.0 / HEAD_DIM) + RMS_EPS)) * hg_ref[...]
    elif mode == "sigmoid":
        h = 1.0 / (1.0 + jnp.exp(-h))
    o_ref[...] = h.astype(o_ref.dtype)


def _inproj(x2d, g, w, col0, ncols, *, mode, out_dtype, tn, head_gain=None, tm_cap=1024):
    m, d = x2d.shape
    tm = _row_tile(m, tm_cap)
    assert ncols % tn == 0 and col0 % tn == 0
    jb = col0 // tn
    in_specs = [
        pl.BlockSpec((tm, d), lambda i, j: (i, 0)),
        pl.BlockSpec((1, d), lambda i, j: (0, 0)),
        pl.BlockSpec((d, tn), lambda i, j: (0, j + jb)),
    ]
    args = [x2d, g.reshape(1, d), w]
    if mode == "headnorm":
        hid = jnp.arange(tn, dtype=jnp.int32) // HEAD_DIM
        bd = (hid[:, None] == hid[None, :]).astype(BF16)
        in_specs += [pl.BlockSpec((tn, tn), lambda i, j: (0, 0)),
                     pl.BlockSpec((1, tn), lambda i, j: (0, 0))]
        args += [bd, jnp.tile(head_gain.astype(F32), tn // HEAD_DIM).reshape(1, tn)]
    return pl.pallas_call(
        functools.partial(_inproj_kernel, mode=mode),
        grid=(m // tm, ncols // tn),
        in_specs=in_specs,
        out_specs=pl.BlockSpec((tm, tn), lambda i, j: (i, j)),
        out_shape=jax.ShapeDtypeStruct((m, ncols), out_dtype),
        scratch_shapes=[pltpu.VMEM((tm, d), BF16)],
        compiler_params=_cparams("parallel", "arbitrary"),
        name="inproj_" + mode,
    )(*args)


def _cumsum_rhs(tk):
    r = lax.broadcasted_iota(jnp.int32, (tk, 2 * tk), 0)
    c = lax.broadcasted_iota(jnp.int32, (tk, 2 * tk), 1)
    return jnp.where((r > c) | (c >= tk), 1.0, 0.0).astype(BF16)


def _sb_tile(s, bias, carry, uo, mask, v_rhs):
    tk = s.shape[1]
    z = s + bias
    sp = jnp.maximum(z, 0.0) + jnp.log(1.0 + jnp.exp(-jnp.abs(z)))
    if mask is not None:
        sp = jnp.where(mask, sp, 0.0)
    cum = jnp.dot(sp.astype(BF16), uo, preferred_element_type=F32)
    a = jnp.exp(z - sp - cum[:, :tk] - carry)
    if mask is not None:
        a = jnp.where(mask, a, 0.0)
    pv = jnp.dot(a.astype(BF16), v_rhs, preferred_element_type=F32)
    return carry + cum[:, tk:], pv


def _attn_prompt_kernel(bias_ref, q_ref, k_ref, v_ref, o_ref, kt_ref, vz_ref, uo_ref, *, seq, tq):
    pair = pl.program_id(1)
    tk = tq
    uo_ref[...] = _cumsum_rhs(tk)

    sub = lax.broadcasted_iota(jnp.int32, (LANES, tk), 0)
    lane = lax.broadcasted_iota(jnp.int32, (tk, LANES), 1)

    def prep(cb, _):
        off = pl.multiple_of(cb * tk, tk)
        kct = k_ref[pl.ds(off, tk), :].T
        vc = v_ref[pl.ds(off, tk), :]
        for h in range(HEADS_PER_BLOCK):
            lo, hi = h * HEAD_DIM, (h + 1) * HEAD_DIM
            kt_ref[h, cb] = jnp.where((sub >= lo) & (sub < hi), kct, 0.0).astype(BF16)
            vz_ref[h, pl.ds(off, tk), :] = jnp.where((lane >= lo) & (lane < hi), vc, 0.0).astype(BF16)
        return 0

    lax.fori_loop(0, seq // tk, prep, 0)

    rq = lax.broadcasted_iota(jnp.int32, (tq, tk), 0)
    ck = lax.broadcasted_iota(jnp.int32, (tq, tk), 1)
    diag_mask = ck < rq

    def qbody(qb, _):
        qoff = pl.multiple_of(qb * tq, tq)
        qblk = q_ref[pl.ds(qoff, tq), :]
        out = jnp.zeros((tq, LANES), F32)
        for h in range(HEADS_PER_BLOCK):
            bias = bias_ref[pair * HEADS_PER_BLOCK + h]
            s = jnp.dot(qblk, kt_ref[h, qb], preferred_element_type=F32)
            carry, acc = _sb_tile(s, bias, jnp.zeros((tq, tk), F32), uo_ref[...], diag_mask,
                                  vz_ref[h, pl.ds(qoff, tk), :])

            def kbody(i, ca, h=h, bias=bias):
                carry, acc = ca
                kb = qb - 1 - i
                koff = pl.multiple_of(kb * tk, tk)
                s = jnp.dot(qblk, kt_ref[h, kb], preferred_element_type=F32)
                carry, pv = _sb_tile(s, bias, carry, uo_ref[...], None, vz_ref[h, pl.ds(koff, tk), :])
                return carry, acc + pv

            carry, acc = lax.fori_loop(0, qb, kbody, (carry, acc))
            out = out + acc
        o_ref[pl.ds(qoff, tq), :] = out.astype(o_ref.dtype)
        return 0

    lax.fori_loop(0, seq // tq, qbody, 0)


def _attn_prompt(q, k, v, bias, batch, seq, tq=128):
    m, w = q.shape
    npairs = w // LANES
    blk = lambda b, p: (b, p)
    return pl.pallas_call(
        functools.partial(_attn_prompt_kernel, seq=seq, tq=tq),
        grid=(batch, npairs),
        in_specs=[
            pl.BlockSpec(memory_space=pltpu.SMEM),
            pl.BlockSpec((seq, LANES), blk),
            pl.BlockSpec((seq, LANES), blk),
            pl.BlockSpec((seq, LANES), blk),
        ],
        out_specs=pl.BlockSpec((seq, LANES), blk),
        out_shape=jax.ShapeDtypeStruct((m, w), BF16),
        scratch_shapes=[
            pltpu.VMEM((HEADS_PER_BLOCK, seq // tq, LANES, tq), BF16),
            pltpu.VMEM((HEADS_PER_BLOCK, seq, LANES), BF16),
            pltpu.VMEM((tq, 2 * tq), BF16),
        ],
        compiler_params=_cparams("parallel", "arbitrary"),
        name="attn_prompt",
    )(bias, q, k, v)


def _attn_decode_kernel(pt_ref, q_ref, bias_ref, kn_ref, vn_ref, kc_ref, vc_ref, o_ref,
                        acc_ref, carry_ref, uo_ref, *, t_new):
    j = pl.program_id(1)
    rows, page = acc_ref.shape[0], kc_ref.shape[0]

    @pl.when(j == 0)
    def _():
        uo_ref[...] = _cumsum_rhs(page)
        r = lax.broadcasted_iota(jnp.int32, (rows, page), 0)
        c = lax.broadcasted_iota(jnp.int32, (rows, page), 1)
        mask = c < (r // N_HEADS)
        s = lax.dot_general(q_ref[...], kn_ref[...].astype(BF16), (((1,), (1,)), ((), ())),
                            preferred_element_type=F32)
        carry, pv = _sb_tile(s, bias_ref[...], jnp.zeros((rows, page), F32), uo_ref[...], mask,
                             vn_ref[...].astype(BF16))
        carry_ref[...] = carry
        acc_ref[...] = pv

    @pl.when(j > 0)
    def _():
        s = lax.dot_general(q_ref[...], kc_ref[...].astype(BF16), (((1,), (1,)), ((), ())),
                            preferred_element_type=F32)
        carry, pv = _sb_tile(s, bias_ref[...], carry_ref[...], uo_ref[...], None,
                             vc_ref[...].astype(BF16))
        carry_ref[...] = carry
        acc_ref[...] += pv

    @pl.when(j == pl.num_programs(1) - 1)
    def _():
        r = lax.broadcasted_iota(jnp.int32, acc_ref.shape, 0)
        l = lax.broadcasted_iota(jnp.int32, acc_ref.shape, 1)
        own = jnp.where((r % N_HEADS) == (l // HEAD_DIM), acc_ref[...], 0.0)
        for t in range(t_new):
            o_ref[pl.ds(t, 1), :] = jnp.sum(own[t * N_HEADS:(t + 1) * N_HEADS], axis=0, keepdims=True)


def _attn_decode(q_rows, bias_rows, k_new, v_new, cache_k, cache_v, page_table, t_new):
    nb, rows, w = q_rows.shape
    n_pages = page_table.shape[1]
    page = cache_k.shape[1]

    def page_idx(b, j, pt):
        return (pt[b, n_pages - jnp.maximum(j, 1)], 0, 0)

    per_b = lambda b, j, pt: (b, 0, 0)
    grid_spec = pltpu.PrefetchScalarGridSpec(
        num_scalar_prefetch=1,
        grid=(nb, n_pages + 1),
        in_specs=[
            pl.BlockSpec((None, rows, w), per_b),
            pl.BlockSpec((rows, page), lambda b, j, pt: (0, 0)),
            pl.BlockSpec((None, page, w), per_b),
            pl.BlockSpec((None, page, w), per_b),
            pl.BlockSpec((None, page, w), page_idx),
            pl.BlockSpec((None, page, w), page_idx),
        ],
        out_specs=pl.BlockSpec((None, t_new, w), per_b),
        scratch_shapes=[
            pltpu.VMEM((rows, w), F32),
            pltpu.VMEM((rows, page), F32),
            pltpu.VMEM((page, 2 * page), BF16),
        ],
    )
    return pl.pallas_call(
        functools.partial(_attn_decode_kernel, t_new=t_new),
        grid_spec=grid_spec,
        out_shape=jax.ShapeDtypeStruct((nb, t_new, w), F32),
        compiler_params=_cparams("parallel", "arbitrary"),
        name="attn_decode",
    )(page_table, q_rows, bias_rows, k_new, v_new, cache_k, cache_v)


def _pool_tile(ext_ref, p_ref, tp, pos0):
    gc = ext_ref.shape[1] // POOL_GROUPS
    pos = pos0 + lax.broadcasted_iota(jnp.int32, (tp, 1), 0)
    for g, win in enumerate(POOL_WINDOWS):
        cols = slice(g * gc, (g + 1) * gc)
        u_new = ext_ref[pl.ds(POOL_HALO, tp), cols]
        win_sum = u_new
        for d in range(1, win):
            win_sum = win_sum + ext_ref[pl.ds(POOL_HALO - d, tp), cols]
        cnt = jnp.minimum(win, pos + 1).astype(F32)
        p_ref[:, cols] = (win_sum / cnt - u_new).astype(p_ref.dtype)


def _pool_prompt_kernel(u_ref, halo_ref, p_ref, ext_ref, *, tp):
    i = pl.program_id(1)
    ext_ref[pl.ds(0, POOL_HALO), :] = jnp.where(i == 0, 0.0, halo_ref[...])
    ext_ref[pl.ds(POOL_HALO, tp), :] = u_ref[...]
    _pool_tile(ext_ref, p_ref, tp, i * tp)


def _pool_prompt(u, batch, seq, tp_cap=512):
    m, w = u.shape
    tp = _row_tile(seq, tp_cap)
    nt = seq // tp
    hb = tp // POOL_HALO
    return pl.pallas_call(
        functools.partial(_pool_prompt_kernel, tp=tp),
        grid=(batch, nt),
        in_specs=[
            pl.BlockSpec((tp, w), lambda b, i: (b * nt + i, 0)),
            pl.BlockSpec((POOL_HALO, w), lambda b, i: (jnp.maximum((b * nt + i) * hb - 1, 0), 0)),
        ],
        out_specs=pl.BlockSpec((tp, w), lambda b, i: (b * nt + i, 0)),
        out_shape=jax.ShapeDtypeStruct((m, w), BF16),
        scratch_shapes=[pltpu.VMEM((POOL_HALO + tp, w), F32)],
        compiler_params=_cparams("parallel", "arbitrary"),
        name="pool_prompt",
    )(u, u)


def _pool_sample_kernel(u_ref, halo_ref, p_ref, ext_ref, *, tp, pos0):
    ext_ref[pl.ds(0, POOL_HALO), :] = halo_ref[...]
    ext_ref[pl.ds(POOL_HALO, tp), :] = u_ref[...]
    _pool_tile(ext_ref, p_ref, tp, pos0)


def _pool_sample(u3, halo3, pos0):
    nb, tp, w = u3.shape
    return pl.pallas_call(
        functools.partial(_pool_sample_kernel, tp=tp, pos0=pos0),
        grid=(nb,),
        in_specs=[
            pl.BlockSpec((None, tp, w), lambda b: (b, 0, 0)),
            pl.BlockSpec((None, POOL_HALO, w), lambda b: (b, 0, 0)),
        ],
        out_specs=pl.BlockSpec((None, tp, w), lambda b: (b, 0, 0)),
        out_shape=jax.ShapeDtypeStruct((nb, tp, w), F32),
        scratch_shapes=[pltpu.VMEM((POOL_HALO + tp, w), F32)],
        compiler_params=_cparams("parallel"),
        name="pool_sample",
    )(u3, halo3)


def _merge_kernel(o_ref, wap_ref, p_ref, wp_ref, ps_ref, ga_ref, gb_ref, m_ref):
    a = jnp.dot(o_ref[...], wap_ref[...], preferred_element_type=F32)
    op = jnp.dot(p_ref[...], wp_ref[...], preferred_element_type=F32) * ps_ref[...]
    m_ref[...] = (ga_ref[...].astype(F32) * a + gb_ref[...].astype(F32) * op).astype(m_ref.dtype)


def _merge(o, w_ap, p, w_pool, pool_scale, gates, tm_cap=1024):
    m, aw = o.shape
    d = w_ap.shape[1]
    g, gc, tn = w_pool.shape
    assert g * tn == d and g * gc == p.shape[1]
    tm = _row_tile(m, tm_cap)
    return pl.pallas_call(
        _merge_kernel,
        grid=(m // tm, g),
        in_specs=[
            pl.BlockSpec((tm, aw), lambda i, j: (i, 0)),
            pl.BlockSpec((aw, tn), lambda i, j: (0, j)),
            pl.BlockSpec((tm, gc), lambda i, j: (i, j)),
            pl.BlockSpec((None, gc, tn), lambda i, j: (j, 0, 0)),
            pl.BlockSpec((1, tn), lambda i, j: (0, j)),
            pl.BlockSpec((tm, tn), lambda i, j: (i, j)),
            pl.BlockSpec((tm, tn), lambda i, j: (i, j + g)),
        ],
        out_specs=pl.BlockSpec((tm, tn), lambda i, j: (i, j)),
        out_shape=jax.ShapeDtypeStruct((m, d), BF16),
        compiler_params=_cparams("parallel", "arbitrary"),
        name="merge",
    )(o, w_ap, p, w_pool, pool_scale.reshape(1, d), gates, gates)


def _resid_mm_kernel(a_ref, w_ref, r_ref, o_ref):
    o_ref[...] = r_ref[...] + jnp.dot(a_ref[...], w_ref[...], preferred_element_type=F32)


def _resid_mm(a, w, resid, tm_cap, tn=512):
    m, kd = a.shape
    n = w.shape[1]
    tm = _row_tile(m, tm_cap)
    return pl.pallas_call(
        _resid_mm_kernel,
        grid=(m // tm, n // tn),
        in_specs=[
            pl.BlockSpec((tm, kd), lambda i, j: (i, 0)),
            pl.BlockSpec((kd, tn), lambda i, j: (0, j)),
            pl.BlockSpec((tm, tn), lambda i, j: (i, j)),
        ],
        out_specs=pl.BlockSpec((tm, tn), lambda i, j: (i, j)),
        out_shape=jax.ShapeDtypeStruct((m, n), F32),
        compiler_params=_cparams("parallel", "arbitrary"),
        name="resid_mm",
    )(a, w, resid)


def _gate_up_kernel(h_ref, g_ref, wg_ref, wu_ref, o_ref, xn_ref):
    @pl.when(pl.program_id(1) == 0)
    def _():
        _rms_to_scratch(h_ref, g_ref, xn_ref)

    xn = xn_ref[...]
    gate = jnp.dot(xn, wg_ref[...], preferred_element_type=F32)
    up = jnp.dot(xn, wu_ref[...], preferred_element_type=F32)
    o_ref[...] = ((gate / (1.0 + jnp.exp(-gate))) * up).astype(o_ref.dtype)


def _gate_up(h, g, w_gu, tm_cap=1024, tn=512):
    m, d = h.shape
    dff = w_gu.shape[1] // 2
    nj = dff // tn
    tm = _row_tile(m, tm_cap)
    return pl.pallas_call(
        _gate_up_kernel,
        grid=(m // tm, nj),
        in_specs=[
            pl.BlockSpec((tm, d), lambda i, j: (i, 0)),
            pl.BlockSpec((1, d), lambda i, j: (0, 0)),
            pl.BlockSpec((d, tn), lambda i, j: (0, j)),
            pl.BlockSpec((d, tn), lambda i, j: (0, j + nj)),
        ],
        out_specs=pl.BlockSpec((tm, tn), lambda i, j: (i, j)),
        out_shape=jax.ShapeDtypeStruct((m, dff), BF16),
        scratch_shapes=[pltpu.VMEM((tm, d), BF16)],
        compiler_params=_cparams("parallel", "arbitrary"),
        name="gate_up",
    )(h, g.reshape(1, d), w_gu, w_gu)


def _project(x2d, wts):
    d = x2d.shape[1]
    aw, pw = ATTN_WIDTH, d // 2
    w_in, g1 = wts["w_in"], wts["norm1_g"]
    q = _inproj(x2d, g1, w_in, 0, aw, mode="headnorm", out_dtype=BF16, tn=256,
                head_gain=wts["q_norm_g"] * (HEAD_DIM ** -0.5))
    k = _inproj(x2d, g1, w_in, aw, aw, mode="headnorm", out_dtype=F32, tn=256, head_gain=wts["k_norm_g"])
    v = _inproj(x2d, g1, w_in, 2 * aw, aw, mode="plain", out_dtype=F32, tn=512)
    u = _inproj(x2d, g1, w_in, 3 * aw, pw, mode="plain", out_dtype=F32, tn=512)
    gates = _inproj(x2d, g1, w_in, 3 * aw + pw, 2 * d, mode="sigmoid", out_dtype=BF16, tn=512)
    return q, k, v, u, gates


def _mix_and_ffn(x2d, o_attn, p, gates, wts):
    mixed = _merge(o_attn, wts["w_attn_proj"], p, wts["w_pool"], wts["pool_scale"], gates)
    h = _resid_mm(mixed, wts["w_out"], x2d, tm_cap=1024)
    act = _gate_up(h, wts["norm2_g"], wts["w_gate_up"])
    return _resid_mm(act, wts["w_down"], h, tm_cap=512)


def kernel(x_prompt, x_sample, cache_k, cache_v, state_pool, page_table, norm1_g, w_in,
           q_norm_g, k_norm_g, sb_bias, w_attn_proj, w_pool, pool_scale, w_out, norm2_g, w_gate_up, w_down):
    b_p, seq, d = x_prompt.shape
    b_s, t_s = x_sample.shape[:2]
    depth, n_pool, page = cache_k.shape[:3]
    n_pages = page_table.shape[1]
    past_len = n_pages * page
    pw = d // 2

    xp = x_prompt.reshape(b_p * seq, d)
    xs = x_sample.reshape(b_s * t_s, d)
    outs = {name: [] for name in ("kp", "vp", "up", "ks", "vs", "us")}
    for l in range(depth):
        wts = {
            "norm1_g": norm1_g[l], "w_in": w_in[l].astype(BF16), "q_norm_g": q_norm_g[l],
            "k_norm_g": k_norm_g[l], "w_attn_proj": w_attn_proj[l].astype(BF16),
            "w_pool": w_pool[l].astype(BF16), "pool_scale": pool_scale[l], "w_out": w_out[l].astype(BF16),
            "norm2_g": norm2_g[l], "w_gate_up": w_gate_up[l].astype(BF16), "w_down": w_down[l].astype(BF16),
        }
        bias = sb_bias[l].astype(F32)

        q, k, v, u, gates = _project(xp, wts)
        o_attn = _attn_prompt(q, k, v, bias, b_p, seq)
        p = _pool_prompt(u, b_p, seq)
        outs["kp"].append(k.reshape(b_p, seq, N_HEADS, HEAD_DIM))
        outs["vp"].append(v.reshape(b_p, seq, N_HEADS, HEAD_DIM))
        outs["up"].append(u.reshape(b_p, seq, pw)[:, seq - POOL_STATE:])
        xp = _mix_and_ffn(xp, o_attn, p, gates, wts)

        q, k, v, u, gates = _project(xs, wts)
        q4 = q.reshape(b_s, t_s, 1, N_HEADS, HEAD_DIM)
        eye = jnp.eye(N_HEADS, dtype=BF16).reshape(1, 1, N_HEADS, N_HEADS, 1)
        q_rows = (q4 * eye).reshape(b_s, t_s * N_HEADS, ATTN_WIDTH)
        bias_rows = jnp.broadcast_to(jnp.tile(bias, t_s)[:, None], (t_s * N_HEADS, page))
        pad = ((0, 0), (0, page - t_s), (0, 0))
        k_new = jnp.pad(k.reshape(b_s, t_s, ATTN_WIDTH), pad)
        v_new = jnp.pad(v.reshape(b_s, t_s, ATTN_WIDTH), pad)
        o_attn = _attn_decode(q_rows, bias_rows, k_new, v_new,
                              cache_k[l].reshape(n_pool, page, ATTN_WIDTH),
                              cache_v[l].reshape(n_pool, page, ATTN_WIDTH), page_table, t_s)
        o_attn = o_attn.reshape(b_s * t_s, ATTN_WIDTH).astype(BF16)
        u3 = u.reshape(b_s, t_s, pw)
        halo = jnp.pad(state_pool[l], ((0, 0), (POOL_HALO - POOL_STATE, 0), (0, 0)))
        p = _pool_sample(u3, halo, past_len).reshape(b_s * t_s, pw).astype(BF16)
        outs["ks"].append(k.reshape(b_s, t_s, N_HEADS, HEAD_DIM))
        outs["vs"].append(v.reshape(b_s, t_s, N_HEADS, HEAD_DIM))
        outs["us"].append(jnp.concatenate([state_pool[l], u3], axis=1)[:, -POOL_STATE:])
        xs = _mix_and_ffn(xs, o_attn, p, gates, wts)

    st = lambda name: jnp.stack(outs[name], axis=0)
    return (xp.reshape(b_p, seq, d), xs.reshape(b_s, t_s, d),
            st("kp"), st("vp"), st("up"), st("ks"), st("vs"), st("us"))
```

```python
import functools

import jax
import jax.numpy as jnp
from jax import lax
from jax.experimental import pallas as pl
from jax.experimental.pallas import tpu as pltpu

F32 = jnp.float32
BF16 = jnp.bfloat16

N_HEADS = 16
HEAD_DIM = 64
ATTN_WIDTH = N_HEADS * HEAD_DIM
POOL_WINDOWS = (2, 4, 8, 16)
POOL_GROUPS = len(POOL_WINDOWS)
POOL_STATE = max(POOL_WINDOWS) - 1
POOL_HALO = 16
RMS_EPS = 1e-6

LANES = 128
HEADS_PER_BLOCK = LANES // HEAD_DIM
VMEM_LIMIT = 52 * 1024 * 1024
ATTN_TILE = 256
DECODE_PAGES_PER_STEP = 8

NT_DIMS = (((1,), (1,)), ((), ()))


def _cparams(*sem):
    return pltpu.CompilerParams(dimension_semantics=sem, vmem_limit_bytes=VMEM_LIMIT)


def _row_tile(m, cap):
    t = min(m, cap)
    while m % t:
        t //= 2
    return t


def _rms_to_scratch(x_ref, g_ref, xn_ref):
    x = x_ref[...]
    ms = jnp.mean(x * x, axis=-1, keepdims=True)
    xn_ref[...] = ((x * lax.rsqrt(ms + RMS_EPS)) * g_ref[...]).astype(BF16)


def _inproj_kernel(x_ref, g_ref, w_ref, *rest, mode, transposed):
    if mode == "headnorm":
        bd_ref, hg_ref, o_ref, xn_ref = rest
    else:
        o_ref, xn_ref = rest

    @pl.when(pl.program_id(1) == 0)
    def _():
        _rms_to_scratch(x_ref, g_ref, xn_ref)

    h = jnp.dot(xn_ref[...], w_ref[...], preferred_element_type=F32)
    if mode == "headnorm":
        ss = jnp.dot((h * h).astype(BF16), bd_ref[...], preferred_element_type=F32)
        h = (h * lax.rsqrt(ss * (1.0 / HEAD_DIM) + RMS_EPS)) * hg_ref[...]
    elif mode == "sigmoid":
        h = 1.0 / (1.0 + jnp.exp(-h))
    if transposed:
        o_ref[...] = h.T.reshape(o_ref.shape).astype(o_ref.dtype)
    else:
        o_ref[...] = h.astype(o_ref.dtype)


def _inproj(x2d, g, w, col0, ncols, *, mode, out_dtype, tn, head_gain=None, seq=None, tm_cap=1024):
    m, d = x2d.shape
    tm = _row_tile(m if seq is None else seq, tm_cap)
    assert ncols % tn == 0 and col0 % tn == 0
    jb = col0 // tn
    in_specs = [
        pl.BlockSpec((tm, d), lambda i, j: (i, 0)),
        pl.BlockSpec((1, d), lambda i, j: (0, 0)),
        pl.BlockSpec((d, tn), lambda i, j: (0, j + jb)),
    ]
    args = [x2d, g.reshape(1, d), w]
    if mode == "headnorm":
        hid = jnp.arange(tn, dtype=jnp.int32) // HEAD_DIM
        bd = (hid[:, None] == hid[None, :]).astype(BF16)
        in_specs += [pl.BlockSpec((tn, tn), lambda i, j: (0, 0)),
                     pl.BlockSpec((1, tn), lambda i, j: (0, 0))]
        args += [bd, jnp.tile(head_gain.astype(F32), tn // HEAD_DIM).reshape(1, tn)]
    if seq is None:
        out_spec = pl.BlockSpec((tm, tn), lambda i, j: (i, j))
        out_shape = jax.ShapeDtypeStruct((m, ncols), out_dtype)
    else:
        nt, hpt = seq // tm, tn // HEAD_DIM
        out_spec = pl.BlockSpec((None, hpt, HEAD_DIM, tm), lambda i, j: (i // nt, j, 0, i % nt))
        out_shape = jax.ShapeDtypeStruct((m // seq, ncols // HEAD_DIM, HEAD_DIM, seq), out_dtype)
    return pl.pallas_call(
        functools.partial(_inproj_kernel, mode=mode, transposed=seq is not None),
        grid=(m // tm, ncols // tn),
        in_specs=in_specs,
        out_specs=out_spec,
        out_shape=out_shape,
        scratch_shapes=[pltpu.VMEM((tm, d), BF16)],
        compiler_params=_cparams("parallel", "arbitrary"),
        name="inproj_" + mode + ("_t" if seq is not None else ""),
    )(*args)


def _cumsum_rhs():
    r = lax.broadcasted_iota(jnp.int32, (LANES, 2 * LANES), 0)
    c = lax.broadcasted_iota(jnp.int32, (LANES, 2 * LANES), 1)
    return jnp.where((r > c) | (c >= LANES), 1.0, 0.0).astype(BF16)


def _sb_block(s, bias, carry, uo, mask, vt):
    z = s + bias
    sp = jnp.maximum(z, 0.0) + jnp.log(1.0 + jnp.exp(-jnp.abs(z)))
    if mask is not None:
        sp = jnp.where(mask, sp, 0.0)
    log_beta = z - sp
    sp16 = sp.astype(BF16)
    parts = []
    for c in reversed(range(s.shape[1] // LANES)):
        cols = slice(c * LANES, (c + 1) * LANES)
        cum = jnp.dot(sp16[:, cols], uo, preferred_element_type=F32)
        parts.append(jnp.exp(log_beta[:, cols] - cum[:, :LANES] - carry))
        carry = carry + cum[:, LANES:]
    a = parts[0] if len(parts) == 1 else jnp.concatenate(parts[::-1], axis=1)
    if mask is not None:
        a = jnp.where(mask, a, 0.0)
    pv = lax.dot_general(a.astype(BF16), vt, NT_DIMS, preferred_element_type=F32)
    return carry, pv


def _attn_prompt_kernel(bias_ref, q_ref, kt_ref, vt_ref, o_ref, kz_ref, vz_ref, uo_ref, carry_ref, acc_ref,
                        *, seq, tile):
    pair = pl.program_id(1)
    nblk = seq // tile
    uo_ref[...] = _cumsum_rhs()
    kz_ref[...] = jnp.zeros(kz_ref.shape, BF16)
    vz_ref[...] = jnp.zeros(vz_ref.shape, BF16)
    for h in range(HEADS_PER_BLOCK):
        rows = pl.ds(h * HEAD_DIM, HEAD_DIM)
        for cb in range(nblk):
            cols = pl.ds(cb * tile, tile)
            kz_ref[h, cb, rows, :] = kt_ref[h, :, cols].astype(BF16)
            vz_ref[h, cb, rows, :] = vt_ref[h, :, cols].astype(BF16)

    rq = lax.broadcasted_iota(jnp.int32, (tile, tile), 0)
    ck = lax.broadcasted_iota(jnp.int32, (tile, tile), 1)
    diag_mask = ck < rq

    def qbody(qb, _):
        qoff = pl.multiple_of(qb * tile, tile)
        qblk = q_ref[pl.ds(qoff, tile), :]

        def step(kb, mask, first):
            for h in range(HEADS_PER_BLOCK):
                s = jnp.dot(qblk, kz_ref[h, kb], preferred_element_type=F32)
                carry = jnp.zeros((tile, LANES), F32) if first else carry_ref[h]
                carry, pv = _sb_block(s, bias_ref[pair * HEADS_PER_BLOCK + h], carry, uo_ref[...], mask,
                                      vz_ref[h, kb])
                carry_ref[h] = carry
                if first:
                    acc_ref[h] = pv
                else:
                    acc_ref[h] += pv

        step(qb, diag_mask, True)

        def kbody(i, _):
            step(qb - 1 - i, None, False)
            return 0

        lax.fori_loop(0, qb, kbody, 0)
        out = acc_ref[0]
        for h in range(1, HEADS_PER_BLOCK):
            out = out + acc_ref[h]
        o_ref[pl.ds(qoff, tile), :] = out.astype(o_ref.dtype)
        return 0

    lax.fori_loop(0, nblk, qbody, 0)


def _attn_prompt(q, kt, vt, bias, batch, seq):
    m, w = q.shape
    npairs = w // LANES
    tile = _row_tile(seq, ATTN_TILE)
    kv_spec = pl.BlockSpec((None, HEADS_PER_BLOCK, HEAD_DIM, seq), lambda b, p: (b, p, 0, 0))
    return pl.pallas_call(
        functools.partial(_attn_prompt_kernel, seq=seq, tile=tile),
        grid=(batch, npairs),
        in_specs=[
            pl.BlockSpec(memory_space=pltpu.SMEM),
            pl.BlockSpec((seq, LANES), lambda b, p: (b, p)),
            kv_spec,
            kv_spec,
        ],
        out_specs=pl.BlockSpec((seq, LANES), lambda b, p: (b, p)),
        out_shape=jax.ShapeDtypeStruct((m, w), BF16),
        scratch_shapes=[
            pltpu.VMEM((HEADS_PER_BLOCK, seq // tile, LANES, tile), BF16),
            pltpu.VMEM((HEADS_PER_BLOCK, seq // tile, LANES, tile), BF16),
            pltpu.VMEM((LANES, 2 * LANES), BF16),
            pltpu.VMEM((HEADS_PER_BLOCK, tile, LANES), F32),
            pltpu.VMEM((HEADS_PER_BLOCK, tile, LANES), F32),
        ],
        compiler_params=_cparams("parallel", "arbitrary"),
        name="attn_prompt",
    )(bias, q, kt, vt)


def _attn_decode_kernel(pt_ref, q_ref, bias_ref, kn_ref, vn_ref, *rest, t_new, pages_per_step):
    kc_refs = rest[:pages_per_step]
    vc_refs = rest[pages_per_step:2 * pages_per_step]
    o_ref, acc_ref, carry_ref, uo_ref = rest[2 * pages_per_step:]
    j = pl.program_id(1)
    rows, slots = carry_ref.shape

    @pl.when(j == 0)
    def _():
        uo_ref[...] = _cumsum_rhs()
        r = lax.broadcasted_iota(jnp.int32, (rows, slots), 0)
        c = lax.broadcasted_iota(jnp.int32, (rows, slots), 1)
        mask = c < (r // N_HEADS)
        s = jnp.dot(q_ref[...], kn_ref[...].astype(BF16), preferred_element_type=F32)
        carry, pv = _sb_block(s, bias_ref[...], jnp.zeros((rows, slots), F32), uo_ref[...], mask,
                              vn_ref[...].astype(BF16))
        carry_ref[...] = carry
        acc_ref[...] = pv

    @pl.when(j > 0)
    def _():
        carry = carry_ref[...]
        total = None
        for kc_ref, vc_ref in zip(kc_refs, vc_refs):
            s = jnp.dot(q_ref[...], kc_ref[...].astype(BF16), preferred_element_type=F32)
            carry, pv = _sb_block(s, bias_ref[...], carry, uo_ref[...], None, vc_ref[...].astype(BF16))
            total = pv if total is None else total + pv
        carry_ref[...] = carry
        acc_ref[...] += total

    @pl.when(j == pl.num_programs(1) - 1)
    def _():
        r = lax.broadcasted_iota(jnp.int32, acc_ref.shape, 0)
        l = lax.broadcasted_iota(jnp.int32, acc_ref.shape, 1)
        own = jnp.where((r % N_HEADS) == (l // HEAD_DIM), acc_ref[...], 0.0)
        for t in range(t_new):
            o_ref[pl.ds(t, 1), :] = jnp.sum(own[t * N_HEADS:(t + 1) * N_HEADS], axis=0, keepdims=True)


def _attn_decode(q_rows, bias_rows, kt_new, vt_new, cache_kt, cache_vt, page_table, t_new):
    nb, rows, w = q_rows.shape
    n_pages = page_table.shape[1]
    slots = cache_kt.shape[2]
    pps = max(p for p in (DECODE_PAGES_PER_STEP, 4, 2, 1) if n_pages % p == 0)

    def page_idx(r):
        return lambda b, j, pt: (pt[b, n_pages - 1 - (jnp.maximum(j, 1) - 1) * pps - r], 0, 0)

    per_b = lambda b, j, pt: (b, 0, 0)
    page_specs = [pl.BlockSpec((None, w, slots), page_idx(r)) for r in range(pps)]
    grid_spec = pltpu.PrefetchScalarGridSpec(
        num_scalar_prefetch=1,
        grid=(nb, n_pages // pps + 1),
        in_specs=[
            pl.BlockSpec((None, rows, w), per_b),
            pl.BlockSpec((rows, slots), lambda b, j, pt: (0, 0)),
            pl.BlockSpec((None, w, slots), per_b),
            pl.BlockSpec((None, w, slots), per_b),
        ] + page_specs + page_specs,
        out_specs=pl.BlockSpec((None, t_new, w), per_b),
        scratch_shapes=[
            pltpu.VMEM((rows, w), F32),
            pltpu.VMEM((rows, slots), F32),
            pltpu.VMEM((LANES, 2 * LANES), BF16),
        ],
    )
    return pl.pallas_call(
        functools.partial(_attn_decode_kernel, t_new=t_new, pages_per_step=pps),
        grid_spec=grid_spec,
        out_shape=jax.ShapeDtypeStruct((nb, t_new, w), F32),
        compiler_params=_cparams("parallel", "arbitrary"),
        name="attn_decode",
    )(page_table, q_rows, bias_rows, kt_new, vt_new, *([cache_kt] * pps), *([cache_vt] * pps))


def _pool_tile(ext_ref, p_ref, tp, pos0):
    gc = ext_ref.shape[1] // POOL_GROUPS
    pos = pos0 + lax.broadcasted_iota(jnp.int32, (tp, 1), 0)
    for g, win in enumerate(POOL_WINDOWS):
        cols = slice(g * gc, (g + 1) * gc)
        u_new = ext_ref[pl.ds(POOL_HALO, tp), cols]
        win_sum = u_new
        for d in range(1, win):
            win_sum = win_sum + ext_ref[pl.ds(POOL_HALO - d, tp), cols]
        cnt = jnp.minimum(win, pos + 1).astype(F32)
        p_ref[:, cols] = (win_sum / cnt - u_new).astype(p_ref.dtype)


def _pool_prompt_kernel(u_ref, halo_ref, p_ref, ext_ref, *, tp):
    i = pl.program_id(1)
    ext_ref[pl.ds(0, POOL_HALO), :] = jnp.where(i == 0, 0.0, halo_ref[...])
    ext_ref[pl.ds(POOL_HALO, tp), :] = u_ref[...]
    _pool_tile(ext_ref, p_ref, tp, i * tp)


def _pool_prompt(u, batch, seq, tp_cap=512):
    m, w = u.shape
    tp = _row_tile(seq, tp_cap)
    nt = seq // tp
    hb = tp // POOL_HALO
    return pl.pallas_call(
        functools.partial(_pool_prompt_kernel, tp=tp),
        grid=(batch, nt),
        in_specs=[
            pl.BlockSpec((tp, w), lambda b, i: (b * nt + i, 0)),
            pl.BlockSpec((POOL_HALO, w), lambda b, i: (jnp.maximum((b * nt + i) * hb - 1, 0), 0)),
        ],
        out_specs=pl.BlockSpec((tp, w), lambda b, i: (b * nt + i, 0)),
        out_shape=jax.ShapeDtypeStruct((m, w), BF16),
        scratch_shapes=[pltpu.VMEM((POOL_HALO + tp, w), F32)],
        compiler_params=_cparams("parallel", "arbitrary"),
        name="pool_prompt",
    )(u, u)


def _pool_sample_kernel(u_ref, halo_ref, p_ref, ext_ref, *, tp, pos0):
    ext_ref[pl.ds(0, POOL_HALO), :] = halo_ref[...]
    ext_ref[pl.ds(POOL_HALO, tp), :] = u_ref[...]
    _pool_tile(ext_ref, p_ref, tp, pos0)


def _pool_sample(u3, halo3, pos0):
    nb, tp, w = u3.shape
    return pl.pallas_call(
        functools.partial(_pool_sample_kernel, tp=tp, pos0=pos0),
        grid=(nb,),
        in_specs=[
            pl.BlockSpec((None, tp, w), lambda b: (b, 0, 0)),
            pl.BlockSpec((None, POOL_HALO, w), lambda b: (b, 0, 0)),
        ],
        out_specs=pl.BlockSpec((None, tp, w), lambda b: (b, 0, 0)),
        out_shape=jax.ShapeDtypeStruct((nb, tp, w), F32),
        scratch_shapes=[pltpu.VMEM((POOL_HALO + tp, w), F32)],
        compiler_params=_cparams("parallel"),
        name="pool_sample",
    )(u3, halo3)


def _merge_kernel(o_ref, wap_ref, p_ref, wp_ref, ps_ref, ga_ref, gb_ref, m_ref):
    a = jnp.dot(o_ref[...], wap_ref[...], preferred_element_type=F32)
    op = jnp.dot(p_ref[...], wp_ref[...], preferred_element_type=F32) * ps_ref[...]
    m_ref[...] = (ga_ref[...].astype(F32) * a + gb_ref[...].astype(F32) * op).astype(m_ref.dtype)


def _merge(o, w_ap, p, w_pool, pool_scale, gates, tm_cap=1024):
    m, aw = o.shape
    d = w_ap.shape[1]
    g, gc, tn = w_pool.shape
    assert g * tn == d and g * gc == p.shape[1]
    tm = _row_tile(m, tm_cap)
    return pl.pallas_call(
        _merge_kernel,
        grid=(m // tm, g),
        in_specs=[
            pl.BlockSpec((tm, aw), lambda i, j: (i, 0)),
            pl.BlockSpec((aw, tn), lambda i, j: (0, j)),
            pl.BlockSpec((tm, gc), lambda i, j: (i, j)),
            pl.BlockSpec((None, gc, tn), lambda i, j: (j, 0, 0)),
            pl.BlockSpec((1, tn), lambda i, j: (0, j)),
            pl.BlockSpec((tm, tn), lambda i, j: (i, j)),
            pl.BlockSpec((tm, tn), lambda i, j: (i, j + g)),
        ],
        out_specs=pl.BlockSpec((tm, tn), lambda i, j: (i, j)),
        out_shape=jax.ShapeDtypeStruct((m, d), BF16),
        compiler_params=_cparams("parallel", "arbitrary"),
        name="merge",
    )(o, w_ap, p, w_pool, pool_scale.reshape(1, d), gates, gates)


def _resid_mm_kernel(a_ref, w_ref, r_ref, o_ref):
    o_ref[...] = r_ref[...] + jnp.dot(a_ref[...], w_ref[...], preferred_element_type=F32)


def _resid_mm(a, w, resid, tm_cap, tn=512):
    m, kd = a.shape
    n = w.shape[1]
    tm = _row_tile(m, tm_cap)
    return pl.pallas_call(
        _resid_mm_kernel,
        grid=(m // tm, n // tn),
        in_specs=[
            pl.BlockSpec((tm, kd), lambda i, j: (i, 0)),
            pl.BlockSpec((kd, tn), lambda i, j: (0, j)),
            pl.BlockSpec((tm, tn), lambda i, j: (i, j)),
        ],
        out_specs=pl.BlockSpec((tm, tn), lambda i, j: (i, j)),
        out_shape=jax.ShapeDtypeStruct((m, n), F32),
        compiler_params=_cparams("parallel", "arbitrary"),
        name="resid_mm",
    )(a, w, resid)


def _gate_up_kernel(h_ref, g_ref, wg_ref, wu_ref, o_ref, xn_ref):
    @pl.when(pl.program_id(1) == 0)
    def _():
        _rms_to_scratch(h_ref, g_ref, xn_ref)

    xn = xn_ref[...]
    gate = jnp.dot(xn, wg_ref[...], preferred_element_type=F32)
    up = jnp.dot(xn, wu_ref[...], preferred_element_type=F32)
    o_ref[...] = ((gate / (1.0 + jnp.exp(-gate))) * up).astype(o_ref.dtype)


def _gate_up(h, g, w_gu, tm_cap=1024, tn=512):
    m, d = h.shape
    dff = w_gu.shape[1] // 2
    nj = dff // tn
    tm = _row_tile(m, tm_cap)
    return pl.pallas_call(
        _gate_up_kernel,
        grid=(m // tm, nj),
        in_specs=[
            pl.BlockSpec((tm, d), lambda i, j: (i, 0)),
            pl.BlockSpec((1, d), lambda i, j: (0, 0)),
            pl.BlockSpec((d, tn), lambda i, j: (0, j)),
            pl.BlockSpec((d, tn), lambda i, j: (0, j + nj)),
        ],
        out_specs=pl.BlockSpec((tm, tn), lambda i, j: (i, j)),
        out_shape=jax.ShapeDtypeStruct((m, dff), BF16),
        scratch_shapes=[pltpu.VMEM((tm, d), BF16)],
        compiler_params=_cparams("parallel", "arbitrary"),
        name="gate_up",
    )(h, g.reshape(1, d), w_gu, w_gu)


def _project(x2d, wts, seq=None):
    d = x2d.shape[1]
    aw, pw = ATTN_WIDTH, d // 2
    w_in, g1 = wts["w_in"], wts["norm1_g"]
    q = _inproj(x2d, g1, w_in, 0, aw, mode="headnorm", out_dtype=BF16, tn=256,
                head_gain=wts["q_norm_g"] * (HEAD_DIM ** -0.5))
    k = _inproj(x2d, g1, w_in, aw, aw, mode="headnorm", out_dtype=F32, tn=256, head_gain=wts["k_norm_g"],
                seq=seq)
    v = _inproj(x2d, g1, w_in, 2 * aw, aw, mode="plain", out_dtype=F32, tn=256, seq=seq)
    u = _inproj(x2d, g1, w_in, 3 * aw, pw, mode="plain", out_dtype=F32, tn=512)
    gates = _inproj(x2d, g1, w_in, 3 * aw + pw, 2 * d, mode="sigmoid", out_dtype=BF16, tn=512)
    return q, k, v, u, gates


def _mix_and_ffn(x2d, o_attn, p, gates, wts):
    mixed = _merge(o_attn, wts["w_attn_proj"], p, wts["w_pool"], wts["pool_scale"], gates)
    h = _resid_mm(mixed, wts["w_out"], x2d, tm_cap=1024)
    act = _gate_up(h, wts["norm2_g"], wts["w_gate_up"])
    return _resid_mm(act, wts["w_down"], h, tm_cap=512)


def kernel(x_prompt, x_sample, cache_k, cache_v, state_pool, page_table, norm1_g, w_in,
           q_norm_g, k_norm_g, sb_bias, w_attn_proj, w_pool, pool_scale, w_out, norm2_g, w_gate_up, w_down):
    b_p, seq, d = x_prompt.shape
    b_s, t_s = x_sample.shape[:2]
    depth, n_pool, page = cache_k.shape[:3]
    n_pages = page_table.shape[1]
    past_len = n_pages * page
    pw = d // 2

    xp = x_prompt.reshape(b_p * seq, d)
    xs = x_sample.reshape(b_s * t_s, d)
    outs = {name: [] for name in ("kp", "vp", "up", "ks", "vs", "us")}
    for l in range(depth):
        wts = {
            "norm1_g": norm1_g[l], "w_in": w_in[l].astype(BF16), "q_norm_g": q_norm_g[l],
            "k_norm_g": k_norm_g[l], "w_attn_proj": w_attn_proj[l].astype(BF16),
            "w_pool": w_pool[l].astype(BF16), "pool_scale": pool_scale[l], "w_out": w_out[l].astype(BF16),
            "norm2_g": norm2_g[l], "w_gate_up": w_gate_up[l].astype(BF16), "w_down": w_down[l].astype(BF16),
        }
        bias = sb_bias[l].astype(F32)

        q, kt, vt, u, gates = _project(xp, wts, seq=seq)
        o_attn = _attn_prompt(q, kt, vt, bias, b_p, seq)
        p = _pool_prompt(u, b_p, seq)
        outs["kp"].append(kt.transpose(0, 3, 1, 2))
        outs["vp"].append(vt.transpose(0, 3, 1, 2))
        outs["up"].append(u.reshape(b_p, seq, pw)[:, seq - POOL_STATE:])
        xp = _mix_and_ffn(xp, o_attn, p, gates, wts)

        q, k, v, u, gates = _project(xs, wts)
        q4 = q.reshape(b_s, t_s, 1, N_HEADS, HEAD_DIM)
        eye = jnp.eye(N_HEADS, dtype=BF16).reshape(1, 1, N_HEADS, N_HEADS, 1)
        q_rows = (q4 * eye).reshape(b_s, t_s * N_HEADS, ATTN_WIDTH)
        bias_rows = jnp.broadcast_to(jnp.tile(bias, t_s)[:, None], (t_s * N_HEADS, page))
        pad = ((0, 0), (0, 0), (0, page - t_s))
        kt_new = jnp.pad(k.reshape(b_s, t_s, ATTN_WIDTH).transpose(0, 2, 1), pad)
        vt_new = jnp.pad(v.reshape(b_s, t_s, ATTN_WIDTH).transpose(0, 2, 1), pad)
        cache_kt = cache_k[l].transpose(0, 2, 3, 1).reshape(n_pool, ATTN_WIDTH, page)
        cache_vt = cache_v[l].transpose(0, 2, 3, 1).reshape(n_pool, ATTN_WIDTH, page)
        o_attn = _attn_decode(q_rows, bias_rows, kt_new, vt_new, cache_kt, cache_vt, page_table, t_s)
        o_attn = o_attn.reshape(b_s * t_s, ATTN_WIDTH).astype(BF16)
        u3 = u.reshape(b_s, t_s, pw)
        halo = jnp.pad(state_pool[l], ((0, 0), (POOL_HALO - POOL_STATE, 0), (0, 0)))
        p = _pool_sample(u3, halo, past_len).reshape(b_s * t_s, pw).astype(BF16)
        outs["ks"].append(k.reshape(b_s, t_s, N_HEADS, HEAD_DIM))
        outs["vs"].append(v.reshape(b_s, t_s, N_HEADS, HEAD_DIM))
        outs["us"].append(jnp.concatenate([state_pool[l], u3], axis=1)[:, -POOL_STATE:])
        xs = _mix_and_ffn(xs, o_attn, p, gates, wts)

    st = lambda name: jnp.stack(outs[name], axis=0)
    return (xp.reshape(b_p, seq, d), xs.reshape(b_s, t_s, d),
            st("kp"), st("vp"), st("up"), st("ks"), st("vs"), st("us"))
```

```python
import functools

import jax
import jax.numpy as jnp
from jax import lax
from jax.experimental import pallas as pl
from jax.experimental.pallas import tpu as pltpu

F32 = jnp.float32
BF16 = jnp.bfloat16

N_HEADS = 16
HEAD_DIM = 64
ATTN_WIDTH = N_HEADS * HEAD_DIM
POOL_WINDOWS = (2, 4, 8, 16)
POOL_GROUPS = len(POOL_WINDOWS)
POOL_STATE = max(POOL_WINDOWS) - 1
POOL_HALO = 16
RMS_EPS = 1e-6

LANES = 128
HEADS_PER_BLOCK = LANES // HEAD_DIM
VMEM_LIMIT = 52 * 1024 * 1024
ATTN_TILE = 512
CUMSUM_GROUP = 256
BIAS_TERMS = 3
BIAS_ROWS = 16
DECODE_PAGES_PER_STEP = 8
LOG2E = 1.4426950408889634

NT_DIMS = (((1,), (1,)), ((), ()))


def _cparams(*sem):
    return pltpu.CompilerParams(dimension_semantics=sem, vmem_limit_bytes=VMEM_LIMIT)


def _row_tile(m, cap):
    t = min(m, cap)
    while m % t:
        t //= 2
    return t


def _rms_to_scratch(x_ref, g_ref, xn_ref):
    x = x_ref[...]
    ms = jnp.mean(x * x, axis=-1, keepdims=True)
    xn_ref[...] = ((x * lax.rsqrt(ms + RMS_EPS)) * g_ref[...]).astype(BF16)


def _inproj_kernel(x_ref, g_ref, w_ref, *rest, mode, transposed):
    if mode == "headnorm":
        bd_ref, hg_ref, o_ref, xn_ref = rest
    else:
        o_ref, xn_ref = rest

    @pl.when(pl.program_id(1) == 0)
    def _():
        _rms_to_scratch(x_ref, g_ref, xn_ref)

    h = jnp.dot(xn_ref[...], w_ref[...], preferred_element_type=F32)
    if mode == "headnorm":
        ss = jnp.dot((h * h).astype(BF16), bd_ref[...], preferred_element_type=F32)
        h = (h * lax.rsqrt(ss * (1.0 / HEAD_DIM) + RMS_EPS)) * hg_ref[...]
    elif mode == "sigmoid":
        h = 1.0 / (1.0 + jnp.exp(-h))
    if transposed:
        o_ref[...] = h.T.reshape(o_ref.shape).astype(o_ref.dtype)
    else:
        o_ref[...] = h.astype(o_ref.dtype)


def _inproj(x2d, g, w, col0, ncols, *, mode, out_dtype, tn, head_gain=None, seq=None, tm_cap=1024):
    m, d = x2d.shape
    tm = _row_tile(m if seq is None else seq, tm_cap)
    assert ncols % tn == 0 and col0 % tn == 0
    jb = col0 // tn
    in_specs = [
        pl.BlockSpec((tm, d), lambda i, j: (i, 0)),
        pl.BlockSpec((1, d), lambda i, j: (0, 0)),
        pl.BlockSpec((d, tn), lambda i, j: (0, j + jb)),
    ]
    args = [x2d, g.reshape(1, d), w]
    if mode == "headnorm":
        hid = jnp.arange(tn, dtype=jnp.int32) // HEAD_DIM
        bd = (hid[:, None] == hid[None, :]).astype(BF16)
        in_specs += [pl.BlockSpec((tn, tn), lambda i, j: (0, 0)),
                     pl.BlockSpec((1, tn), lambda i, j: (0, 0))]
        args += [bd, jnp.tile(head_gain.astype(F32), tn // HEAD_DIM).reshape(1, tn)]
    if seq is None:
        out_spec = pl.BlockSpec((tm, tn), lambda i, j: (i, j))
        out_shape = jax.ShapeDtypeStruct((m, ncols), out_dtype)
    else:
        nt, hpt = seq // tm, tn // HEAD_DIM
        out_spec = pl.BlockSpec((None, hpt, HEAD_DIM, tm), lambda i, j: (i // nt, j, 0, i % nt))
        out_shape = jax.ShapeDtypeStruct((m // seq, ncols // HEAD_DIM, HEAD_DIM, seq), out_dtype)
    return pl.pallas_call(
        functools.partial(_inproj_kernel, mode=mode, transposed=seq is not None),
        grid=(m // tm, ncols // tn),
        in_specs=in_specs,
        out_specs=out_spec,
        out_shape=out_shape,
        scratch_shapes=[pltpu.VMEM((tm, d), BF16)],
        compiler_params=_cparams("parallel", "arbitrary"),
        name="inproj_" + mode + ("_t" if seq is not None else ""),
    )(*args)


def _cumsum_rhs(g):
    r = lax.broadcasted_iota(jnp.int32, (g, g + LANES), 0)
    c = lax.broadcasted_iota(jnp.int32, (g, g + LANES), 1)
    return jnp.where((r > c) | (c >= g), 1.0, 0.0).astype(BF16)


def _sb_block(z, carry, u, mask, vt):
    g = u.shape[0]
    neg_abs = lax.bitcast_convert_type(lax.bitcast_convert_type(z, jnp.uint32) | jnp.uint32(1 << 31), F32)
    sp = jnp.maximum(z, 0.0) + jnp.log(1.0 + jnp.exp2(neg_abs)) * LOG2E
    if mask is not None:
        sp = jnp.where(mask, sp, 0.0)
    log_beta = z - sp
    sp16 = sp.astype(BF16)
    parts = []
    for c in reversed(range(z.shape[1] // g)):
        cols = slice(c * g, (c + 1) * g)
        cum = jnp.dot(sp16[:, cols], u, preferred_element_type=F32)
        for l in reversed(range(0, g, LANES)):
            parts.append(jnp.exp2(log_beta[:, c * g + l:c * g + l + LANES] - cum[:, l:l + LANES] - carry))
        carry = carry + cum[:, g:]
    a = parts[0] if len(parts) == 1 else jnp.concatenate(parts[::-1], axis=1)
    if mask is not None:
        a = jnp.where(mask, a, 0.0)
    pv = lax.dot_general(a.astype(BF16), vt, NT_DIMS, preferred_element_type=F32)
    return carry, pv


def _attn_prompt_kernel(bias_ref, q_ref, kt_ref, vt_ref, o_ref, qa_ref, kz_ref, vz_ref, u_ref, carry_ref, acc_ref,
                        *, seq, tile):
    pair = pl.program_id(1)
    nblk = seq // tile
    u_ref[...] = _cumsum_rhs(u_ref.shape[0])
    qa_ref[:, pl.ds(0, LANES)] = q_ref[...]
    qa_ref[:, pl.ds(LANES, LANES)] = jnp.ones((seq, LANES), BF16)
    kz_ref[...] = jnp.zeros(kz_ref.shape, BF16)
    vz_ref[...] = jnp.zeros(vz_ref.shape, BF16)
    term_row = lax.broadcasted_iota(jnp.int32, (BIAS_ROWS, tile), 0)
    for h in range(HEADS_PER_BLOCK):
        rows = pl.ds(h * HEAD_DIM, HEAD_DIM)
        rest = jnp.full((BIAS_ROWS, tile), bias_ref[pair * HEADS_PER_BLOCK + h], F32)
        terms = jnp.zeros((BIAS_ROWS, tile), F32)
        for i in range(BIAS_TERMS):
            term = rest.astype(BF16).astype(F32)
            terms = jnp.where(term_row == i, term, terms)
            rest = rest - term
        for cb in range(nblk):
            cols = pl.ds(cb * tile, tile)
            kz_ref[h, cb, rows, :] = kt_ref[h, :, cols].astype(BF16)
            kz_ref[h, cb, pl.ds(LANES, BIAS_ROWS), :] = terms.astype(BF16)
            vz_ref[h, cb, rows, :] = vt_ref[h, :, cols].astype(BF16)

    rq = lax.broadcasted_iota(jnp.int32, (tile, tile), 0)
    ck = lax.broadcasted_iota(jnp.int32, (tile, tile), 1)
    diag_mask = ck < rq

    def qbody(qb, _):
        qoff = pl.multiple_of(qb * tile, tile)
        qblk = qa_ref[pl.ds(qoff, tile), :]

        def step(kb, mask, first):
            for h in range(HEADS_PER_BLOCK):
                z = jnp.dot(qblk, kz_ref[h, kb], preferred_element_type=F32)
                carry = jnp.zeros((tile, LANES), F32) if first else carry_ref[h]
                carry, pv = _sb_block(z, carry, u_ref[...], mask, vz_ref[h, kb])
                carry_ref[h] = carry
                if first:
                    acc_ref[h] = pv
                else:
                    acc_ref[h] += pv

        step(qb, diag_mask, True)

        def kbody(i, _):
            step(qb - 1 - i, None, False)
            return 0

        lax.fori_loop(0, qb, kbody, 0)
        out = acc_ref[0]
        for h in range(1, HEADS_PER_BLOCK):
            out = out + acc_ref[h]
        o_ref[pl.ds(qoff, tile), :] = out.astype(o_ref.dtype)
        return 0

    lax.fori_loop(0, nblk, qbody, 0)


def _attn_prompt(q, kt, vt, bias, batch, seq):
    m, w = q.shape
    npairs = w // LANES
    tile = _row_tile(seq, ATTN_TILE)
    group = min(CUMSUM_GROUP, tile)
    kv_spec =pl.BlockSpec((None, HEADS_PER_BLOCK, HEAD_DIM, seq), lambda b, p: (b, p, 0, 0))
    return pl.pallas_call(
        functools.partial(_attn_prompt_kernel, seq=seq, tile=tile),
        grid=(batch, npairs),
        in_specs=[
            pl.BlockSpec(memory_space=pltpu.SMEM),
            pl.BlockSpec((seq, LANES), lambda b, p: (b, p)),
            kv_spec,
            kv_spec,
        ],
        out_specs=pl.BlockSpec((seq, LANES), lambda b, p: (b, p)),
        out_shape=jax.ShapeDtypeStruct((m, w), BF16),
        scratch_shapes=[
            pltpu.VMEM((seq, 2 * LANES), BF16),
            pltpu.VMEM((HEADS_PER_BLOCK, seq // tile, 2 * LANES, tile), BF16),
            pltpu.VMEM((HEADS_PER_BLOCK, seq // tile, LANES, tile), BF16),
            pltpu.VMEM((group, group + LANES), BF16),
            pltpu.VMEM((HEADS_PER_BLOCK, tile, LANES), F32),
            pltpu.VMEM((HEADS_PER_BLOCK, tile, LANES), F32),
        ],
        compiler_params=_cparams("parallel", "arbitrary"),
        name="attn_prompt",
    )(bias, q, kt, vt)


def _attn_decode_kernel(pt_ref, q_ref, bias_ref, kn_ref, vn_ref, *rest, t_new, pages_per_step):
    kc_refs = rest[:pages_per_step]
    vc_refs = rest[pages_per_step:2 * pages_per_step]
    o_ref, acc_ref, carry_ref, u_ref = rest[2 * pages_per_step:]
    j = pl.program_id(1)
    rows, slots = bias_ref.shape

    @pl.when(j == 0)
    def _():
        u_ref[...] = _cumsum_rhs(slots)
        r = lax.broadcasted_iota(jnp.int32, (rows, slots), 0)
        c = lax.broadcasted_iota(jnp.int32, (rows, slots), 1)
        mask = c < (r // N_HEADS)
        s = jnp.dot(q_ref[...], kn_ref[...].astype(BF16), preferred_element_type=F32)
        carry, pv = _sb_block(s + bias_ref[...], jnp.zeros((rows, LANES), F32), u_ref[...], mask,
                              vn_ref[...].astype(BF16))
        carry_ref[...] = carry
        acc_ref[...] = pv

    @pl.when(j > 0)
    def _():
        carry = carry_ref[...]
        total = None
        for kc_ref, vc_ref in zip(kc_refs, vc_refs):
            s = jnp.dot(q_ref[...], kc_ref[...].astype(BF16), preferred_element_type=F32)
            carry, pv = _sb_block(s + bias_ref[...], carry, u_ref[...], None, vc_ref[...].astype(BF16))
            total = pv if total is None else total + pv
        carry_ref[...] = carry
        acc_ref[...] += total

    @pl.when(j == pl.num_programs(1) - 1)
    def _():
        r = lax.broadcasted_iota(jnp.int32, acc_ref.shape, 0)
        l = lax.broadcasted_iota(jnp.int32, acc_ref.shape, 1)
        own = jnp.where((r % N_HEADS) == (l // HEAD_DIM), acc_ref[...], 0.0)
        for t in range(t_new):
            o_ref[pl.ds(t, 1), :] = jnp.sum(own[t * N_HEADS:(t + 1) * N_HEADS], axis=0, keepdims=True)


def _attn_decode(q_rows, bias_rows, kt_new, vt_new, cache_kt, cache_vt, page_table, t_new):
    nb, rows, w = q_rows.shape
    n_pages = page_table.shape[1]
    slots = cache_kt.shape[2]
    pps = max(p for p in (DECODE_PAGES_PER_STEP, 4, 2, 1) if n_pages % p == 0)

    def page_idx(r):
        return lambda b, j, pt: (pt[b, n_pages - 1 - (jnp.maximum(j, 1) - 1) * pps - r], 0, 0)

    per_b = lambda b, j, pt: (b, 0, 0)
    page_specs = [pl.BlockSpec((None, w, slots), page_idx(r)) for r in range(pps)]
    grid_spec = pltpu.PrefetchScalarGridSpec(
        num_scalar_prefetch=1,
        grid=(nb, n_pages // pps + 1),
        in_specs=[
            pl.BlockSpec((None, rows, w), per_b),
            pl.BlockSpec((rows, slots), lambda b, j, pt: (0, 0)),
            pl.BlockSpec((None, w, slots), per_b),
            pl.BlockSpec((None, w, slots), per_b),
        ] + page_specs + page_specs,
        out_specs=pl.BlockSpec((None, t_new, w), per_b),
        scratch_shapes=[
            pltpu.VMEM((rows, w), F32),
            pltpu.VMEM((rows, LANES), F32),
            pltpu.VMEM((slots, slots + LANES), BF16),
        ],
    )
    return pl.pallas_call(
        functools.partial(_attn_decode_kernel, t_new=t_new, pages_per_step=pps),
        grid_spec=grid_spec,
        out_shape=jax.ShapeDtypeStruct((nb, t_new, w), F32),
        compiler_params=_cparams("parallel", "arbitrary"),
        name="attn_decode",
    )(page_table, q_rows, bias_rows, kt_new, vt_new, *([cache_kt] * pps), *([cache_vt] * pps))


def _pool_tile(ext_ref, p_ref, tp, pos0):
    gc = ext_ref.shape[1] // POOL_GROUPS
    pos = pos0 + lax.broadcasted_iota(jnp.int32, (tp, 1), 0)
    for g, win in enumerate(POOL_WINDOWS):
        cols = slice(g * gc, (g + 1) * gc)
        u_new = ext_ref[pl.ds(POOL_HALO, tp), cols]
        win_sum = u_new
        for d in range(1, win):
            win_sum = win_sum + ext_ref[pl.ds(POOL_HALO - d, tp), cols]
        cnt = jnp.minimum(win, pos + 1).astype(F32)
        p_ref[:, cols] = (win_sum / cnt - u_new).astype(p_ref.dtype)


def _pool_prompt_kernel(u_ref, halo_ref, p_ref, ext_ref, *, tp):
    i = pl.program_id(1)
    ext_ref[pl.ds(0, POOL_HALO), :] = jnp.where(i == 0, 0.0, halo_ref[...])
    ext_ref[pl.ds(POOL_HALO, tp), :] = u_ref[...]
    _pool_tile(ext_ref, p_ref, tp, i * tp)


def _pool_prompt(u, batch, seq, tp_cap=512):
    m, w = u.shape
    tp = _row_tile(seq, tp_cap)
    nt = seq // tp
    hb = tp // POOL_HALO
    return pl.pallas_call(
        functools.partial(_pool_prompt_kernel, tp=tp),
        grid=(batch, nt),
        in_specs=[
            pl.BlockSpec((tp, w), lambda b, i: (b * nt + i, 0)),
            pl.BlockSpec((POOL_HALO, w), lambda b, i: (jnp.maximum((b * nt + i) * hb - 1, 0), 0)),
        ],
        out_specs=pl.BlockSpec((tp, w), lambda b, i: (b * nt + i, 0)),
        out_shape=jax.ShapeDtypeStruct((m, w), BF16),
        scratch_shapes=[pltpu.VMEM((POOL_HALO + tp, w), F32)],
        compiler_params=_cparams("parallel", "arbitrary"),
        name="pool_prompt",
    )(u, u)


def _pool_sample_kernel(u_ref, halo_ref, p_ref, ext_ref, *, tp, pos0):
    ext_ref[pl.ds(0, POOL_HALO), :] = halo_ref[...]
    ext_ref[pl.ds(POOL_HALO, tp), :] = u_ref[...]
    _pool_tile(ext_ref, p_ref, tp, pos0)


def _pool_sample(u3, halo3, pos0):
    nb, tp, w = u3.shape
    return pl.pallas_call(
        functools.partial(_pool_sample_kernel, tp=tp, pos0=pos0),
        grid=(nb,),
        in_specs=[
            pl.BlockSpec((None, tp, w), lambda b: (b, 0, 0)),
            pl.BlockSpec((None, POOL_HALO, w), lambda b: (b, 0, 0)),
        ],
        out_specs=pl.BlockSpec((None, tp, w), lambda b: (b, 0, 0)),
        out_shape=jax.ShapeDtypeStruct((nb, tp, w), F32),
        scratch_shapes=[pltpu.VMEM((POOL_HALO + tp, w), F32)],
        compiler_params=_cparams("parallel"),
        name="pool_sample",
    )(u3, halo3)


def _merge_kernel(o_ref, wap_ref, p_ref, wp_ref, ps_ref, ga_ref, gb_ref, m_ref):
    a = jnp.dot(o_ref[...], wap_ref[...], preferred_element_type=F32)
    op = jnp.dot(p_ref[...], wp_ref[...], preferred_element_type=F32) * ps_ref[...]
    m_ref[...] = (ga_ref[...].astype(F32) * a + gb_ref[...].astype(F32) * op).astype(m_ref.dtype)


def _merge(o, w_ap, p, w_pool, pool_scale, gates, tm_cap=1024):
    m, aw = o.shape
    d = w_ap.shape[1]
    g, gc, tn = w_pool.shape
    assert g * tn == d and g * gc == p.shape[1]
    tm = _row_tile(m, tm_cap)
    return pl.pallas_call(
        _merge_kernel,
        grid=(m // tm, g),
        in_specs=[
            pl.BlockSpec((tm, aw), lambda i, j: (i, 0)),
            pl.BlockSpec((aw, tn), lambda i, j: (0, j)),
            pl.BlockSpec((tm, gc), lambda i, j: (i, j)),
            pl.BlockSpec((None, gc, tn), lambda i, j: (j, 0, 0)),
            pl.BlockSpec((1, tn), lambda i, j: (0, j)),
            pl.BlockSpec((tm, tn), lambda i, j: (i, j)),
            pl.BlockSpec((tm, tn), lambda i, j: (i, j + g)),
        ],
        out_specs=pl.BlockSpec((tm, tn), lambda i, j: (i, j)),
        out_shape=jax.ShapeDtypeStruct((m, d), BF16),
        compiler_params=_cparams("parallel", "arbitrary"),
        name="merge",
    )(o, w_ap, p, w_pool, pool_scale.reshape(1, d), gates, gates)


def _resid_mm_kernel(a_ref, w_ref, r_ref, o_ref):
    o_ref[...] = r_ref[...] + jnp.dot(a_ref[...], w_ref[...], preferred_element_type=F32)


def _resid_mm(a, w, resid, tm_cap, tn=512):
    m, kd = a.shape
    n = w.shape[1]
    tm = _row_tile(m, tm_cap)
    return pl.pallas_call(
        _resid_mm_kernel,
        grid=(m // tm, n // tn),
        in_specs=[
            pl.BlockSpec((tm, kd), lambda i, j: (i, 0)),
            pl.BlockSpec((kd, tn), lambda i, j: (0, j)),
            pl.BlockSpec((tm, tn), lambda i, j: (i, j)),
        ],
        out_specs=pl.BlockSpec((tm, tn), lambda i, j: (i, j)),
        out_shape=jax.ShapeDtypeStruct((m, n), F32),
        compiler_params=_cparams("parallel", "arbitrary"),
        name="resid_mm",
    )(a, w, resid)


def _gate_up_kernel(h_ref, g_ref, wg_ref, wu_ref, o_ref, xn_ref):
    @pl.when(pl.program_id(1) == 0)
    def _():
        _rms_to_scratch(h_ref, g_ref, xn_ref)

    xn = xn_ref[...]
    gate = jnp.dot(xn, wg_ref[...], preferred_element_type=F32)
    up = jnp.dot(xn, wu_ref[...], preferred_element_type=F32)
    o_ref[...] = ((gate / (1.0 + jnp.exp(-gate))) * up).astype(o_ref.dtype)


def _gate_up(h, g, w_gu, tm_cap=1024, tn=512):
    m, d = h.shape
    dff = w_gu.shape[1] // 2
    nj = dff // tn
    tm = _row_tile(m, tm_cap)
    return pl.pallas_call(
        _gate_up_kernel,
        grid=(m // tm, nj),
        in_specs=[
            pl.BlockSpec((tm, d), lambda i, j: (i, 0)),
            pl.BlockSpec((1, d), lambda i, j: (0, 0)),
            pl.BlockSpec((d, tn), lambda i, j: (0, j)),
            pl.BlockSpec((d, tn), lambda i, j: (0, j + nj)),
        ],
        out_specs=pl.BlockSpec((tm, tn), lambda i, j: (i, j)),
        out_shape=jax.ShapeDtypeStruct((m, dff), BF16),
        scratch_shapes=[pltpu.VMEM((tm, d), BF16)],
        compiler_params=_cparams("parallel", "arbitrary"),
        name="gate_up",
    )(h, g.reshape(1, d), w_gu, w_gu)


def _project(x2d, wts, seq=None):
    d = x2d.shape[1]
    aw, pw = ATTN_WIDTH, d // 2
    w_in, g1 = wts["w_in"], wts["norm1_g"]
    q = _inproj(x2d, g1, w_in, 0, aw, mode="headnorm", out_dtype=BF16, tn=256,
                head_gain=wts["q_norm_g"] * (HEAD_DIM ** -0.5 * LOG2E))
    k = _inproj(x2d, g1, w_in, aw, aw, mode="headnorm", out_dtype=F32, tn=256, head_gain=wts["k_norm_g"],
                seq=seq)
    v = _inproj(x2d, g1, w_in, 2 * aw, aw, mode="plain", out_dtype=F32, tn=256, seq=seq)
    u = _inproj(x2d, g1, w_in, 3 * aw, pw, mode="plain", out_dtype=F32, tn=512)
    gates = _inproj(x2d, g1, w_in, 3 * aw + pw, 2 * d, mode="sigmoid", out_dtype=BF16, tn=512)
    return q, k, v, u, gates


def _mix_and_ffn(x2d, o_attn, p, gates, wts):
    mixed = _merge(o_attn, wts["w_attn_proj"], p, wts["w_pool"], wts["pool_scale"], gates)
    h = _resid_mm(mixed, wts["w_out"], x2d, tm_cap=1024)
    act = _gate_up(h, wts["norm2_g"], wts["w_gate_up"])
    return _resid_mm(act, wts["w_down"], h, tm_cap=512)


def kernel(x_prompt, x_sample, cache_k, cache_v, state_pool, page_table, norm1_g, w_in,
           q_norm_g, k_norm_g, sb_bias, w_attn_proj, w_pool, pool_scale, w_out, norm2_g, w_gate_up, w_down):
    b_p, seq, d = x_prompt.shape
    b_s, t_s = x_sample.shape[:2]
    depth, n_pool, page = cache_k.shape[:3]
    n_pages = page_table.shape[1]
    past_len = n_pages * page
    pw = d // 2

    xp = x_prompt.reshape(b_p * seq, d)
    xs = x_sample.reshape(b_s * t_s, d)
    outs = {name: [] for name in ("kp", "vp", "up", "ks", "vs", "us")}
    for l in range(depth):
        wts = {
            "norm1_g": norm1_g[l], "w_in": w_in[l].astype(BF16), "q_norm_g": q_norm_g[l],
            "k_norm_g": k_norm_g[l], "w_attn_proj": w_attn_proj[l].astype(BF16),
            "w_pool": w_pool[l].astype(BF16), "pool_scale": pool_scale[l], "w_out": w_out[l].astype(BF16),
            "norm2_g": norm2_g[l], "w_gate_up": w_gate_up[l].astype(BF16), "w_down": w_down[l].astype(BF16),
        }
        bias = sb_bias[l].astype(F32) * LOG2E

        q, kt, vt, u, gates = _project(xp, wts, seq=seq)
        o_attn = _attn_prompt(q, kt, vt, bias, b_p, seq)
        p = _pool_prompt(u, b_p, seq)
        outs["kp"].append(kt.transpose(0, 3, 1, 2))
        outs["vp"].append(vt.transpose(0, 3, 1, 2))
        outs["up"].append(u.reshape(b_p, seq, pw)[:, seq - POOL_STATE:])
        xp = _mix_and_ffn(xp, o_attn, p, gates, wts)

        q, k, v, u, gates = _project(xs, wts)
        q4 = q.reshape(b_s, t_s, 1, N_HEADS, HEAD_DIM)
        eye = jnp.eye(N_HEADS, dtype=BF16).reshape(1, 1, N_HEADS, N_HEADS, 1)
        q_rows = (q4 * eye).reshape(b_s, t_s * N_HEADS, ATTN_WIDTH)
        bias_rows = jnp.broadcast_to(jnp.tile(bias, t_s)[:, None], (t_s * N_HEADS, page))
        pad = ((0, 0), (0, 0), (0, page - t_s))
        kt_new = jnp.pad(k.reshape(b_s, t_s, ATTN_WIDTH).transpose(0, 2, 1), pad)
        vt_new = jnp.pad(v.reshape(b_s, t_s, ATTN_WIDTH).transpose(0, 2, 1), pad)
        cache_kt = cache_k[l].transpose(0, 2, 3, 1).reshape(n_pool, ATTN_WIDTH, page)
        cache_vt = cache_v[l].transpose(0, 2, 3, 1).reshape(n_pool, ATTN_WIDTH, page)
        o_attn = _attn_decode(q_rows, bias_rows, kt_new, vt_new, cache_kt, cache_vt, page_table, t_s)
        o_attn = o_attn.reshape(b_s * t_s, ATTN_WIDTH).astype(BF16)
        u3 = u.reshape(b_s, t_s, pw)
        halo = jnp.pad(state_pool[l], ((0, 0), (POOL_HALO - POOL_STATE, 0), (0, 0)))
        p = _pool_sample(u3, halo, past_len).reshape(b_s * t_s, pw).astype(BF16)
        outs["ks"].append(k.reshape(b_s, t_s, N_HEADS, HEAD_DIM))
        outs["vs"].append(v.reshape(b_s, t_s, N_HEADS, HEAD_DIM))
        outs["us"].append(jnp.concatenate([state_pool[l], u3], axis=1)[:, -POOL_STATE:])
        xs = _mix_and_ffn(xs, o_attn, p, gates, wts)

    st = lambda name: jnp.stack(outs[name], axis=0)
    return (xp.reshape(b_p, seq, d), xs.reshape(b_s, t_s, d),
            st("kp"), st("vp"), st("up"), st("ks"), st("vs"), st("us"))
```

```python
import functools
from typing import NamedTuple

import jax
import jax.numpy as jnp
from jax import lax
from jax.experimental import pallas as pl
from jax.experimental.pallas import tpu as pltpu

F32 = jnp.float32
BF16 = jnp.bfloat16

N_HEADS = 16
HEAD_DIM = 64
ATTN_WIDTH = N_HEADS * HEAD_DIM
POOL_WINDOWS = (2, 4, 8, 16)
POOL_GROUPS = len(POOL_WINDOWS)
POOL_STATE = max(POOL_WINDOWS) - 1
POOL_HALO = 16
RMS_EPS = 1e-6

LANES = 128
HEADS_PER_BLOCK = LANES // HEAD_DIM
VMEM_LIMIT = 52 * 1024 * 1024
ATTN_TILE = 512
CUMSUM_GROUP = 256
BIAS_TERMS = 3
BIAS_ROWS = 16
DECODE_PAGES_PER_STEP = 8
LOG2E = 1.4426950408889634

NT_DIMS = (((1,), (1,)), ((), ()))


def _cparams(*sem):
    return pltpu.CompilerParams(dimension_semantics=sem, vmem_limit_bytes=VMEM_LIMIT)


def _row_tile(m, cap):
    t = min(m, cap)
    while m % t:
        t //= 2
    return t


def _rmsnorm_kernel(x_ref, g_ref, o_ref):
    x = x_ref[...]
    ms = jnp.mean(x * x, axis=-1, keepdims=True)
    o_ref[...] = ((x * lax.rsqrt(ms + RMS_EPS)) * g_ref[...]).astype(o_ref.dtype)


def _rmsnorm(x2d, g, tm_cap=512):
    m, d = x2d.shape
    tm = _row_tile(m, tm_cap)
    return pl.pallas_call(
        _rmsnorm_kernel,
        grid=(m // tm,),
        in_specs=[pl.BlockSpec((tm, d), lambda i: (i, 0)), pl.BlockSpec((1, d), lambda i: (0, 0))],
        out_specs=pl.BlockSpec((tm, d), lambda i: (i, 0)),
        out_shape=jax.ShapeDtypeStruct((m, d), BF16),
        compiler_params=_cparams("parallel"),
        name="rmsnorm",
    )(x2d, g.reshape(1, d))


class _Rows(NamedTuple):
    lhs: tuple
    extras: tuple = ()
    head_major_seq: int = 0


def _ws_kernel(*refs, lhs_of_w, n_const, sets, epilogue):
    n_w = len(lhs_of_w)
    w_refs, refs = refs[:n_w], refs[n_w:]
    const_refs, refs = refs[:n_const], refs[n_const:]
    set_refs = []
    for n_lhs, n_extra, _ in sets:
        set_refs.append((refs[:n_lhs], refs[n_lhs:n_lhs + n_extra]))
        refs = refs[n_lhs + n_extra:]
    out_refs, wb_refs = refs[:len(sets)], refs[len(sets):]

    def compute(lhs_refs, extra_refs, o_ref, head_major):
        accs = [jnp.dot(lhs_refs[lhs_of_w[k]][...], wb_refs[k][...], preferred_element_type=F32)
                for k in range(n_w)]
        r = epilogue(accs, [e[...] for e in extra_refs], [c[...] for c in const_refs])
        if head_major:
            r = r.T.reshape(o_ref.shape)
        o_ref[...] = r.astype(o_ref.dtype)

    @pl.when(pl.program_id(1) == 0)
    def _():
        for w_ref, wb_ref in zip(w_refs, wb_refs):
            wb_ref[...] = w_ref[...].astype(BF16)
        for (lhs_refs, extra_refs), o_ref, (_, _, head_major) in zip(set_refs[1:], out_refs[1:], sets[1:]):
            compute(lhs_refs, extra_refs, o_ref, head_major)

    compute(*set_refs[0], out_refs[0], sets[0][2])


def _ws_matmul(name, weights, lhs_of_w, consts, row_sets, epilogue, n_cols, tn, out_dtype, tm_cap):
    assert n_cols % tn == 0
    m0 = row_sets[0].lhs[0][0].shape[0]
    tm0 = _row_tile(row_sets[0].head_major_seq or m0, tm_cap)
    in_specs, args = [], []
    for w, kb, idx in weights:
        in_specs.append(pl.BlockSpec((kb, tn), lambda j, i, idx=idx: idx(j)))
        args.append(w)
    for c, shape, idx in consts:
        in_specs.append(pl.BlockSpec(shape, lambda j, i, idx=idx: idx(j)))
        args.append(c)
    out_specs, out_shapes, sets = [], [], []
    for s, rows in enumerate(row_sets):
        m = rows.lhs[0][0].shape[0]
        tm = tm0 if s == 0 else m
        row = (lambda i: i) if s == 0 else (lambda i: 0)
        for arr, kb, kidx in rows.lhs:
            in_specs.append(pl.BlockSpec((tm, kb), lambda j, i, row=row, kidx=kidx: (row(i), kidx(j))))
            args.append(arr)
        for arr, cidx in rows.extras:
            in_specs.append(pl.BlockSpec((tm, tn), lambda j, i, row=row, cidx=cidx: (row(i), cidx(j))))
            args.append(arr)
        if rows.head_major_seq:
            assert s == 0
            seq = rows.head_major_seq
            nt, hpt = seq // tm, tn // HEAD_DIM
            out_specs.append(pl.BlockSpec((None, hpt, HEAD_DIM, tm), lambda j, i, nt=nt: (i // nt, j, 0, i % nt)))
            out_shapes.append(jax.ShapeDtypeStruct((m // seq, n_cols // HEAD_DIM, HEAD_DIM, seq), out_dtype))
        else:
            out_specs.append(pl.BlockSpec((tm, tn), lambda j, i, row=row: (row(i), j)))
            out_shapes.append(jax.ShapeDtypeStruct((m, n_cols), out_dtype))
        sets.append((len(rows.lhs), len(rows.extras), bool(rows.head_major_seq)))
    return pl.pallas_call(
        functools.partial(_ws_kernel, lhs_of_w=tuple(lhs_of_w), n_const=len(consts), sets=tuple(sets),
                          epilogue=epilogue),
        grid=(n_cols // tn, m0 // tm0),
        in_specs=in_specs,
        out_specs=out_specs,
        out_shape=out_shapes,
        scratch_shapes=[pltpu.VMEM((kb, tn), BF16) for _, kb, _ in weights],
        compiler_params=_cparams("parallel", "arbitrary"),
        name=name,
    )(*args)


def _epi_plain(accs, extras, consts):
    return accs[0]


def _epi_headnorm(accs, extras, consts):
    h, (bd, gain) = accs[0], consts
    ss = jnp.dot((h * h).astype(BF16), bd, preferred_element_type=F32)
    return (h * lax.rsqrt(ss * (1.0 / HEAD_DIM) + RMS_EPS)) * gain


def _epi_sigmoid(accs, extras, consts):
    return 1.0 / (1.0 + jnp.exp(-accs[0]))


def _epi_merge(accs, extras, consts):
    (a, op), (ga, gb), (pool_scale,) = accs, extras, consts
    return ga.astype(F32) * a + gb.astype(F32) * (op * pool_scale)


def _epi_residual(accs, extras, consts):
    return extras[0] + accs[0]


def _epi_swiglu(accs, extras, consts):
    gate, up = accs
    return (gate / (1.0 + jnp.exp(-gate))) * up


def _cumsum_rhs(g):
    r = lax.broadcasted_iota(jnp.int32, (g, g + LANES), 0)
    c = lax.broadcasted_iota(jnp.int32, (g, g + LANES), 1)
    return jnp.where((r > c) | (c >= g), 1.0, 0.0).astype(BF16)


def _sb_block(z, carry, u, mask, vt):
    g = u.shape[0]
    neg_abs = lax.bitcast_convert_type(lax.bitcast_convert_type(z, jnp.uint32) | jnp.uint32(1 << 31), F32)
    sp = jnp.maximum(z, 0.0) + jnp.log(1.0 + jnp.exp2(neg_abs)) * LOG2E
    if mask is not None:
        sp = jnp.where(mask, sp, 0.0)
    log_beta = z - sp
    sp16 = sp.astype(BF16)
    parts = []
    for c in reversed(range(z.shape[1] // g)):
        cols = slice(c * g, (c + 1) * g)
        cum = jnp.dot(sp16[:, cols], u, preferred_element_type=F32)
        for l in reversed(range(0, g, LANES)):
            parts.append(jnp.exp2(log_beta[:, c * g + l:c * g + l + LANES] - cum[:, l:l + LANES] - carry))
        carry = carry + cum[:, g:]
    a = parts[0] if len(parts) == 1 else jnp.concatenate(parts[::-1], axis=1)
    if mask is not None:
        a = jnp.where(mask, a, 0.0)
    pv = lax.dot_general(a.astype(BF16), vt, NT_DIMS, preferred_element_type=F32)
    return carry, pv


def _attn_prompt_kernel(bias_ref, q_ref, kt_ref, vt_ref, o_ref, qa_ref, kz_ref, vz_ref, u_ref, carry_ref, acc_ref,
                        *, seq, tile):
    pair = pl.program_id(1)
    nblk = seq // tile
    u_ref[...] = _cumsum_rhs(u_ref.shape[0])
    qa_ref[:, pl.ds(0, LANES)] = q_ref[...]
    qa_ref[:, pl.ds(LANES, LANES)] = jnp.ones((seq, LANES), BF16)
    kz_ref[...] = jnp.zeros(kz_ref.shape, BF16)
    vz_ref[...] = jnp.zeros(vz_ref.shape, BF16)
    term_row = lax.broadcasted_iota(jnp.int32, (BIAS_ROWS, tile), 0)
    for h in range(HEADS_PER_BLOCK):
        rows = pl.ds(h * HEAD_DIM, HEAD_DIM)
        rest = jnp.full((BIAS_ROWS, tile), bias_ref[pair * HEADS_PER_BLOCK + h], F32)
        terms = jnp.zeros((BIAS_ROWS, tile), F32)
        for i in range(BIAS_TERMS):
            term = rest.astype(BF16).astype(F32)
            terms = jnp.where(term_row == i, term, terms)
            rest = rest - term
        for cb in range(nblk):
            cols = pl.ds(cb * tile, tile)
            kz_ref[h, cb, rows, :] = kt_ref[h, :, cols].astype(BF16)
            kz_ref[h, cb, pl.ds(LANES, BIAS_ROWS), :] = terms.astype(BF16)
            vz_ref[h, cb, rows, :] = vt_ref[h, :, cols].astype(BF16)

    rq = lax.broadcasted_iota(jnp.int32, (tile, tile), 0)
    ck = lax.broadcasted_iota(jnp.int32, (tile, tile), 1)
    diag_mask = ck < rq

    def qbody(qb, _):
        qoff = pl.multiple_of(qb * tile, tile)
        qblk = qa_ref[pl.ds(qoff, tile), :]

        def step(kb, mask, first):
            for h in range(HEADS_PER_BLOCK):
                z = jnp.dot(qblk, kz_ref[h, kb], preferred_element_type=F32)
                carry = jnp.zeros((tile, LANES), F32) if first else carry_ref[h]
                carry, pv = _sb_block(z, carry, u_ref[...], mask, vz_ref[h, kb])
                carry_ref[h] = carry
                if first:
                    acc_ref[h] = pv
                else:
                    acc_ref[h] += pv

        step(qb, diag_mask, True)

        def kbody(i, _):
            step(qb - 1 - i, None, False)
            return 0

        lax.fori_loop(0, qb, kbody, 0)
        out = acc_ref[0]
        for h in range(1, HEADS_PER_BLOCK):
            out = out + acc_ref[h]
        o_ref[pl.ds(qoff, tile), :] = out.astype(o_ref.dtype)
        return 0

    lax.fori_loop(0, nblk, qbody, 0)


def _attn_prompt(q, kt, vt, bias, batch, seq):
    m, w = q.shape
    npairs = w // LANES
    tile = _row_tile(seq, ATTN_TILE)
    group = min(CUMSUM_GROUP, tile)
    kv_spec = pl.BlockSpec((None, HEADS_PER_BLOCK, HEAD_DIM, seq), lambda b, p: (b, p, 0, 0))
    return pl.pallas_call(
        functools.partial(_attn_prompt_kernel, seq=seq, tile=tile),
        grid=(batch, npairs),
        in_specs=[
            pl.BlockSpec(memory_space=pltpu.SMEM),
            pl.BlockSpec((seq, LANES), lambda b, p: (b, p)),
            kv_spec,
            kv_spec,
        ],
        out_specs=pl.BlockSpec((seq, LANES), lambda b, p: (b, p)),
        out_shape=jax.ShapeDtypeStruct((m, w), BF16),
        scratch_shapes=[
            pltpu.VMEM((seq, 2 * LANES), BF16),
            pltpu.VMEM((HEADS_PER_BLOCK, seq // tile, 2 * LANES, tile), BF16),
            pltpu.VMEM((HEADS_PER_BLOCK, seq // tile, LANES, tile), BF16),
            pltpu.VMEM((group, group + LANES), BF16),
            pltpu.VMEM((HEADS_PER_BLOCK, tile, LANES), F32),
            pltpu.VMEM((HEADS_PER_BLOCK, tile, LANES), F32),
        ],
        compiler_params=_cparams("parallel", "arbitrary"),
        name="attn_prompt",
    )(bias, q, kt, vt)


def _attn_decode_kernel(pt_ref, q_ref, bias_ref, kn_ref, vn_ref, *rest, t_new, pages_per_step):
    kc_refs = rest[:pages_per_step]
    vc_refs = rest[pages_per_step:2 * pages_per_step]
    o_ref, acc_ref, carry_ref, u_ref = rest[2 * pages_per_step:]
    j = pl.program_id(1)
    rows, slots = bias_ref.shape

    @pl.when(j == 0)
    def _():
        u_ref[...] = _cumsum_rhs(slots)
        r = lax.broadcasted_iota(jnp.int32, (rows, slots), 0)
        c = lax.broadcasted_iota(jnp.int32, (rows, slots), 1)
        mask = c < (r // N_HEADS)
        s = jnp.dot(q_ref[...], kn_ref[...].astype(BF16), preferred_element_type=F32)
        carry, pv = _sb_block(s + bias_ref[...], jnp.zeros((rows, LANES), F32), u_ref[...], mask,
                              vn_ref[...].astype(BF16))
        carry_ref[...] = carry
        acc_ref[...] = pv

    @pl.when(j > 0)
    def _():
        carry = carry_ref[...]
        total = None
        for kc_ref, vc_ref in zip(kc_refs, vc_refs):
            s = jnp.dot(q_ref[...], kc_ref[...].astype(BF16), preferred_element_type=F32)
            carry, pv = _sb_block(s + bias_ref[...], carry, u_ref[...], None, vc_ref[...].astype(BF16))
            total = pv if total is None else total + pv
        carry_ref[...] = carry
        acc_ref[...] += total

    @pl.when(j == pl.num_programs(1) - 1)
    def _():
        r = lax.broadcasted_iota(jnp.int32, acc_ref.shape, 0)
        l = lax.broadcasted_iota(jnp.int32, acc_ref.shape, 1)
        own = jnp.where((r % N_HEADS) == (l // HEAD_DIM), acc_ref[...], 0.0)
        for t in range(t_new):
            o_ref[pl.ds(t, 1), :] = jnp.sum(own[t * N_HEADS:(t + 1) * N_HEADS], axis=0, keepdims=True)


def _attn_decode(q_rows, bias_rows, kt_new, vt_new, cache_kt, cache_vt, page_table, t_new):
    nb, rows, w = q_rows.shape
    n_pages = page_table.shape[1]
    slots = cache_kt.shape[2]
    pps = max(p for p in (DECODE_PAGES_PER_STEP, 4, 2, 1) if n_pages % p == 0)

    def page_idx(r):
        return lambda b, j, pt: (pt[b, n_pages - 1 - (jnp.maximum(j, 1) - 1) * pps - r], 0, 0)

    per_b = lambda b, j, pt: (b, 0, 0)
    page_specs = [pl.BlockSpec((None, w, slots), page_idx(r)) for r in range(pps)]
    grid_spec = pltpu.PrefetchScalarGridSpec(
        num_scalar_prefetch=1,
        grid=(nb, n_pages // pps + 1),
        in_specs=[
            pl.BlockSpec((None, rows, w), per_b),
            pl.BlockSpec((rows, slots), lambda b, j, pt: (0, 0)),
            pl.BlockSpec((None, w, slots), per_b),
            pl.BlockSpec((None, w, slots), per_b),
        ] + page_specs + page_specs,
        out_specs=pl.BlockSpec((None, t_new, w), per_b),
        scratch_shapes=[
            pltpu.VMEM((rows, w), F32),
            pltpu.VMEM((rows, LANES), F32),
            pltpu.VMEM((slots, slots + LANES), BF16),
        ],
    )
    return pl.pallas_call(
        functools.partial(_attn_decode_kernel, t_new=t_new, pages_per_step=pps),
        grid_spec=grid_spec,
        out_shape=jax.ShapeDtypeStruct((nb, t_new, w), F32),
        compiler_params=_cparams("parallel", "arbitrary"),
        name="attn_decode",
    )(page_table, q_rows, bias_rows, kt_new, vt_new, *([cache_kt] * pps), *([cache_vt] * pps))


def _pool_tile(ext_ref, p_ref, tp, pos0):
    gc = ext_ref.shape[1] // POOL_GROUPS
    pos = pos0 + lax.broadcasted_iota(jnp.int32, (tp, 1), 0)
    for g, win in enumerate(POOL_WINDOWS):
        cols = slice(g * gc, (g + 1) * gc)
        u_new = ext_ref[pl.ds(POOL_HALO, tp), cols]
        win_sum = u_new
        for d in range(1, win):
            win_sum = win_sum + ext_ref[pl.ds(POOL_HALO - d, tp), cols]
        cnt = jnp.minimum(win, pos + 1).astype(F32)
        p_ref[:, cols] = (win_sum / cnt - u_new).astype(p_ref.dtype)


def _pool_prompt_kernel(u_ref, halo_ref, p_ref, ext_ref, *, tp):
    i = pl.program_id(1)
    ext_ref[pl.ds(0, POOL_HALO), :] = jnp.where(i == 0, 0.0, halo_ref[...])
    ext_ref[pl.ds(POOL_HALO, tp), :] = u_ref[...]
    _pool_tile(ext_ref, p_ref, tp, i * tp)


def _pool_prompt(u, batch, seq, tp_cap=512):
    m, w = u.shape
    tp = _row_tile(seq, tp_cap)
    nt = seq // tp
    hb = tp // POOL_HALO
    return pl.pallas_call(
        functools.partial(_pool_prompt_kernel, tp=tp),
        grid=(batch, nt),
        in_specs=[
            pl.BlockSpec((tp, w), lambda b, i: (b * nt + i, 0)),
            pl.BlockSpec((POOL_HALO, w), lambda b, i: (jnp.maximum((b * nt + i) * hb - 1, 0), 0)),
        ],
        out_specs=pl.BlockSpec((tp, w), lambda b, i: (b * nt + i, 0)),
        out_shape=jax.ShapeDtypeStruct((m, w), BF16),
        scratch_shapes=[pltpu.VMEM((POOL_HALO + tp, w), F32)],
        compiler_params=_cparams("parallel", "arbitrary"),
        name="pool_prompt",
    )(u, u)


def _pool_sample_kernel(u_ref, halo_ref, p_ref, ext_ref, *, tp, pos0):
    ext_ref[pl.ds(0, POOL_HALO), :] = halo_ref[...]
    ext_ref[pl.ds(POOL_HALO, tp), :] = u_ref[...]
    _pool_tile(ext_ref, p_ref, tp, pos0)


def _pool_sample(u3, halo3, pos0):
    nb, tp, w = u3.shape
    return pl.pallas_call(
        functools.partial(_pool_sample_kernel, tp=tp, pos0=pos0),
        grid=(nb,),
        in_specs=[
            pl.BlockSpec((None, tp, w), lambda b: (b, 0, 0)),
            pl.BlockSpec((None, POOL_HALO, w), lambda b: (b, 0, 0)),
        ],
        out_specs=pl.BlockSpec((None, tp, w), lambda b: (b, 0, 0)),
        out_shape=jax.ShapeDtypeStruct((nb, tp, w), F32),
        scratch_shapes=[pltpu.VMEM((POOL_HALO + tp, w), F32)],
        compiler_params=_cparams("parallel"),
        name="pool_sample",
    )(u3, halo3)


def _inproj(xns, w_in, col0, ncols, *, mode, out_dtype, tn, head_gain=None, seq=0):
    d = w_in.shape[0]
    jb = col0 // tn
    assert col0 % tn == 0
    consts, epilogue = (), {"plain": _epi_plain, "sigmoid": _epi_sigmoid, "headnorm": _epi_headnorm}[mode]
    if mode == "headnorm":
        hid = jnp.arange(tn, dtype=jnp.int32) // HEAD_DIM
        bd = (hid[:, None] == hid[None, :]).astype(BF16)
        gain = jnp.tile(head_gain.astype(F32), tn // HEAD_DIM).reshape(1, tn)
        consts = ((bd, (tn, tn), lambda j: (0, 0)), (gain, (1, tn), lambda j: (0, 0)))
    row_sets = [_Rows(lhs=((xn, d, lambda j: 0),), head_major_seq=seq if s == 0 else 0)
                for s, xn in enumerate(xns)]
    return _ws_matmul("inproj_" + mode + ("_t" if seq else ""), ((w_in, d, lambda j: (0, j + jb)),), (0,),
                      consts, row_sets, epilogue, ncols, tn, out_dtype, tm_cap=1024)


def _project(xns, wts, seq):
    w_in = wts["w_in"]
    d = w_in.shape[0]
    aw, pw = ATTN_WIDTH, d // 2
    q = _inproj(xns, w_in, 0, aw, mode="headnorm", out_dtype=BF16, tn=256,
                head_gain=wts["q_norm_g"] * (HEAD_DIM ** -0.5 * LOG2E))
    k = _inproj(xns, w_in, aw, aw, mode="headnorm", out_dtype=F32, tn=256, head_gain=wts["k_norm_g"], seq=seq)
    v = _inproj(xns, w_in, 2 * aw, aw, mode="plain", out_dtype=F32, tn=256, seq=seq)
    u = _inproj(xns, w_in, 3 * aw, pw, mode="plain", out_dtype=F32, tn=512)
    gates = _inproj(xns, w_in, 3 * aw + pw, 2 * d, mode="sigmoid", out_dtype=BF16, tn=512)
    return q, k, v, u, gates


def _mix_and_ffn(xs, o_attns, ps, gates, wts):
    w_ap, w_pool, w_out = wts["w_attn_proj"], wts["w_pool"], wts["w_out"]
    w_gu, w_down = wts["w_gate_up"], wts["w_down"]
    aw, d = w_ap.shape
    g, gc, tn = w_pool.shape
    dff = w_down.shape[0]
    assert g * tn == d
    mixed = _ws_matmul(
        "merge",
        ((w_ap, aw, lambda j: (0, j)), (w_pool.reshape(g * gc, tn), gc, lambda j: (j, 0))), (0, 1),
        ((wts["pool_scale"].reshape(1, d), (1, tn), lambda j: (0, j)),),
        [_Rows(lhs=((o, aw, lambda j: 0), (p, gc, lambda j: j)),
               extras=((gt, lambda j: j), (gt, lambda j: j + g)))
         for o, p, gt in zip(o_attns, ps, gates)],
        _epi_merge, d, tn, BF16, tm_cap=1024)
    hs = _ws_matmul(
        "out_proj", ((w_out, d, lambda j: (0, j)),), (0,), (),
        [_Rows(lhs=((mx, d, lambda j: 0),), extras=((x, lambda j: j),)) for mx, x in zip(mixed, xs)],
        _epi_residual, d, 512, F32, tm_cap=1024)
    hns = [_rmsnorm(h, wts["norm2_g"]) for h in hs]
    nj = dff // 512
    acts = _ws_matmul(
        "gate_up", ((w_gu, d, lambda j: (0, j)), (w_gu, d, lambda j: (0, j + nj))), (0, 0), (),
        [_Rows(lhs=((hn, d, lambda j: 0),)) for hn in hns],
        _epi_swiglu, dff, 512, BF16, tm_cap=1024)
    return _ws_matmul(
        "down_proj", ((w_down, dff, lambda j: (0, j)),), (0,), (),
        [_Rows(lhs=((act, dff, lambda j: 0),), extras=((h, lambda j: j),)) for act, h in zip(acts, hs)],
        _epi_residual, d, 512, F32, tm_cap=512)


def kernel(x_prompt, x_sample, cache_k, cache_v, state_pool, page_table, norm1_g, w_in,
           q_norm_g, k_norm_g, sb_bias, w_attn_proj, w_pool, pool_scale, w_out, norm2_g, w_gate_up, w_down):
    b_p, seq, d = x_prompt.shape
    b_s, t_s = x_sample.shape[:2]
    depth, n_pool, page = cache_k.shape[:3]
    n_pages = page_table.shape[1]
    past_len = n_pages * page
    pw = d // 2

    xp = x_prompt.reshape(b_p * seq, d)
    xs = x_sample.reshape(b_s * t_s, d)
    outs = {name: [] for name in ("kp", "vp", "up", "ks", "vs", "us")}
    for l in range(depth):
        wts = {
            "w_in": w_in[l], "q_norm_g": q_norm_g[l], "k_norm_g": k_norm_g[l], "w_attn_proj": w_attn_proj[l],
            "w_pool": w_pool[l], "pool_scale": pool_scale[l], "w_out": w_out[l], "norm2_g": norm2_g[l],
            "w_gate_up": w_gate_up[l], "w_down": w_down[l],
        }
        bias = sb_bias[l].astype(F32) * LOG2E

        xns = [_rmsnorm(xp, norm1_g[l]), _rmsnorm(xs, norm1_g[l])]
        (q_p, q_s), (kt_p, k_s), (vt_p, v_s), (u_p, u_s), (gates_p, gates_s) = _project(xns, wts, seq)

        o_attn_p = _attn_prompt(q_p, kt_p, vt_p, bias, b_p, seq)
        p_p = _pool_prompt(u_p, b_p, seq)
        outs["kp"].append(kt_p.transpose(0, 3, 1, 2))
        outs["vp"].append(vt_p.transpose(0, 3, 1, 2))
        outs["up"].append(u_p.reshape(b_p, seq, pw)[:, seq - POOL_STATE:])

        q4 = q_s.reshape(b_s, t_s, 1, N_HEADS, HEAD_DIM)
        eye = jnp.eye(N_HEADS, dtype=BF16).reshape(1, 1, N_HEADS, N_HEADS, 1)
        q_rows = (q4 * eye).reshape(b_s, t_s * N_HEADS, ATTN_WIDTH)
        bias_rows = jnp.broadcast_to(jnp.tile(bias, t_s)[:, None], (t_s * N_HEADS, page))
        pad = ((0, 0), (0, 0), (0, page - t_s))
        kt_new = jnp.pad(k_s.reshape(b_s, t_s, ATTN_WIDTH).transpose(0, 2, 1), pad)
        vt_new = jnp.pad(v_s.reshape(b_s, t_s, ATTN_WIDTH).transpose(0, 2, 1), pad)
        cache_kt = cache_k[l].transpose(0, 2, 3, 1).reshape(n_pool, ATTN_WIDTH, page)
        cache_vt = cache_v[l].transpose(0, 2, 3, 1).reshape(n_pool, ATTN_WIDTH, page)
        o_attn_s = _attn_decode(q_rows, bias_rows, kt_new, vt_new, cache_kt, cache_vt, page_table, t_s)
        o_attn_s = o_attn_s.reshape(b_s * t_s, ATTN_WIDTH).astype(BF16)
        u3 = u_s.reshape(b_s, t_s, pw)
        halo = jnp.pad(state_pool[l], ((0, 0), (POOL_HALO - POOL_STATE, 0), (0, 0)))
        p_s = _pool_sample(u3, halo, past_len).reshape(b_s * t_s, pw).astype(BF16)
        outs["ks"].append(k_s.reshape(b_s, t_s, N_HEADS, HEAD_DIM))
        outs["vs"].append(v_s.reshape(b_s, t_s, N_HEADS, HEAD_DIM))
        outs["us"].append(jnp.concatenate([state_pool[l], u3], axis=1)[:, -POOL_STATE:])

        xp, xs = _mix_and_ffn([xp, xs], [o_attn_p, o_attn_s], [p_p, p_s], [gates_p, gates_s], wts)

    st = lambda name: jnp.stack(outs[name], axis=0)
    return (xp.reshape(b_p, seq, d), xs.reshape(b_s, t_s, d),
            st("kp"), st("vp"), st("up"), st("ks"), st("vs"), st("us"))
```

```python
import functools
from typing import NamedTuple

import jax
import jax.numpy as jnp
from jax import lax
from jax.experimental import pallas as pl
from jax.experimental.pallas import tpu as pltpu

F32 = jnp.float32
BF16 = jnp.bfloat16

N_HEADS = 16
HEAD_DIM = 64
ATTN_WIDTH = N_HEADS * HEAD_DIM
POOL_WINDOWS = (2, 4, 8, 16)
POOL_GROUPS = len(POOL_WINDOWS)
POOL_STATE = max(POOL_WINDOWS) - 1
POOL_HALO = 16
RMS_EPS = 1e-6

LANES = 128
HEADS_PER_BLOCK = LANES // HEAD_DIM
VMEM_LIMIT = 52 * 1024 * 1024
ATTN_TILE = 512
CUMSUM_GROUP = 256
BIAS_TERMS = 3
BIAS_ROWS = 16
DECODE_PAGES_PER_STEP = 8
INPROJ_COLS = 1024
HEADNORM_GROUP = 256
LOG2E = 1.4426950408889634

NT_DIMS = (((1,), (1,)), ((), ()))


def _cparams(*sem):
    return pltpu.CompilerParams(dimension_semantics=sem, vmem_limit_bytes=VMEM_LIMIT)


def _row_tile(m, cap):
    t = min(m, cap)
    while m % t:
        t //= 2
    return t


def _rmsnorm_kernel(x_ref, g_ref, o_ref):
    x = x_ref[...]
    ms = jnp.mean(x * x, axis=-1, keepdims=True)
    o_ref[...] = ((x * lax.rsqrt(ms + RMS_EPS)) * g_ref[...]).astype(o_ref.dtype)


def _rmsnorm(x2d, g, tm_cap=512):
    m, d = x2d.shape
    tm = _row_tile(m, tm_cap)
    return pl.pallas_call(
        _rmsnorm_kernel,
        grid=(m // tm,),
        in_specs=[pl.BlockSpec((tm, d), lambda i: (i, 0)), pl.BlockSpec((1, d), lambda i: (0, 0))],
        out_specs=pl.BlockSpec((tm, d), lambda i: (i, 0)),
        out_shape=jax.ShapeDtypeStruct((m, d), BF16),
        compiler_params=_cparams("parallel"),
        name="rmsnorm",
    )(x2d, g.reshape(1, d))


class _Rows(NamedTuple):
    lhs: tuple
    extras: tuple = ()
    head_major_seq: int = 0


def _ws_kernel(*refs, lhs_of_w, n_const, sets, epilogue):
    n_w = len(lhs_of_w)
    w_refs, refs = refs[:n_w], refs[n_w:]
    const_refs, refs = refs[:n_const], refs[n_const:]
    set_refs = []
    for n_lhs, n_extra, _ in sets:
        set_refs.append((refs[:n_lhs], refs[n_lhs:n_lhs + n_extra]))
        refs = refs[n_lhs + n_extra:]
    out_refs, wb_refs = refs[:len(sets)], refs[len(sets):]

    def compute(lhs_refs, extra_refs, o_ref, head_major):
        accs = [jnp.dot(lhs_refs[lhs_of_w[k]][...], wb_refs[k][...], preferred_element_type=F32)
                for k in range(n_w)]
        r = epilogue(accs, [e[...] for e in extra_refs], [c[...] for c in const_refs])
        if head_major:
            r = r.T.reshape(o_ref.shape)
        o_ref[...] = r.astype(o_ref.dtype)

    @pl.when(pl.program_id(1) == 0)
    def _():
        for w_ref, wb_ref in zip(w_refs, wb_refs):
            wb_ref[...] = w_ref[...].astype(BF16)
        for (lhs_refs, extra_refs), o_ref, (_, _, head_major) in zip(set_refs[1:], out_refs[1:], sets[1:]):
            compute(lhs_refs, extra_refs, o_ref, head_major)

    compute(*set_refs[0], out_refs[0], sets[0][2])


def _ws_matmul(name, weights, lhs_of_w, consts, row_sets, epilogue, n_cols, tn, out_dtype, tm_cap):
    assert n_cols % tn == 0
    m0 = row_sets[0].lhs[0][0].shape[0]
    tm0 = _row_tile(row_sets[0].head_major_seq or m0, tm_cap)
    in_specs, args = [], []
    for w, kb, idx in weights:
        in_specs.append(pl.BlockSpec((kb, tn), lambda j, i, idx=idx: idx(j)))
        args.append(w)
    for c, shape, idx in consts:
        in_specs.append(pl.BlockSpec(shape, lambda j, i, idx=idx: idx(j)))
        args.append(c)
    out_specs, out_shapes, sets = [], [], []
    for s, rows in enumerate(row_sets):
        m = rows.lhs[0][0].shape[0]
        tm = tm0 if s == 0 else m
        row = (lambda i: i) if s == 0 else (lambda i: 0)
        for arr, kb, kidx in rows.lhs:
            in_specs.append(pl.BlockSpec((tm, kb), lambda j, i, row=row, kidx=kidx: (row(i), kidx(j))))
            args.append(arr)
        for arr, cidx in rows.extras:
            in_specs.append(pl.BlockSpec((tm, tn), lambda j, i, row=row, cidx=cidx: (row(i), cidx(j))))
            args.append(arr)
        if rows.head_major_seq:
            assert s == 0
            seq = rows.head_major_seq
            nt, hpt = seq // tm, tn // HEAD_DIM
            out_specs.append(pl.BlockSpec((None, hpt, HEAD_DIM, tm), lambda j, i, nt=nt: (i // nt, j, 0, i % nt)))
            out_shapes.append(jax.ShapeDtypeStruct((m // seq, n_cols // HEAD_DIM, HEAD_DIM, seq), out_dtype))
        else:
            out_specs.append(pl.BlockSpec((tm, tn), lambda j, i, row=row: (row(i), j)))
            out_shapes.append(jax.ShapeDtypeStruct((m, n_cols), out_dtype))
        sets.append((len(rows.lhs), len(rows.extras), bool(rows.head_major_seq)))
    return pl.pallas_call(
        functools.partial(_ws_kernel, lhs_of_w=tuple(lhs_of_w), n_const=len(consts), sets=tuple(sets),
                          epilogue=epilogue),
        grid=(n_cols // tn, m0 // tm0),
        in_specs=in_specs,
        out_specs=out_specs,
        out_shape=out_shapes,
        scratch_shapes=[pltpu.VMEM((kb, tn), BF16) for _, kb, _ in weights],
        compiler_params=_cparams("parallel", "arbitrary"),
        name=name,
    )(*args)


def _epi_plain(accs, extras, consts):
    return accs[0]


def _epi_headnorm(accs, extras, consts):
    h, (bd, gain) = accs[0], consts
    w = bd.shape[0]
    hh = (h * h).astype(BF16)
    ss = [jnp.dot(hh[:, c:c + w], bd, preferred_element_type=F32) for c in range(0, h.shape[1], w)]
    ss = ss[0] if len(ss) == 1 else jnp.concatenate(ss, axis=1)
    return (h * lax.rsqrt(ss * (1.0 / HEAD_DIM) + RMS_EPS)) * gain


def _epi_sigmoid(accs, extras, consts):
    return 1.0 / (1.0 + jnp.exp(-accs[0]))


def _epi_residual(accs, extras, consts):
    return extras[0] + accs[0]


def _epi_swiglu(accs, extras, consts):
    gate, up = accs
    return (gate / (1.0 + jnp.exp(-gate))) * up


def _cumsum_rhs(g):
    r = lax.broadcasted_iota(jnp.int32, (g, g + LANES), 0)
    c = lax.broadcasted_iota(jnp.int32, (g, g + LANES), 1)
    return jnp.where((r > c) | (c >= g), 1.0, 0.0).astype(BF16)


def _sb_block(z, carry, u, mask, vt):
    g = u.shape[0]
    neg_abs = lax.bitcast_convert_type(lax.bitcast_convert_type(z, jnp.uint32) | jnp.uint32(1 << 31), F32)
    sp = jnp.maximum(z, 0.0) + jnp.log(1.0 + jnp.exp2(neg_abs)) * LOG2E
    if mask is not None:
        sp = jnp.where(mask, sp, 0.0)
    log_beta = z - sp
    sp16 = sp.astype(BF16)
    parts = []
    for c in reversed(range(z.shape[1] // g)):
        cols = slice(c * g, (c + 1) * g)
        cum = jnp.dot(sp16[:, cols], u, preferred_element_type=F32)
        for l in reversed(range(0, g, LANES)):
            parts.append(jnp.exp2(log_beta[:, c * g + l:c * g + l + LANES] - cum[:, l:l + LANES] - carry))
        carry = carry + cum[:, g:]
    a = parts[0] if len(parts) == 1 else jnp.concatenate(parts[::-1], axis=1)
    if mask is not None:
        a = jnp.where(mask, a, 0.0)
    pv = lax.dot_general(a.astype(BF16), vt, NT_DIMS, preferred_element_type=F32)
    return carry, pv


def _attn_prompt_kernel(bias_ref, q_ref, kt_ref, vt_ref, o_ref, qa_ref, kz_ref, vz_ref, u_ref, carry_ref, acc_ref,
                        *, seq, tile):
    pair = pl.program_id(1)
    nblk = seq // tile
    u_ref[...] = _cumsum_rhs(u_ref.shape[0])
    qa_ref[:, pl.ds(0, LANES)] = q_ref[...]
    qa_ref[:, pl.ds(LANES, LANES)] = jnp.ones((seq, LANES), BF16)
    kz_ref[...] = jnp.zeros(kz_ref.shape, BF16)
    vz_ref[...] = jnp.zeros(vz_ref.shape, BF16)
    term_row = lax.broadcasted_iota(jnp.int32, (BIAS_ROWS, tile), 0)
    for h in range(HEADS_PER_BLOCK):
        rows = pl.ds(h * HEAD_DIM, HEAD_DIM)
        rest = jnp.full((BIAS_ROWS, tile), bias_ref[pair * HEADS_PER_BLOCK + h], F32)
        terms = jnp.zeros((BIAS_ROWS, tile), F32)
        for i in range(BIAS_TERMS):
            term = rest.astype(BF16).astype(F32)
            terms = jnp.where(term_row == i, term, terms)
            rest = rest - term
        for cb in range(nblk):
            cols = pl.ds(cb * tile, tile)
            kz_ref[h, cb, rows, :] = kt_ref[h, :, cols].astype(BF16)
            kz_ref[h, cb, pl.ds(LANES, BIAS_ROWS), :] = terms.astype(BF16)
            vz_ref[h, cb, rows, :] = vt_ref[h, :, cols].astype(BF16)

    rq = lax.broadcasted_iota(jnp.int32, (tile, tile), 0)
    ck = lax.broadcasted_iota(jnp.int32, (tile, tile), 1)
    diag_mask = ck < rq

    def qbody(qb, _):
        qoff = pl.multiple_of(qb * tile, tile)
        qblk = qa_ref[pl.ds(qoff, tile), :]

        def step(kb, mask, first):
            for h in range(HEADS_PER_BLOCK):
                z = jnp.dot(qblk, kz_ref[h, kb], preferred_element_type=F32)
                carry = jnp.zeros((tile, LANES), F32) if first else carry_ref[h]
                carry, pv = _sb_block(z, carry, u_ref[...], mask, vz_ref[h, kb])
                carry_ref[h] = carry
                if first:
                    acc_ref[h] = pv
                else:
                    acc_ref[h] += pv

        step(qb, diag_mask, True)

        def kbody(i, _):
            step(qb - 1 - i, None, False)
            return 0

        lax.fori_loop(0, qb, kbody, 0)
        out = acc_ref[0]
        for h in range(1, HEADS_PER_BLOCK):
            out = out + acc_ref[h]
        o_ref[pl.ds(qoff, tile), :] = out.astype(o_ref.dtype)
        return 0

    lax.fori_loop(0, nblk, qbody, 0)


def _attn_prompt(q, kt, vt, bias, batch, seq):
    m, w = q.shape
    npairs = w // LANES
    tile = _row_tile(seq, ATTN_TILE)
    group = min(CUMSUM_GROUP, tile)
    kv_spec = pl.BlockSpec((None, HEADS_PER_BLOCK, HEAD_DIM, seq), lambda b, p: (b, p, 0, 0))
    return pl.pallas_call(
        functools.partial(_attn_prompt_kernel, seq=seq, tile=tile),
        grid=(batch, npairs),
        in_specs=[
            pl.BlockSpec(memory_space=pltpu.SMEM),
            pl.BlockSpec((seq, LANES), lambda b, p: (b, p)),
            kv_spec,
            kv_spec,
        ],
        out_specs=pl.BlockSpec((seq, LANES), lambda b, p: (b, p)),
        out_shape=jax.ShapeDtypeStruct((m, w), BF16),
        scratch_shapes=[
            pltpu.VMEM((seq, 2 * LANES), BF16),
            pltpu.VMEM((HEADS_PER_BLOCK, seq // tile, 2 * LANES, tile), BF16),
            pltpu.VMEM((HEADS_PER_BLOCK, seq // tile, LANES, tile), BF16),
            pltpu.VMEM((group, group + LANES), BF16),
            pltpu.VMEM((HEADS_PER_BLOCK, tile, LANES), F32),
            pltpu.VMEM((HEADS_PER_BLOCK, tile, LANES), F32),
        ],
        compiler_params=_cparams("parallel", "arbitrary"),
        name="attn_prompt",
    )(bias, q, kt, vt)


def _attn_decode_kernel(pt_ref, q_ref, bias_ref, kn_ref, vn_ref, *rest, t_new, pages_per_step):
    kc_refs = rest[:pages_per_step]
    vc_refs = rest[pages_per_step:2 * pages_per_step]
    o_ref, acc_ref, carry_ref, u_ref, kb_ref, vb_ref = rest[2 * pages_per_step:]
    j = pl.program_id(1)
    rows = bias_ref.shape[0]
    slots = kn_ref.shape[1]

    @pl.when(j == 0)
    def _():
        u_ref[...] = _cumsum_rhs(slots)
        r = lax.broadcasted_iota(jnp.int32, (rows, slots), 0)
        c = lax.broadcasted_iota(jnp.int32, (rows, slots), 1)
        mask = c < (r // N_HEADS)
        s = jnp.dot(q_ref[...], kn_ref[...].astype(BF16), preferred_element_type=F32)
        carry, pv = _sb_block(s + bias_ref[:, pl.ds(0, slots)], jnp.zeros((rows, LANES), F32), u_ref[...], mask,
                              vn_ref[...].astype(BF16))
        carry_ref[...] = carry
        acc_ref[...] = pv

    @pl.when(j > 0)
    def _():
        for r, (kc_ref, vc_ref) in enumerate(zip(kc_refs, vc_refs)):
            cols = pl.ds((pages_per_step - 1 - r) * slots, slots)
            kb_ref[:, cols] = kc_ref[...].astype(BF16)
            vb_ref[:, cols] = vc_ref[...].astype(BF16)
        s = jnp.dot(q_ref[...], kb_ref[...], preferred_element_type=F32)
        carry, pv = _sb_block(s + bias_ref[...], carry_ref[...], u_ref[...], None, vb_ref[...])
        carry_ref[...] = carry
        acc_ref[...] += pv

    @pl.when(j == pl.num_programs(1) - 1)
    def _():
        r = lax.broadcasted_iota(jnp.int32, acc_ref.shape, 0)
        l = lax.broadcasted_iota(jnp.int32, acc_ref.shape, 1)
        own = jnp.where((r % N_HEADS) == (l // HEAD_DIM), acc_ref[...], 0.0)
        for t in range(t_new):
            o_ref[pl.ds(t, 1), :] = jnp.sum(own[t * N_HEADS:(t + 1) * N_HEADS], axis=0, keepdims=True)


def _attn_decode(q_rows, row_bias, kt_new, vt_new, cache_kt, cache_vt, page_table, t_new):
    nb, rows, w = q_rows.shape
    n_pages = page_table.shape[1]
    slots = cache_kt.shape[2]
    pps = max(p for p in (DECODE_PAGES_PER_STEP, 4, 2, 1) if n_pages % p == 0)
    bias_rows = jnp.broadcast_to(row_bias[:, None], (rows, pps * slots))

    def page_idx(r):
        return lambda b, j, pt: (pt[b, n_pages - 1 - (jnp.maximum(j, 1) - 1) * pps - r], 0, 0)

    per_b = lambda b, j, pt: (b, 0, 0)
    page_specs = [pl.BlockSpec((None, w, slots), page_idx(r)) for r in range(pps)]
    grid_spec = pltpu.PrefetchScalarGridSpec(
        num_scalar_prefetch=1,
        grid=(nb, n_pages // pps + 1),
        in_specs=[
            pl.BlockSpec((None, rows, w), per_b),
            pl.BlockSpec((rows, pps * slots), lambda b, j, pt: (0, 0)),
            pl.BlockSpec((None, w, slots), per_b),
            pl.BlockSpec((None, w, slots), per_b),
        ] + page_specs + page_specs,
        out_specs=pl.BlockSpec((None, t_new, w), per_b),
        scratch_shapes=[
            pltpu.VMEM((rows, w), F32),
            pltpu.VMEM((rows, LANES), F32),
            pltpu.VMEM((slots, slots + LANES), BF16),
            pltpu.VMEM((w, pps * slots), BF16),
            pltpu.VMEM((w, pps * slots), BF16),
        ],
    )
    return pl.pallas_call(
        functools.partial(_attn_decode_kernel, t_new=t_new, pages_per_step=pps),
        grid_spec=grid_spec,
        out_shape=jax.ShapeDtypeStruct((nb, t_new, w), F32),
        compiler_params=_cparams("parallel", "arbitrary"),
        name="attn_decode",
    )(page_table, q_rows, bias_rows, kt_new, vt_new, *([cache_kt] * pps), *([cache_vt] * pps))


def _pool_tile(ext_ref, p_ref, tp, pos0):
    gc = ext_ref.shape[1] // POOL_GROUPS
    pos = pos0 + lax.broadcasted_iota(jnp.int32, (tp, 1), 0)
    for g, win in enumerate(POOL_WINDOWS):
        cols = slice(g * gc, (g + 1) * gc)
        u_new = ext_ref[pl.ds(POOL_HALO, tp), cols]
        win_sum = u_new
        for d in range(1, win):
            win_sum = win_sum + ext_ref[pl.ds(POOL_HALO - d, tp), cols]
        cnt = jnp.minimum(win, pos + 1).astype(F32)
        p_ref[:, cols] = (win_sum / cnt - u_new).astype(p_ref.dtype)


def _pool_prompt_kernel(u_ref, halo_ref, p_ref, ext_ref, *, tp):
    i = pl.program_id(1)
    ext_ref[pl.ds(0, POOL_HALO), :] = jnp.where(i == 0, 0.0, halo_ref[...])
    ext_ref[pl.ds(POOL_HALO, tp), :] = u_ref[...]
    _pool_tile(ext_ref, p_ref, tp, i * tp)


def _pool_prompt(u, batch, seq, tp_cap=512):
    m, w = u.shape
    tp = _row_tile(seq, tp_cap)
    nt = seq // tp
    hb = tp // POOL_HALO
    return pl.pallas_call(
        functools.partial(_pool_prompt_kernel, tp=tp),
        grid=(batch, nt),
        in_specs=[
            pl.BlockSpec((tp, w), lambda b, i: (b * nt + i, 0)),
            pl.BlockSpec((POOL_HALO, w), lambda b, i: (jnp.maximum((b * nt + i) * hb - 1, 0), 0)),
        ],
        out_specs=pl.BlockSpec((tp, w), lambda b, i: (b * nt + i, 0)),
        out_shape=jax.ShapeDtypeStruct((m, w), BF16),
        scratch_shapes=[pltpu.VMEM((POOL_HALO + tp, w), F32)],
        compiler_params=_cparams("parallel", "arbitrary"),
        name="pool_prompt",
    )(u, u)


def _pool_sample_kernel(u_ref, halo_ref, p_ref, ext_ref, *, tp, pos0):
    ext_ref[pl.ds(0, POOL_HALO), :] = halo_ref[...]
    ext_ref[pl.ds(POOL_HALO, tp), :] = u_ref[...]
    _pool_tile(ext_ref, p_ref, tp, pos0)


def _pool_sample(u3, halo3, pos0):
    nb, tp, w = u3.shape
    return pl.pallas_call(
        functools.partial(_pool_sample_kernel, tp=tp, pos0=pos0),
        grid=(nb,),
        in_specs=[
            pl.BlockSpec((None, tp, w), lambda b: (b, 0, 0)),
            pl.BlockSpec((None, POOL_HALO, w), lambda b: (b, 0, 0)),
        ],
        out_specs=pl.BlockSpec((None, tp, w), lambda b: (b, 0, 0)),
        out_shape=jax.ShapeDtypeStruct((nb, tp, w), F32),
        scratch_shapes=[pltpu.VMEM((POOL_HALO + tp, w), F32)],
        compiler_params=_cparams("parallel"),
        name="pool_sample",
    )(u3, halo3)


def _inproj(xns, w_in, col0, ncols, *, mode, out_dtype, tn, head_gain=None, seq=0):
    d = w_in.shape[0]
    jb = col0 // tn
    assert col0 % tn == 0
    consts, epilogue = (), {"plain": _epi_plain, "sigmoid": _epi_sigmoid, "headnorm": _epi_headnorm}[mode]
    if mode == "headnorm":
        hid = jnp.arange(HEADNORM_GROUP, dtype=jnp.int32) // HEAD_DIM
        bd = (hid[:, None] == hid[None, :]).astype(BF16)
        gain = jnp.tile(head_gain.astype(F32), tn // HEAD_DIM).reshape(1, tn)
        consts = ((bd, bd.shape, lambda j: (0, 0)), (gain, (1, tn), lambda j: (0, 0)))
    row_sets = [_Rows(lhs=((xn, d, lambda j: 0),), head_major_seq=seq if s == 0 else 0)
                for s, xn in enumerate(xns)]
    return _ws_matmul("inproj_" + mode + ("_t" if seq else ""), ((w_in, d, lambda j: (0, j + jb)),), (0,),
                      consts, row_sets, epilogue, ncols, tn, out_dtype, tm_cap=1024)


def _project(xns, wts, seq):
    w_in = wts["w_in"]
    d = w_in.shape[0]
    aw, pw = ATTN_WIDTH, d // 2
    tn = INPROJ_COLS
    q = _inproj(xns, w_in, 0, aw, mode="headnorm", out_dtype=BF16, tn=tn,
                head_gain=wts["q_norm_g"] * (HEAD_DIM ** -0.5 * LOG2E))
    k = _inproj(xns, w_in, aw, aw, mode="headnorm", out_dtype=F32, tn=tn, head_gain=wts["k_norm_g"], seq=seq)
    v = _inproj(xns, w_in, 2 * aw, aw, mode="plain", out_dtype=F32, tn=tn, seq=seq)
    u = _inproj(xns, w_in, 3 * aw, pw, mode="plain", out_dtype=F32, tn=tn)
    gates = _inproj(xns, w_in, 3 * aw + pw, 2 * d, mode="sigmoid", out_dtype=BF16, tn=tn)
    return q, k, v, u, gates


def _mix_kernel(o_ref, p_ref, ga_ref, gb_ref, x_ref, wap_ref, wp_ref, ps_ref, wo_ref, g2_ref, h_ref, hn_ref):
    a = jnp.dot(o_ref[...], wap_ref[...], preferred_element_type=F32)
    groups, gc, _ = wp_ref.shape
    op = jnp.concatenate(
        [jnp.dot(p_ref[:, g * gc:(g + 1) * gc], wp_ref[g], preferred_element_type=F32) for g in range(groups)],
        axis=1)
    merged = ga_ref[...].astype(F32) * a + gb_ref[...].astype(F32) * (op * ps_ref[...])
    h = x_ref[...] + jnp.dot(merged.astype(BF16), wo_ref[...], preferred_element_type=F32)
    h_ref[...] = h
    ms = jnp.mean(h * h, axis=-1, keepdims=True)
    hn_ref[...] = ((h * lax.rsqrt(ms + RMS_EPS)) * g2_ref[...]).astype(hn_ref.dtype)


def _mix(x, o_attn, p, gates, w_ap, w_pool, pool_scale, w_out, norm2_g, tm_cap=256):
    m, d = x.shape
    aw, pw = o_attn.shape[1], p.shape[1]
    tm = _row_tile(m, tm_cap)
    whole = lambda a: pl.BlockSpec(a.shape, lambda i: (0,) * a.ndim)
    ps2, g2 = pool_scale.reshape(1, d), norm2_g.reshape(1, d)
    return pl.pallas_call(
        _mix_kernel,
        grid=(m // tm,),
        in_specs=[
            pl.BlockSpec((tm, aw), lambda i: (i, 0)),
            pl.BlockSpec((tm, pw), lambda i: (i, 0)),
            pl.BlockSpec((tm, d), lambda i: (i, 0)),
            pl.BlockSpec((tm, d), lambda i: (i, 1)),
            pl.BlockSpec((tm, d), lambda i: (i, 0)),
            whole(w_ap), whole(w_pool), whole(ps2), whole(w_out), whole(g2),
        ],
        out_specs=[pl.BlockSpec((tm, d), lambda i: (i, 0)), pl.BlockSpec((tm, d), lambda i: (i, 0))],
        out_shape=[jax.ShapeDtypeStruct((m, d), F32), jax.ShapeDtypeStruct((m, d), BF16)],
        compiler_params=_cparams("parallel"),
        name="mix",
    )(o_attn, p, gates, gates, x, w_ap, w_pool, ps2, w_out, g2)


def _mix_and_ffn(xs, o_attns, ps, gates, wts):
    w_gu, w_down = wts["w_gate_up"], wts["w_down"]
    dff, d = w_down.shape
    w_ap, w_pool, w_out = (wts[n].astype(BF16) for n in ("w_attn_proj", "w_pool", "w_out"))
    hs, hns = zip(*[_mix(x, o, p, gt, w_ap, w_pool, wts["pool_scale"], w_out, wts["norm2_g"])
                    for x, o, p, gt in zip(xs, o_attns, ps, gates)])
    nj = dff // 512
    acts = _ws_matmul(
        "gate_up", ((w_gu, d, lambda j: (0, j)), (w_gu, d, lambda j: (0, j + nj))), (0, 0), (),
        [_Rows(lhs=((hn, d, lambda j: 0),)) for hn in hns],
        _epi_swiglu, dff, 512, BF16, tm_cap=1024)
    return _ws_matmul(
        "down_proj", ((w_down, dff, lambda j: (0, j)),), (0,), (),
        [_Rows(lhs=((act, dff, lambda j: 0),), extras=((h, lambda j: j),)) for act, h in zip(acts, hs)],
        _epi_residual, d, 512, F32, tm_cap=512)


def kernel(x_prompt, x_sample, cache_k, cache_v, state_pool, page_table, norm1_g, w_in,
           q_norm_g, k_norm_g, sb_bias, w_attn_proj, w_pool, pool_scale, w_out, norm2_g, w_gate_up, w_down):
    b_p, seq, d = x_prompt.shape
    b_s, t_s = x_sample.shape[:2]
    depth, n_pool, page = cache_k.shape[:3]
    n_pages = page_table.shape[1]
    past_len = n_pages * page
    pw = d // 2

    xp = x_prompt.reshape(b_p * seq, d)
    xs = x_sample.reshape(b_s * t_s, d)
    outs = {name: [] for name in ("kp", "vp", "up", "ks", "vs", "us")}
    for l in range(depth):
        wts = {
            "w_in": w_in[l], "q_norm_g": q_norm_g[l], "k_norm_g": k_norm_g[l], "w_attn_proj": w_attn_proj[l],
            "w_pool": w_pool[l], "pool_scale": pool_scale[l], "w_out": w_out[l], "norm2_g": norm2_g[l],
            "w_gate_up": w_gate_up[l], "w_down": w_down[l],
        }
        bias = sb_bias[l].astype(F32) * LOG2E

        xns = [_rmsnorm(xp, norm1_g[l]), _rmsnorm(xs, norm1_g[l])]
        (q_p, q_s), (kt_p, k_s), (vt_p, v_s), (u_p, u_s), (gates_p, gates_s) = _project(xns, wts, seq)

        o_attn_p = _attn_prompt(q_p, kt_p, vt_p, bias, b_p, seq)
        p_p = _pool_prompt(u_p, b_p, seq)
        outs["kp"].append(kt_p.transpose(0, 3, 1, 2))
        outs["vp"].append(vt_p.transpose(0, 3, 1, 2))
        outs["up"].append(u_p.reshape(b_p, seq, pw)[:, seq - POOL_STATE:])

        q4 = q_s.reshape(b_s, t_s, 1, N_HEADS, HEAD_DIM)
        eye = jnp.eye(N_HEADS, dtype=BF16).reshape(1, 1, N_HEADS, N_HEADS, 1)
        q_rows = (q4 * eye).reshape(b_s, t_s * N_HEADS, ATTN_WIDTH)
        pad = ((0, 0), (0, 0), (0, page - t_s))
        kt_new = jnp.pad(k_s.reshape(b_s, t_s, ATTN_WIDTH).transpose(0, 2, 1), pad)
        vt_new = jnp.pad(v_s.reshape(b_s, t_s, ATTN_WIDTH).transpose(0, 2, 1), pad)
        cache_kt = cache_k[l].transpose(0, 2, 3, 1).reshape(n_pool, ATTN_WIDTH, page)
        cache_vt = cache_v[l].transpose(0, 2, 3, 1).reshape(n_pool, ATTN_WIDTH, page)
        o_attn_s = _attn_decode(q_rows, jnp.tile(bias, t_s), kt_new, vt_new, cache_kt, cache_vt, page_table, t_s)
        o_attn_s = o_attn_s.reshape(b_s * t_s, ATTN_WIDTH).astype(BF16)
        u3 = u_s.reshape(b_s, t_s, pw)
        halo = jnp.pad(state_pool[l], ((0, 0), (POOL_HALO - POOL_STATE, 0), (0, 0)))
        p_s = _pool_sample(u3, halo, past_len).reshape(b_s * t_s, pw).astype(BF16)
        outs["ks"].append(k_s.reshape(b_s, t_s, N_HEADS, HEAD_DIM))
        outs["vs"].append(v_s.reshape(b_s, t_s, N_HEADS, HEAD_DIM))
        outs["us"].append(jnp.concatenate([state_pool[l], u3], axis=1)[:, -POOL_STATE:])

        xp, xs = _mix_and_ffn([xp, xs], [o_attn_p, o_attn_s], [p_p, p_s], [gates_p, gates_s], wts)

    st = lambda name: jnp.stack(outs[name], axis=0)
    return (xp.reshape(b_p, seq, d), xs.reshape(b_s, t_s, d),
            st("kp"), st("vp"), st("up"), st("ks"), st("vs"), st("us"))
```

```python
import functools
from typing import NamedTuple

import jax
import jax.numpy as jnp
from jax import lax
from jax.experimental import pallas as pl
from jax.experimental.pallas import tpu as pltpu

F32 = jnp.float32
BF16 = jnp.bfloat16

N_HEADS = 16
HEAD_DIM = 64
ATTN_WIDTH = N_HEADS * HEAD_DIM
POOL_WINDOWS = (2, 4, 8, 16)
POOL_GROUPS = len(POOL_WINDOWS)
POOL_STATE = max(POOL_WINDOWS) - 1
POOL_HALO = 16
RMS_EPS = 1e-6

LANES = 128
HEADS_PER_BLOCK = LANES // HEAD_DIM
VMEM_LIMIT = 52 * 1024 * 1024
ATTN_TILE = 512
CUMSUM_GROUP = 256
BIAS_TERMS = 3
BIAS_ROWS = 16
DECODE_PAGES_PER_STEP = 8
INPROJ_COLS = 1024
HEADNORM_GROUP = 256
LOG2E = 1.4426950408889634

NT_DIMS = (((1,), (1,)), ((), ()))


def _cparams(*sem):
    return pltpu.CompilerParams(dimension_semantics=sem, vmem_limit_bytes=VMEM_LIMIT)


def _row_tile(m, cap):
    t = min(m, cap)
    while m % t:
        t //= 2
    return t


def _rmsnorm_kernel(x_ref, g_ref, o_ref):
    x = x_ref[...]
    ms = jnp.mean(x * x, axis=-1, keepdims=True)
    o_ref[...] = ((x * lax.rsqrt(ms + RMS_EPS)) * g_ref[...]).astype(o_ref.dtype)


def _rmsnorm(x2d, g, tm_cap=512):
    m, d = x2d.shape
    tm = _row_tile(m, tm_cap)
    return pl.pallas_call(
        _rmsnorm_kernel,
        grid=(m // tm,),
        in_specs=[pl.BlockSpec((tm, d), lambda i: (i, 0)), pl.BlockSpec((1, d), lambda i: (0, 0))],
        out_specs=pl.BlockSpec((tm, d), lambda i: (i, 0)),
        out_shape=jax.ShapeDtypeStruct((m, d), BF16),
        compiler_params=_cparams("parallel"),
        name="rmsnorm",
    )(x2d, g.reshape(1, d))


class _Rows(NamedTuple):
    lhs: tuple
    extras: tuple = ()
    head_major_seq: int = 0


def _ws_kernel(*refs, lhs_of_w, n_const, sets, epilogue):
    n_w = len(lhs_of_w)
    w_refs, refs = refs[:n_w], refs[n_w:]
    const_refs, refs = refs[:n_const], refs[n_const:]
    set_refs = []
    for n_lhs, n_extra, _ in sets:
        set_refs.append((refs[:n_lhs], refs[n_lhs:n_lhs + n_extra]))
        refs = refs[n_lhs + n_extra:]
    out_refs, wb_refs = refs[:len(sets)], refs[len(sets):]

    def compute(lhs_refs, extra_refs, o_ref, head_major):
        accs = [jnp.dot(lhs_refs[lhs_of_w[k]][...], wb_refs[k][...], preferred_element_type=F32)
                for k in range(n_w)]
        r = epilogue(accs, [e[...] for e in extra_refs], [c[...] for c in const_refs])
        if head_major:
            r = r.T.reshape(o_ref.shape)
        o_ref[...] = r.astype(o_ref.dtype)

    @pl.when(pl.program_id(1) == 0)
    def _():
        for w_ref, wb_ref in zip(w_refs, wb_refs):
            wb_ref[...] = w_ref[...].astype(BF16)
        for (lhs_refs, extra_refs), o_ref, (_, _, head_major) in zip(set_refs[1:], out_refs[1:], sets[1:]):
            compute(lhs_refs, extra_refs, o_ref, head_major)

    compute(*set_refs[0], out_refs[0], sets[0][2])


def _ws_matmul(name, weights, lhs_of_w, consts, row_sets, epilogue, n_cols, tn, out_dtype, tm_cap):
    assert n_cols % tn == 0
    m0 = row_sets[0].lhs[0][0].shape[0]
    tm0 = _row_tile(row_sets[0].head_major_seq or m0, tm_cap)
    in_specs, args = [], []
    for w, kb, idx in weights:
        in_specs.append(pl.BlockSpec((kb, tn), lambda j, i, idx=idx: idx(j)))
        args.append(w)
    for c, shape, idx in consts:
        in_specs.append(pl.BlockSpec(shape, lambda j, i, idx=idx: idx(j)))
        args.append(c)
    out_specs, out_shapes, sets = [], [], []
    for s, rows in enumerate(row_sets):
        m = rows.lhs[0][0].shape[0]
        tm = tm0 if s == 0 else m
        row = (lambda i: i) if s == 0 else (lambda i: 0)
        for arr, kb, kidx in rows.lhs:
            in_specs.append(pl.BlockSpec((tm, kb), lambda j, i, row=row, kidx=kidx: (row(i), kidx(j))))
            args.append(arr)
        for arr, cidx in rows.extras:
            in_specs.append(pl.BlockSpec((tm, tn), lambda j, i, row=row, cidx=cidx: (row(i), cidx(j))))
            args.append(arr)
        if rows.head_major_seq:
            assert s == 0
            seq = rows.head_major_seq
            nt, hpt = seq // tm, tn // HEAD_DIM
            out_specs.append(pl.BlockSpec((None, hpt, HEAD_DIM, tm), lambda j, i, nt=nt: (i // nt, j, 0, i % nt)))
            out_shapes.append(jax.ShapeDtypeStruct((m // seq, n_cols // HEAD_DIM, HEAD_DIM, seq), out_dtype))
        else:
            out_specs.append(pl.BlockSpec((tm, tn), lambda j, i, row=row: (row(i), j)))
            out_shapes.append(jax.ShapeDtypeStruct((m, n_cols), out_dtype))
        sets.append((len(rows.lhs), len(rows.extras), bool(rows.head_major_seq)))
    return pl.pallas_call(
        functools.partial(_ws_kernel, lhs_of_w=tuple(lhs_of_w), n_const=len(consts), sets=tuple(sets),
                          epilogue=epilogue),
        grid=(n_cols // tn, m0 // tm0),
        in_specs=in_specs,
        out_specs=out_specs,
        out_shape=out_shapes,
        scratch_shapes=[pltpu.VMEM((kb, tn), BF16) for _, kb, _ in weights],
        compiler_params=_cparams("parallel", "arbitrary"),
        name=name,
    )(*args)


def _epi_plain(accs, extras, consts):
    return accs[0]


def _epi_headnorm(accs, extras, consts):
    h, (bd, gain) = accs[0], consts
    w = bd.shape[0]
    hh = (h * h).astype(BF16)
    ss = [jnp.dot(hh[:, c:c + w], bd, preferred_element_type=F32) for c in range(0, h.shape[1], w)]
    ss = ss[0] if len(ss) == 1 else jnp.concatenate(ss, axis=1)
    return (h * lax.rsqrt(ss * (1.0 / HEAD_DIM) + RMS_EPS)) * gain


def _epi_sigmoid(accs, extras, consts):
    return 1.0 / (1.0 + jnp.exp(-accs[0]))


def _epi_residual(accs, extras, consts):
    return extras[0] + accs[0]


def _epi_swiglu(accs, extras, consts):
    gate, up = accs
    return (gate / (1.0 + jnp.exp(-gate))) * up


def _cumsum_rhs(g):
    r = lax.broadcasted_iota(jnp.int32, (g, g + LANES), 0)
    c = lax.broadcasted_iota(jnp.int32, (g, g + LANES), 1)
    return jnp.where((r > c) | (c >= g), 1.0, 0.0).astype(BF16)


def _sb_weights(z, carry, u, mask):
    g = u.shape[0]
    neg_abs = lax.bitcast_convert_type(lax.bitcast_convert_type(z, jnp.uint32) | jnp.uint32(1 << 31), F32)
    sp = jnp.maximum(z, 0.0) + jnp.log(1.0 + jnp.exp2(neg_abs)) * LOG2E
    if mask is not None:
        sp = jnp.where(mask, sp, 0.0)
    log_beta = z - sp
    sp16 = sp.astype(BF16)
    parts = []
    for c in reversed(range(z.shape[1] // g)):
        cols = slice(c * g, (c + 1) * g)
        cum = jnp.dot(sp16[:, cols], u, preferred_element_type=F32)
        for l in reversed(range(0, g, LANES)):
            parts.append(jnp.exp2(log_beta[:, c * g + l:c * g + l + LANES] - cum[:, l:l + LANES] - carry))
        carry = carry + cum[:, g:]
    a = parts[0] if len(parts) == 1 else jnp.concatenate(parts[::-1], axis=1)
    if mask is not None:
        a = jnp.where(mask, a, 0.0)
    return carry, a.astype(BF16)


def _sb_block(z, carry, u, mask, vt):
    carry, a = _sb_weights(z, carry, u, mask)
    return carry, lax.dot_general(a, vt, NT_DIMS, preferred_element_type=F32)


class _DecodePlan(NamedTuple):
    n_batch: int
    t_new: int
    n_pages: int
    pages_per_group: int
    n_groups: int
    steps_per_batch: int
    slots_per_step: int
    groups_per_slot: int
    every_slot_full: bool


def _attn_kernel(pt_ref, bias_ref, q_ref, kt_ref, vt_ref, qd_ref, biasd_ref, kn_ref, vn_ref, ck_hbm, cv_hbm,
                 o_ref, od_ref, qa_ref, kz_ref, vz_ref, u_ref, carry_ref, acc_ref, za_ref, zb_ref, aa_ref, ab_ref,
                 pk_ref, pv_ref, sem, kb_ref, vb_ref, dacc_ref, dcarry_ref, du_ref, *, seq, tile, plan):
    pair = pl.program_id(1)
    step = pl.program_id(0) * pl.num_programs(1) + pair
    kw = tile // 2
    nkb = seq // kw
    u_ref[...] = _cumsum_rhs(u_ref.shape[0])
    qa_ref[:, pl.ds(0, LANES)] = q_ref[...]
    qa_ref[:, pl.ds(LANES, LANES)] = jnp.ones((seq, LANES), BF16)
    kz_ref[...] = jnp.zeros(kz_ref.shape, BF16)
    vz_ref[...] = jnp.zeros(vz_ref.shape, BF16)
    term_row = lax.broadcasted_iota(jnp.int32, (BIAS_ROWS, kw), 0)
    for h in range(HEADS_PER_BLOCK):
        rows = pl.ds(h * HEAD_DIM, HEAD_DIM)
        rest = jnp.full((BIAS_ROWS, kw), bias_ref[pair * HEADS_PER_BLOCK + h], F32)
        terms = jnp.zeros((BIAS_ROWS, kw), F32)
        for i in range(BIAS_TERMS):
            term = rest.astype(BF16).astype(F32)
            terms = jnp.where(term_row == i, term, terms)
            rest = rest - term
        for cb in range(nkb):
            cols = pl.ds(cb * kw, kw)
            kz_ref[h, cb, rows, :] = kt_ref[h, :, cols].astype(BF16)
            kz_ref[h, cb, pl.ds(LANES, BIAS_ROWS), :] = terms.astype(BF16)
            vz_ref[h, cb, rows, :] = vt_ref[h, :, cols].astype(BF16)

    rq = lax.broadcasted_iota(jnp.int32, (tile, tile), 0)
    ck = lax.broadcasted_iota(jnp.int32, (tile, tile), 1)
    diag_mask = ck < rq

    heads = range(HEADS_PER_BLOCK)

    rows_d, slots = qd_ref.shape[0], kn_ref.shape[1]
    n_all_groups = plan.n_batch * plan.n_groups
    step_in_batch = step % plan.steps_per_batch
    batch_d = step // plan.steps_per_batch

    def group_copies(grp):
        bd, n, buf = grp // plan.n_groups, grp % plan.n_groups, grp % 2
        copies = []
        for r in range(plan.pages_per_group):
            page = pt_ref[bd, plan.n_pages - 1 - n * plan.pages_per_group - r]
            copies.append(pltpu.make_async_copy(ck_hbm.at[page], pk_ref.at[buf, r], sem.at[buf, 0, r]))
            copies.append(pltpu.make_async_copy(cv_hbm.at[page], pv_ref.at[buf, r], sem.at[buf, 1, r]))
        return copies

    def start_group(grp):
        for c in group_copies(grp):
            c.start()

    def decode_group(grp):
        buf = grp % 2
        for c in group_copies(grp):
            c.wait()

        @pl.when(grp + 1 < n_all_groups)
        def _():
            start_group(grp + 1)

        for r in range(plan.pages_per_group):
            cols = pl.ds((plan.pages_per_group - 1 - r) * slots, slots)
            kb_ref[:, cols] = pk_ref[buf, r].astype(BF16)
            vb_ref[:, cols] = pv_ref[buf, r].astype(BF16)
        s = jnp.dot(qd_ref[...], kb_ref[...], preferred_element_type=F32)
        dcarry_ref[...], pv = _sb_block(s + biasd_ref[...], dcarry_ref[...], du_ref[...], None, vb_ref[...])
        dacc_ref[...] += pv

    def decode_new_tokens():
        du_ref[...] = _cumsum_rhs(slots)
        r = lax.broadcasted_iota(jnp.int32, (rows_d, slots), 0)
        c = lax.broadcasted_iota(jnp.int32, (rows_d, slots), 1)
        s = jnp.dot(qd_ref[...], kn_ref[...].astype(BF16), preferred_element_type=F32)
        dcarry_ref[...], dacc_ref[...] = _sb_block(
            s + biasd_ref[:, pl.ds(0, slots)], jnp.zeros((rows_d, LANES), F32), du_ref[...], c < (r // N_HEADS),
            vn_ref[...].astype(BF16))

    def decode_finish():
        r = lax.broadcasted_iota(jnp.int32, dacc_ref.shape, 0)
        l = lax.broadcasted_iota(jnp.int32, dacc_ref.shape, 1)
        own = jnp.where((r % N_HEADS) == (l // HEAD_DIM), dacc_ref[...], 0.0)
        for t in range(plan.t_new):
            od_ref[pl.ds(t, 1), :] = jnp.sum(own[t * N_HEADS:(t + 1) * N_HEADS], axis=0, keepdims=True)

    def decode_slot(qb):
        slot = step_in_batch * plan.slots_per_step + qb

        @pl.when((step == 0) & (qb == 0))
        def _():
            start_group(0)

        pl.when(slot == 0)(decode_new_tokens)
        for t in range(plan.groups_per_slot):
            n = slot * plan.groups_per_slot + t
            grp = batch_d * plan.n_groups + n
            if plan.every_slot_full:
                decode_group(grp)
            else:
                pl.when(n < plan.n_groups)(functools.partial(decode_group, grp))
        pl.when(slot == plan.steps_per_batch * plan.slots_per_step - 1)(decode_finish)

    def qbody(qb, _):
        decode_slot(qb)
        qoff = pl.multiple_of(qb * tile, tile)
        qblk = qa_ref[pl.ds(qoff, tile), :]
        k0 = 2 * qb

        def logits(kb, z_ref):
            for h in heads:
                z_ref[h] = jnp.dot(qblk, kz_ref[h, kb], preferred_element_type=F32)

        def weights(z_ref, a_ref):
            for h in heads:
                carry_ref[h], a_ref[h] = _sb_weights(z_ref[h], carry_ref[h], u_ref[...], None)

        def values(a_ref, kb):
            for h in heads:
                acc_ref[h] += lax.dot_general(a_ref[h], vz_ref[h, kb], NT_DIMS, preferred_element_type=F32)

        for h in heads:
            z = jnp.concatenate([jnp.dot(qblk, kz_ref[h, k0 + c], preferred_element_type=F32) for c in range(2)],
                                axis=1)
            carry_ref[h], a = _sb_weights(z, jnp.zeros((tile, LANES), F32), u_ref[...], diag_mask)
            acc_ref[h] = (lax.dot_general(a[:, :kw], vz_ref[h, k0], NT_DIMS, preferred_element_type=F32)
                          + lax.dot_general(a[:, kw:], vz_ref[h, k0 + 1], NT_DIMS, preferred_element_type=F32))

        logits(jnp.maximum(k0 - 1, 0), za_ref)
        ab_ref[...] = jnp.zeros(ab_ref.shape, BF16)

        def kbody(i, _):
            ka = k0 - 1 - 2 * i
            logits(ka - 1, zb_ref)
            values(ab_ref, jnp.minimum(ka + 1, nkb - 1))
            weights(za_ref, aa_ref)
            values(aa_ref, ka)
            logits(jnp.maximum(ka - 2, 0), za_ref)
            weights(zb_ref, ab_ref)
            return 0

        lax.fori_loop(0, qb, kbody, 0)
        values(ab_ref, 0)
        out = acc_ref[0]
        for h in range(1, HEADS_PER_BLOCK):
            out = out + acc_ref[h]
        o_ref[pl.ds(qoff, tile), :] = out.astype(o_ref.dtype)
        return 0

    lax.fori_loop(0, seq // tile, qbody, 0)


def _attention(q, kt, vt, bias, batch, seq, q_rows, row_bias, kt_new, vt_new, cache_kt, cache_vt, page_table,
               t_new):
    m, w = q.shape
    npairs = w // LANES
    tile = _row_tile(seq, ATTN_TILE)
    kw = tile // 2
    group = min(CUMSUM_GROUP, kw)
    nb, rows_d, _ = q_rows.shape
    n_pages, slots = page_table.shape[1], cache_kt.shape[2]
    ppg = max(p for p in (DECODE_PAGES_PER_STEP, 4, 2, 1) if n_pages % p == 0)
    n_steps = batch * npairs
    assert n_steps % nb == 0, (n_steps, nb)
    n_slots = (n_steps // nb) * (seq // tile)
    n_groups = n_pages // ppg
    gps = -(-n_groups // n_slots)
    plan = _DecodePlan(n_batch=nb, t_new=t_new, n_pages=n_pages, pages_per_group=ppg, n_groups=n_groups,
                       steps_per_batch=n_steps // nb, slots_per_step=seq // tile, groups_per_slot=gps,
                       every_slot_full=gps * n_slots == n_groups)
    bias_rows = jnp.broadcast_to(row_bias[:, None], (rows_d, ppg * slots))
    spb = plan.steps_per_batch
    kv_spec = pl.BlockSpec((None, HEADS_PER_BLOCK, HEAD_DIM, seq), lambda b, p, pt: (b, p, 0, 0))
    per_d = lambda b, p, pt: ((b * npairs + p) // spb, 0, 0)
    grid_spec = pltpu.PrefetchScalarGridSpec(
        num_scalar_prefetch=1,
        grid=(batch, npairs),
        in_specs=[
            pl.BlockSpec(memory_space=pltpu.SMEM),
            pl.BlockSpec((seq, LANES), lambda b, p, pt: (b, p)),
            kv_spec,
            kv_spec,
            pl.BlockSpec((None, rows_d, w), per_d),
            pl.BlockSpec((rows_d, ppg * slots), lambda b, p, pt: (0, 0)),
            pl.BlockSpec((None, w, slots), per_d),
            pl.BlockSpec((None, w, slots), per_d),
            pl.BlockSpec(memory_space=pl.ANY),
            pl.BlockSpec(memory_space=pl.ANY),
        ],
        out_specs=[pl.BlockSpec((seq, LANES), lambda b, p, pt: (b, p)),
                   pl.BlockSpec((None, t_new, w), per_d)],
        scratch_shapes=[
            pltpu.VMEM((seq, 2 * LANES), BF16),
            pltpu.VMEM((HEADS_PER_BLOCK, seq // kw, 2 * LANES, kw), BF16),
            pltpu.VMEM((HEADS_PER_BLOCK, seq // kw, LANES, kw), BF16),
            pltpu.VMEM((group, group + LANES), BF16),
            pltpu.VMEM((HEADS_PER_BLOCK, tile, LANES), F32),
            pltpu.VMEM((HEADS_PER_BLOCK, tile, LANES), F32),
            pltpu.VMEM((HEADS_PER_BLOCK, tile, kw), F32),
            pltpu.VMEM((HEADS_PER_BLOCK, tile, kw), F32),
            pltpu.VMEM((HEADS_PER_BLOCK, tile, kw), BF16),
            pltpu.VMEM((HEADS_PER_BLOCK, tile, kw), BF16),
            pltpu.VMEM((2, ppg, w, slots), F32),
            pltpu.VMEM((2, ppg, w, slots), F32),
            pltpu.SemaphoreType.DMA((2, 2, ppg)),
            pltpu.VMEM((w, ppg * slots), BF16),
            pltpu.VMEM((w, ppg * slots), BF16),
            pltpu.VMEM((rows_d, w), F32),
            pltpu.VMEM((rows_d, LANES), F32),
            pltpu.VMEM((slots, slots + LANES), BF16),
        ],
    )
    return pl.pallas_call(
        functools.partial(_attn_kernel, seq=seq, tile=tile, plan=plan),
        grid_spec=grid_spec,
        out_shape=[jax.ShapeDtypeStruct((m, w), BF16), jax.ShapeDtypeStruct((nb, t_new, w), F32)],
        compiler_params=_cparams("arbitrary", "arbitrary"),
        name="attention",
    )(page_table, bias, q, kt, vt, q_rows, bias_rows, kt_new, vt_new, cache_kt, cache_vt)


def _pool_tile(ext_ref, p_ref, tp, pos0):
    gc = ext_ref.shape[1] // POOL_GROUPS
    pos = pos0 + lax.broadcasted_iota(jnp.int32, (tp, 1), 0)
    for g, win in enumerate(POOL_WINDOWS):
        cols = slice(g * gc, (g + 1) * gc)
        u_new = ext_ref[pl.ds(POOL_HALO, tp), cols]
        win_sum = u_new
        for d in range(1, win):
            win_sum = win_sum + ext_ref[pl.ds(POOL_HALO - d, tp), cols]
        cnt = jnp.minimum(win, pos + 1).astype(F32)
        p_ref[:, cols] = (win_sum / cnt - u_new).astype(p_ref.dtype)


def _pool_prompt_kernel(u_ref, halo_ref, p_ref, ext_ref, *, tp):
    i = pl.program_id(1)
    ext_ref[pl.ds(0, POOL_HALO), :] = jnp.where(i == 0, 0.0, halo_ref[...])
    ext_ref[pl.ds(POOL_HALO, tp), :] = u_ref[...]
    _pool_tile(ext_ref, p_ref, tp, i * tp)


def _pool_prompt(u, batch, seq, tp_cap=512):
    m, w = u.shape
    tp = _row_tile(seq, tp_cap)
    nt = seq // tp
    hb = tp // POOL_HALO
    return pl.pallas_call(
        functools.partial(_pool_prompt_kernel, tp=tp),
        grid=(batch, nt),
        in_specs=[
            pl.BlockSpec((tp, w), lambda b, i: (b * nt + i, 0)),
            pl.BlockSpec((POOL_HALO, w), lambda b, i: (jnp.maximum((b * nt + i) * hb - 1, 0), 0)),
        ],
        out_specs=pl.BlockSpec((tp, w), lambda b, i: (b * nt + i, 0)),
        out_shape=jax.ShapeDtypeStruct((m, w), BF16),
        scratch_shapes=[pltpu.VMEM((POOL_HALO + tp, w), F32)],
        compiler_params=_cparams("parallel", "arbitrary"),
        name="pool_prompt",
    )(u, u)


def _pool_sample_kernel(u_ref, halo_ref, p_ref, ext_ref, *, tp, pos0):
    ext_ref[pl.ds(0, POOL_HALO), :] = halo_ref[...]
    ext_ref[pl.ds(POOL_HALO, tp), :] = u_ref[...]
    _pool_tile(ext_ref, p_ref, tp, pos0)


def _pool_sample(u3, halo3, pos0):
    nb, tp, w = u3.shape
    return pl.pallas_call(
        functools.partial(_pool_sample_kernel, tp=tp, pos0=pos0),
        grid=(nb,),
        in_specs=[
            pl.BlockSpec((None, tp, w), lambda b: (b, 0, 0)),
            pl.BlockSpec((None, POOL_HALO, w), lambda b: (b, 0, 0)),
        ],
        out_specs=pl.BlockSpec((None, tp, w), lambda b: (b, 0, 0)),
        out_shape=jax.ShapeDtypeStruct((nb, tp, w), F32),
        scratch_shapes=[pltpu.VMEM((POOL_HALO + tp, w), F32)],
        compiler_params=_cparams("parallel"),
        name="pool_sample",
    )(u3, halo3)


def _inproj(xns, w_in, col0, ncols, *, mode, out_dtype, tn, head_gain=None, seq=0):
    d = w_in.shape[0]
    jb = col0 // tn
    assert col0 % tn == 0
    consts, epilogue = (), {"plain": _epi_plain, "sigmoid": _epi_sigmoid, "headnorm": _epi_headnorm}[mode]
    if mode == "headnorm":
        hid = jnp.arange(HEADNORM_GROUP, dtype=jnp.int32) // HEAD_DIM
        bd = (hid[:, None] == hid[None, :]).astype(BF16)
        gain = jnp.tile(head_gain.astype(F32), tn // HEAD_DIM).reshape(1, tn)
        consts = ((bd, bd.shape, lambda j: (0, 0)), (gain, (1, tn), lambda j: (0, 0)))
    row_sets = [_Rows(lhs=((xn, d, lambda j: 0),), head_major_seq=seq if s == 0 else 0)
                for s, xn in enumerate(xns)]
    return _ws_matmul("inproj_" + mode + ("_t" if seq else ""), ((w_in, d, lambda j: (0, j + jb)),), (0,),
                      consts, row_sets, epilogue, ncols, tn, out_dtype, tm_cap=1024)


def _project(xns, wts, seq):
    w_in = wts["w_in"]
    d = w_in.shape[0]
    aw, pw = ATTN_WIDTH, d // 2
    tn = INPROJ_COLS
    q = _inproj(xns, w_in, 0, aw, mode="headnorm", out_dtype=BF16, tn=tn,
                head_gain=wts["q_norm_g"] * (HEAD_DIM ** -0.5 * LOG2E))
    k = _inproj(xns, w_in, aw, aw, mode="headnorm", out_dtype=F32, tn=tn, head_gain=wts["k_norm_g"], seq=seq)
    v = _inproj(xns, w_in, 2 * aw, aw, mode="plain", out_dtype=F32, tn=tn, seq=seq)
    u = _inproj(xns, w_in, 3 * aw, pw, mode="plain", out_dtype=F32, tn=tn)
    gates = _inproj(xns, w_in, 3 * aw + pw, 2 * d, mode="sigmoid", out_dtype=BF16, tn=tn)
    return q, k, v, u, gates


def _mix_kernel(o_ref, p_ref, ga_ref, gb_ref, x_ref, wap_ref, wp_ref, ps_ref, wo_ref, g2_ref, h_ref, hn_ref):
    a = jnp.dot(o_ref[...], wap_ref[...], preferred_element_type=F32)
    groups, gc, _ = wp_ref.shape
    op = jnp.concatenate(
        [jnp.dot(p_ref[:, g * gc:(g + 1) * gc], wp_ref[g], preferred_element_type=F32) for g in range(groups)],
        axis=1)
    merged = ga_ref[...].astype(F32) * a + gb_ref[...].astype(F32) * (op * ps_ref[...])
    h = x_ref[...] + jnp.dot(merged.astype(BF16), wo_ref[...], preferred_element_type=F32)
    h_ref[...] = h
    ms = jnp.mean(h * h, axis=-1, keepdims=True)
    hn_ref[...] = ((h * lax.rsqrt(ms + RMS_EPS)) * g2_ref[...]).astype(hn_ref.dtype)


def _mix(x, o_attn, p, gates, w_ap, w_pool, pool_scale, w_out, norm2_g, tm_cap=256):
    m, d = x.shape
    aw, pw = o_attn.shape[1], p.shape[1]
    tm = _row_tile(m, tm_cap)
    whole = lambda a: pl.BlockSpec(a.shape, lambda i: (0,) * a.ndim)
    ps2, g2 = pool_scale.reshape(1, d), norm2_g.reshape(1, d)
    return pl.pallas_call(
        _mix_kernel,
        grid=(m // tm,),
        in_specs=[
            pl.BlockSpec((tm, aw), lambda i: (i, 0)),
            pl.BlockSpec((tm, pw), lambda i: (i, 0)),
            pl.BlockSpec((tm, d), lambda i: (i, 0)),
            pl.BlockSpec((tm, d), lambda i: (i, 1)),
            pl.BlockSpec((tm, d), lambda i: (i, 0)),
            whole(w_ap), whole(w_pool), whole(ps2), whole(w_out), whole(g2),
        ],
        out_specs=[pl.BlockSpec((tm, d), lambda i: (i, 0)), pl.BlockSpec((tm, d), lambda i: (i, 0))],
        out_shape=[jax.ShapeDtypeStruct((m, d), F32), jax.ShapeDtypeStruct((m, d), BF16)],
        compiler_params=_cparams("parallel"),
        name="mix",
    )(o_attn, p, gates, gates, x, w_ap, w_pool, ps2, w_out, g2)


def _mix_and_ffn(xs, o_attns, ps, gates, wts):
    w_gu, w_down = wts["w_gate_up"], wts["w_down"]
    dff, d = w_down.shape
    w_ap, w_pool, w_out = (wts[n].astype(BF16) for n in ("w_attn_proj", "w_pool", "w_out"))
    hs, hns = zip(*[_mix(x, o, p, gt, w_ap, w_pool, wts["pool_scale"], w_out, wts["norm2_g"])
                    for x, o, p, gt in zip(xs, o_attns, ps, gates)])
    nj = dff // 512
    acts = _ws_matmul(
        "gate_up", ((w_gu, d, lambda j: (0, j)), (w_gu, d, lambda j: (0, j + nj))), (0, 0), (),
        [_Rows(lhs=((hn, d, lambda j: 0),)) for hn in hns],
        _epi_swiglu, dff, 512, BF16, tm_cap=1024)
    return _ws_matmul(
        "down_proj", ((w_down, dff, lambda j: (0, j)),), (0,), (),
        [_Rows(lhs=((act, dff, lambda j: 0),), extras=((h, lambda j: j),)) for act, h in zip(acts, hs)],
        _epi_residual, d, 512, F32, tm_cap=512)


def kernel(x_prompt, x_sample, cache_k, cache_v, state_pool, page_table, norm1_g, w_in,
           q_norm_g, k_norm_g, sb_bias, w_attn_proj, w_pool, pool_scale, w_out, norm2_g, w_gate_up, w_down):
    b_p, seq, d = x_prompt.shape
    b_s, t_s = x_sample.shape[:2]
    depth, n_pool, page = cache_k.shape[:3]
    n_pages = page_table.shape[1]
    past_len = n_pages * page
    pw = d // 2

    xp = x_prompt.reshape(b_p * seq, d)
    xs = x_sample.reshape(b_s * t_s, d)
    outs = {name: [] for name in ("kp", "vp", "up", "ks", "vs", "us")}
    for l in range(depth):
        wts = {
            "w_in": w_in[l], "q_norm_g": q_norm_g[l], "k_norm_g": k_norm_g[l], "w_attn_proj": w_attn_proj[l],
            "w_pool": w_pool[l], "pool_scale": pool_scale[l], "w_out": w_out[l], "norm2_g": norm2_g[l],
            "w_gate_up": w_gate_up[l], "w_down": w_down[l],
        }
        bias = sb_bias[l].astype(F32) * LOG2E

        xns = [_rmsnorm(xp, norm1_g[l]), _rmsnorm(xs, norm1_g[l])]
        (q_p, q_s), (kt_p, k_s), (vt_p, v_s), (u_p, u_s), (gates_p, gates_s) = _project(xns, wts, seq)

        q4 = q_s.reshape(b_s, t_s, 1, N_HEADS, HEAD_DIM)
        eye = jnp.eye(N_HEADS, dtype=BF16).reshape(1, 1, N_HEADS, N_HEADS, 1)
        q_rows = (q4 * eye).reshape(b_s, t_s * N_HEADS, ATTN_WIDTH)
        pad = ((0, 0), (0, 0), (0, page - t_s))
        kt_new = jnp.pad(k_s.reshape(b_s, t_s, ATTN_WIDTH).transpose(0, 2, 1), pad)
        vt_new = jnp.pad(v_s.reshape(b_s, t_s, ATTN_WIDTH).transpose(0, 2, 1), pad)
        cache_kt = cache_k[l].transpose(0, 2, 3, 1).reshape(n_pool, ATTN_WIDTH, page)
        cache_vt = cache_v[l].transpose(0, 2, 3, 1).reshape(n_pool, ATTN_WIDTH, page)
        o_attn_p, o_attn_s = _attention(q_p, kt_p, vt_p, bias, b_p, seq, q_rows, jnp.tile(bias, t_s), kt_new, vt_new,
                                        cache_kt, cache_vt, page_table, t_s)
        o_attn_s = o_attn_s.reshape(b_s * t_s, ATTN_WIDTH).astype(BF16)
        outs["kp"].append(kt_p.transpose(0, 3, 1, 2))
        outs["vp"].append(vt_p.transpose(0, 3, 1, 2))

        p_p = _pool_prompt(u_p, b_p, seq)
        outs["up"].append(u_p.reshape(b_p, seq, pw)[:, seq - POOL_STATE:])
        u3 = u_s.reshape(b_s, t_s, pw)
        halo = jnp.pad(state_pool[l], ((0, 0), (POOL_HALO - POOL_STATE, 0), (0, 0)))
        p_s = _pool_sample(u3, halo, past_len).reshape(b_s * t_s, pw).astype(BF16)
        outs["ks"].append(k_s.reshape(b_s, t_s, N_HEADS, HEAD_DIM))
        outs["vs"].append(v_s.reshape(b_s, t_s, N_HEADS, HEAD_DIM))
        outs["us"].append(jnp.concatenate([state_pool[l], u3], axis=1)[:, -POOL_STATE:])

        xp, xs = _mix_and_ffn([xp, xs], [o_attn_p, o_attn_s], [p_p, p_s], [gates_p, gates_s], wts)

    st = lambda name: jnp.stack(outs[name], axis=0)
    return (xp.reshape(b_p, seq, d), xs.reshape(b_s, t_s, d),
            st("kp"), st("vp"), st("up"), st("ks"), st("vs"), st("us"))
```

```python
import functools
from typing import NamedTuple

import jax
import jax.numpy as jnp
from jax import lax
from jax.experimental import pallas as pl
from jax.experimental.pallas import tpu as pltpu

F32 = jnp.float32
BF16 = jnp.bfloat16

N_HEADS = 16
HEAD_DIM = 64
ATTN_WIDTH = N_HEADS * HEAD_DIM
POOL_WINDOWS = (2, 4, 8, 16)
POOL_GROUPS = len(POOL_WINDOWS)
POOL_STATE = max(POOL_WINDOWS) - 1
POOL_HALO = 16
RMS_EPS = 1e-6

LANES = 128
SUBLANES = 8
HEADS_PER_BLOCK = LANES // HEAD_DIM
VMEM_LIMIT = 52 * 1024 * 1024
ATTN_VMEM_LIMIT = 57 * 1024 * 1024
ATTN_TILE = 512
CUMSUM_GROUP = 256
BIAS_TERMS = 3
BIAS_ROWS = 16
WEIGHTS_CHUNK = 32
DECODE_PAGES_PER_STEP = 8
INPROJ_COLS = 1024
HEADNORM_GROUP = 256
LOG2E = 1.4426950408889634

NT_DIMS = (((1,), (1,)), ((), ()))


def _cparams(*sem):
    return pltpu.CompilerParams(dimension_semantics=sem, vmem_limit_bytes=VMEM_LIMIT)


def _row_tile(m, cap):
    t = min(m, cap)
    while m % t:
        t //= 2
    return t


def _rmsnorm_kernel(x_ref, g_ref, o_ref):
    x = x_ref[...]
    ms = jnp.mean(x * x, axis=-1, keepdims=True)
    o_ref[...] = ((x * lax.rsqrt(ms + RMS_EPS)) * g_ref[...]).astype(o_ref.dtype)


def _rmsnorm(x2d, g, tm_cap=512):
    m, d = x2d.shape
    tm = _row_tile(m, tm_cap)
    return pl.pallas_call(
        _rmsnorm_kernel,
        grid=(m // tm,),
        in_specs=[pl.BlockSpec((tm, d), lambda i: (i, 0)), pl.BlockSpec((1, d), lambda i: (0, 0))],
        out_specs=pl.BlockSpec((tm, d), lambda i: (i, 0)),
        out_shape=jax.ShapeDtypeStruct((m, d), BF16),
        compiler_params=_cparams("parallel"),
        name="rmsnorm",
    )(x2d, g.reshape(1, d))


class _Rows(NamedTuple):
    lhs: tuple
    extras: tuple = ()
    head_major_seq: int = 0


def _ws_kernel(*refs, lhs_of_w, n_const, sets, epilogue):
    n_w = len(lhs_of_w)
    w_refs, refs = refs[:n_w], refs[n_w:]
    const_refs, refs = refs[:n_const], refs[n_const:]
    set_refs = []
    for n_lhs, n_extra, _ in sets:
        set_refs.append((refs[:n_lhs], refs[n_lhs:n_lhs + n_extra]))
        refs = refs[n_lhs + n_extra:]
    out_refs, wb_refs = refs[:len(sets)], refs[len(sets):]

    def compute(lhs_refs, extra_refs, o_ref, head_major):
        accs = [jnp.dot(lhs_refs[lhs_of_w[k]][...], wb_refs[k][...], preferred_element_type=F32)
                for k in range(n_w)]
        r = epilogue(accs, [e[...] for e in extra_refs], [c[...] for c in const_refs])
        if head_major:
            r = r.T.reshape(o_ref.shape)
        o_ref[...] = r.astype(o_ref.dtype)

    @pl.when(pl.program_id(1) == 0)
    def _():
        for w_ref, wb_ref in zip(w_refs, wb_refs):
            wb_ref[...] = w_ref[...].astype(BF16)
        for (lhs_refs, extra_refs), o_ref, (_, _, head_major) in zip(set_refs[1:], out_refs[1:], sets[1:]):
            compute(lhs_refs, extra_refs, o_ref, head_major)

    compute(*set_refs[0], out_refs[0], sets[0][2])


def _ws_matmul(name, weights, lhs_of_w, consts, row_sets, epilogue, n_cols, tn, out_dtype, tm_cap):
    assert n_cols % tn == 0
    m0 = row_sets[0].lhs[0][0].shape[0]
    tm0 = _row_tile(row_sets[0].head_major_seq or m0, tm_cap)
    in_specs, args = [], []
    for w, kb, idx in weights:
        in_specs.append(pl.BlockSpec((kb, tn), lambda j, i, idx=idx: idx(j)))
        args.append(w)
    for c, shape, idx in consts:
        in_specs.append(pl.BlockSpec(shape, lambda j, i, idx=idx: idx(j)))
        args.append(c)
    out_specs, out_shapes, sets = [], [], []
    for s, rows in enumerate(row_sets):
        m = rows.lhs[0][0].shape[0]
        tm = tm0 if s == 0 else m
        row = (lambda i: i) if s == 0 else (lambda i: 0)
        for arr, kb, kidx in rows.lhs:
            in_specs.append(pl.BlockSpec((tm, kb), lambda j, i, row=row, kidx=kidx: (row(i), kidx(j))))
            args.append(arr)
        for arr, cidx in rows.extras:
            in_specs.append(pl.BlockSpec((tm, tn), lambda j, i, row=row, cidx=cidx: (row(i), cidx(j))))
            args.append(arr)
        if rows.head_major_seq:
            assert s == 0
            seq = rows.head_major_seq
            nt, hpt = seq // tm, tn // HEAD_DIM
            out_specs.append(pl.BlockSpec((None, hpt, HEAD_DIM, tm), lambda j, i, nt=nt: (i // nt, j, 0, i % nt)))
            out_shapes.append(jax.ShapeDtypeStruct((m // seq, n_cols // HEAD_DIM, HEAD_DIM, seq), out_dtype))
        else:
            out_specs.append(pl.BlockSpec((tm, tn), lambda j, i, row=row: (row(i), j)))
            out_shapes.append(jax.ShapeDtypeStruct((m, n_cols), out_dtype))
        sets.append((len(rows.lhs), len(rows.extras), bool(rows.head_major_seq)))
    return pl.pallas_call(
        functools.partial(_ws_kernel, lhs_of_w=tuple(lhs_of_w), n_const=len(consts), sets=tuple(sets),
                          epilogue=epilogue),
        grid=(n_cols // tn, m0 // tm0),
        in_specs=in_specs,
        out_specs=out_specs,
        out_shape=out_shapes,
        scratch_shapes=[pltpu.VMEM((kb, tn), BF16) for _, kb, _ in weights],
        compiler_params=_cparams("parallel", "arbitrary"),
        name=name,
    )(*args)


def _epi_plain(accs, extras, consts):
    return accs[0]


def _epi_headnorm(accs, extras, consts):
    h, (bd, gain) = accs[0], consts
    w = bd.shape[0]
    hh = (h * h).astype(BF16)
    ss = [jnp.dot(hh[:, c:c + w], bd, preferred_element_type=F32) for c in range(0, h.shape[1], w)]
    ss = ss[0] if len(ss) == 1 else jnp.concatenate(ss, axis=1)
    return (h * lax.rsqrt(ss * (1.0 / HEAD_DIM) + RMS_EPS)) * gain


def _epi_sigmoid(accs, extras, consts):
    return 1.0 / (1.0 + jnp.exp(-accs[0]))


def _epi_residual(accs, extras, consts):
    return extras[0] + accs[0]


def _epi_swiglu(accs, extras, consts):
    gate, up = accs
    return (gate / (1.0 + jnp.exp(-gate))) * up


def _cumsum_rhs(g):
    r = lax.broadcasted_iota(jnp.int32, (g, g + LANES), 0)
    c = lax.broadcasted_iota(jnp.int32, (g, g + LANES), 1)
    return jnp.where((r > c) | (c >= g), 1.0, 0.0).astype(BF16)


def _sb_weights(z, carry, u, mask):
    g = u.shape[0]
    neg_abs = lax.bitcast_convert_type(lax.bitcast_convert_type(z, jnp.uint32) | jnp.uint32(1 << 31), F32)
    sp = jnp.maximum(z, 0.0) + jnp.log(1.0 + jnp.exp2(neg_abs)) * LOG2E
    if mask is not None:
        sp = jnp.where(mask, sp, 0.0)
    log_beta = z - sp
    sp16 = sp.astype(BF16)
    parts = []
    for c in reversed(range(z.shape[1] // g)):
        cols = slice(c * g, (c + 1) * g)
        cum = jnp.dot(sp16[:, cols], u, preferred_element_type=F32)
        for l in reversed(range(0, g, LANES)):
            parts.append(jnp.exp2(log_beta[:, c * g + l:c * g + l + LANES] - cum[:, l:l + LANES] - carry))
        carry = carry + cum[:, g:]
    a = parts[0] if len(parts) == 1 else jnp.concatenate(parts[::-1], axis=1)
    if mask is not None:
        a = jnp.where(mask, a, 0.0)
    return carry, a.astype(BF16)


def _cumsum_lhs(g):
    r = lax.broadcasted_iota(jnp.int32, (g + 2 * SUBLANES, g), 0)
    c = lax.broadcasted_iota(jnp.int32, (g + 2 * SUBLANES, g), 1)
    return jnp.where((c > r) | (r >= g), 1.0, 0.0).astype(BF16)


def _softplus_t(z_ref, sp_ref, lb_ref, mask):
    ch = WEIGHTS_CHUNK
    for r0 in range(0, z_ref.shape[0], ch):
        rows = pl.ds(r0, ch)
        zt = z_ref[rows, :]
        neg_abs = lax.bitcast_convert_type(lax.bitcast_convert_type(zt, jnp.uint32) | jnp.uint32(1 << 31), F32)
        sp = jnp.maximum(zt, 0.0) + jnp.log(1.0 + jnp.exp2(neg_abs)) * LOG2E
        if mask is not None:
            sp = jnp.where(mask[r0:r0 + ch], sp, 0.0)
        sp_ref[rows, :] = sp.astype(BF16)
        lb_ref[rows, :] = zt - sp


def _stick_weights_t(lb_ref, cum_ref, carry, a_ref, mask):
    g, nq = lb_ref.shape
    ch = WEIGHTS_CHUNK
    for r0 in range(0, g, ch):
        rows = pl.ds(r0, ch)
        x = (lb_ref[rows, :] - cum_ref[rows, :]).reshape(ch // SUBLANES, SUBLANES, nq) - carry[None]
        a = jnp.exp2(x).reshape(ch, nq)
        if mask is not None:
            a = jnp.where(mask[r0:r0 + ch], a, 0.0)
        a_ref[rows, :] = a.astype(BF16)
    return carry + cum_ref[pl.ds(g, SUBLANES), :]


def _sb_block(z, carry, u, mask, vt):
    carry, a = _sb_weights(z, carry, u, mask)
    return carry, lax.dot_general(a, vt, NT_DIMS, preferred_element_type=F32)


class _DecodePlan(NamedTuple):
    n_batch: int
    t_new: int
    n_pages: int
    pages_per_group: int
    n_groups: int
    steps_per_batch: int
    slots_per_step: int
    groups_per_slot: int
    every_slot_full: bool


def _attn_kernel(pt_ref, bias_ref, qt_ref, kt_ref, vt_ref, qd_ref, biasd_ref, kn_ref, vn_ref, ck_hbm, cv_hbm,
                 o_ref, od_ref, ka_ref, qz_ref, vz_ref, lcum_ref, carry_ref, acc_ref, za_ref, zb_ref, aa_ref, ab_ref,
                 sp_ref, lb_ref, cum_ref, qs_ref, ks_ref, vs_ref, pk_ref, pv_ref, sem, kb_ref, vb_ref, dacc_ref, dcarry_ref, du_ref, *, seq, tile, plan):
    pair = pl.program_id(1)
    step = pl.program_id(0) * pl.num_programs(1) + pair
    kw = tile // 2
    nkb = seq // kw
    lcum_ref[...] = _cumsum_lhs(lcum_ref.shape[1])
    ka_ref[:, pl.ds(LANES, LANES)] = jnp.ones((seq, LANES), BF16)
    for cb in range(nkb):
        cols = pl.ds(cb * kw, kw)
        ka_ref[cols, pl.ds(0, LANES)] = kt_ref[:, :, cols].reshape(LANES, kw).T.astype(BF16)
    qz_ref[...] = jnp.zeros(qz_ref.shape, BF16)
    term_row = lax.broadcasted_iota(jnp.int32, (BIAS_ROWS, tile), 0)
    for h in range(HEADS_PER_BLOCK):
        rows = pl.ds(h * HEAD_DIM, HEAD_DIM)
        rest = jnp.full((BIAS_ROWS, tile), bias_ref[pair * HEADS_PER_BLOCK + h], F32)
        terms = jnp.zeros((BIAS_ROWS, tile), F32)
        for i in range(BIAS_TERMS):
            term = rest.astype(BF16).astype(F32)
            terms = jnp.where(term_row == i, term, terms)
            rest = rest - term
        for qb in range(seq // tile):
            cols = pl.ds(qb * tile, tile)
            qz_ref[h, qb, rows, :] = qt_ref[h, :, cols]
            qz_ref[h, qb, pl.ds(LANES, BIAS_ROWS), :] = terms.astype(BF16)
        for cb in range(nkb):
            vz_ref[h, cb] = vt_ref[h, :, pl.ds(cb * kw, kw)].astype(BF16)

    rk = lax.broadcasted_iota(jnp.int32, (tile, tile), 0)
    cq = lax.broadcasted_iota(jnp.int32, (tile, tile), 1)
    diag_mask = rk < cq

    heads = range(HEADS_PER_BLOCK)

    rows_d, slots = qd_ref.shape[0], kn_ref.shape[1]
    n_all_groups = plan.n_batch * plan.n_groups
    step_in_batch = step % plan.steps_per_batch
    batch_d = step // plan.steps_per_batch

    def group_copies(grp):
        bd, n, buf = grp // plan.n_groups, grp % plan.n_groups, grp % 2
        copies = []
        for r in range(plan.pages_per_group):
            page = pt_ref[bd, plan.n_pages - 1 - n * plan.pages_per_group - r]
            copies.append(pltpu.make_async_copy(ck_hbm.at[page], pk_ref.at[buf, r], sem.at[buf, 0, r]))
            copies.append(pltpu.make_async_copy(cv_hbm.at[page], pv_ref.at[buf, r], sem.at[buf, 1, r]))
        return copies

    def start_group(grp):
        for c in group_copies(grp):
            c.start()

    def decode_group(grp):
        buf = grp % 2
        for c in group_copies(grp):
            c.wait()

        @pl.when(grp + 1 < n_all_groups)
        def _():
            start_group(grp + 1)

        for r in range(plan.pages_per_group):
            cols = pl.ds((plan.pages_per_group - 1 - r) * slots, slots)
            kb_ref[:, cols] = pk_ref[buf, r].astype(BF16)
            vb_ref[:, cols] = pv_ref[buf, r].astype(BF16)
        s = jnp.dot(qd_ref[...], kb_ref[...], preferred_element_type=F32)
        dcarry_ref[...], pv = _sb_block(s + biasd_ref[...], dcarry_ref[...], du_ref[...], None, vb_ref[...])
        dacc_ref[...] += pv

    def decode_new_tokens():
        du_ref[...] = _cumsum_rhs(slots)
        r = lax.broadcasted_iota(jnp.int32, (rows_d, slots), 0)
        c = lax.broadcasted_iota(jnp.int32, (rows_d, slots), 1)
        s = jnp.dot(qd_ref[...], kn_ref[...].astype(BF16), preferred_element_type=F32)
        dcarry_ref[...], dacc_ref[...] = _sb_block(
            s + biasd_ref[:, pl.ds(0, slots)], jnp.zeros((rows_d, LANES), F32), du_ref[...], c < (r // N_HEADS),
            vn_ref[...].astype(BF16))

    def decode_finish():
        r = lax.broadcasted_iota(jnp.int32, dacc_ref.shape, 0)
        l = lax.broadcasted_iota(jnp.int32, dacc_ref.shape, 1)
        own = jnp.where((r % N_HEADS) == (l // HEAD_DIM), dacc_ref[...], 0.0)
        for t in range(plan.t_new):
            od_ref[pl.ds(t, 1), :] = jnp.sum(own[t * N_HEADS:(t + 1) * N_HEADS], axis=0, keepdims=True)

    def decode_slot(qb):
        slot = step_in_batch * plan.slots_per_step + qb

        @pl.when((step == 0) & (qb == 0))
        def _():
            start_group(0)

        pl.when(slot == 0)(decode_new_tokens)
        for t in range(plan.groups_per_slot):
            n = slot * plan.groups_per_slot + t
            grp = batch_d * plan.n_groups + n
            if plan.every_slot_full:
                decode_group(grp)
            else:
                pl.when(n < plan.n_groups)(functools.partial(decode_group, grp))
        if plan.every_slot_full:
            decode_finish()
        else:
            pl.when(slot == plan.steps_per_batch * plan.slots_per_step - 1)(decode_finish)

    def qbody(qb, _):
        decode_slot(qb)
        qoff = pl.multiple_of(qb * tile, tile)
        k0 = 2 * qb

        for h in heads:
            qs_ref[h] = qz_ref[h, qb]

        def stage_keys(slot, kb):
            ks_ref[slot] = ka_ref[pl.ds(pl.multiple_of(kb * kw, kw), kw), :]

        def stage_values(slot, kb):
            for h in heads:
                vs_ref[slot, h] = vz_ref[h, kb]

        def logits(slot, z_ref):
            for h in heads:
                z_ref[h] = jnp.dot(ks_ref[slot], qs_ref[h], preferred_element_type=F32)

        def softplus(z_ref, half, mask=None):
            for h in heads:
                _softplus_t(z_ref.at[h], sp_ref.at[half, h], lb_ref.at[half, h], mask)

        def cumsums(half):
            for h in heads:
                cum_ref[half, h] = jnp.dot(lcum_ref[...], sp_ref[half, h], preferred_element_type=F32)

        def stick(half, a_ref, mask=None, first=False):
            for h in heads:
                carry = jnp.zeros((SUBLANES, tile), F32) if first else carry_ref[h]
                carry_ref[h] = _stick_weights_t(lb_ref.at[half, h], cum_ref.at[half, h], carry, a_ref.at[h], mask)

        def values(a_ref, slot, first=False):
            for h in heads:
                pv = jnp.dot(vs_ref[slot, h], a_ref[h], preferred_element_type=F32)
                acc_ref[h] = pv if first else acc_ref[h] + pv


        def weights_and_next_logits(mask_a=None, mask_b=None, first=False):
            softplus(za_ref, 0, mask_a)
            cumsums(0)
            logits(0, za_ref)
            softplus(zb_ref, 1, mask_b)
            logits(1, zb_ref)
            cumsums(1)
            stick(0, aa_ref, mask_a, first)
            stick(1, ab_ref, mask_b)

        acc_ref[...] = jnp.zeros(acc_ref.shape, F32)
        stage_keys(2, k0 + 1)
        stage_keys(3, k0)
        stage_keys(0, jnp.maximum(k0 - 1, 0))
        stage_keys(1, jnp.maximum(k0 - 2, 0))
        logits(2, za_ref)
        logits(3, zb_ref)
        weights_and_next_logits(diag_mask[kw:], diag_mask[:kw], first=True)

        def kbody(i, _):
            ka = k0 - 1 - 2 * i
            stage_keys(0, jnp.maximum(ka - 2, 0))
            stage_keys(1, jnp.maximum(ka - 3, 0))
            stage_values(0, ka + 2)
            stage_values(1, ka + 1)
            values(aa_ref, 0)
            values(ab_ref, 1)
            weights_and_next_logits()
            return 0

        lax.fori_loop(0, qb, kbody, 0)
        stage_values(0, 1)
        stage_values(1, 0)
        values(aa_ref, 0)
        values(ab_ref, 1)
        out = jnp.concatenate([acc_ref[h] for h in heads], axis=0)
        o_ref[pl.ds(qoff, tile), :] = out.T.astype(o_ref.dtype)
        return 0

    lax.fori_loop(0, seq // tile, qbody, 0)


def _attention(qt, kt, vt, bias, q_rows, row_bias, kt_new, vt_new, cache_kt, cache_vt, page_table, t_new):
    batch, n_heads, _, seq = qt.shape
    m, w = batch * seq, n_heads * HEAD_DIM
    npairs = w // LANES
    tile = _row_tile(seq, ATTN_TILE)
    kw = tile // 2
    assert kw <= CUMSUM_GROUP
    group = kw
    nb, rows_d, _ = q_rows.shape
    n_pages, slots = page_table.shape[1], cache_kt.shape[2]
    ppg = max(p for p in (DECODE_PAGES_PER_STEP, 4, 2, 1) if n_pages % p == 0)
    n_steps = batch * npairs
    assert n_steps % nb == 0, (n_steps, nb)
    n_slots = (n_steps // nb) * (seq // tile)
    n_groups = n_pages // ppg
    gps = -(-n_groups // n_slots)
    plan = _DecodePlan(n_batch=nb, t_new=t_new, n_pages=n_pages, pages_per_group=ppg, n_groups=n_groups,
                       steps_per_batch=n_steps // nb, slots_per_step=seq // tile, groups_per_slot=gps,
                       every_slot_full=gps * n_slots == n_groups)
    bias_rows = jnp.broadcast_to(row_bias[:, None], (rows_d, ppg * slots))
    spb = plan.steps_per_batch
    kv_spec = pl.BlockSpec((None, HEADS_PER_BLOCK, HEAD_DIM, seq), lambda b, p, pt: (b, p, 0, 0))
    per_d = lambda b, p, pt: ((b * npairs + p) // spb, 0, 0)
    grid_spec = pltpu.PrefetchScalarGridSpec(
        num_scalar_prefetch=1,
        grid=(batch, npairs),
        in_specs=[
            pl.BlockSpec(memory_space=pltpu.SMEM),
            kv_spec,
            kv_spec,
            kv_spec,
            pl.BlockSpec((None, rows_d, w), per_d),
            pl.BlockSpec((rows_d, ppg * slots), lambda b, p, pt: (0, 0)),
            pl.BlockSpec((None, w, slots), per_d),
            pl.BlockSpec((None, w, slots), per_d),
            pl.BlockSpec(memory_space=pl.ANY),
            pl.BlockSpec(memory_space=pl.ANY),
        ],
        out_specs=[pl.BlockSpec((seq, LANES), lambda b, p, pt: (b, p)),
                   pl.BlockSpec((None, t_new, w), per_d)],
        scratch_shapes=[
            pltpu.VMEM((seq, 2 * LANES), BF16),
            pltpu.VMEM((HEADS_PER_BLOCK, seq // tile, 2 * LANES, tile), BF16),
            pltpu.VMEM((HEADS_PER_BLOCK, seq // kw, HEAD_DIM, kw), BF16),
            pltpu.VMEM((group + 2 * SUBLANES, group), BF16),
            pltpu.VMEM((HEADS_PER_BLOCK, SUBLANES, tile), F32),
            pltpu.VMEM((HEADS_PER_BLOCK, HEAD_DIM, tile), F32),
            pltpu.VMEM((HEADS_PER_BLOCK, kw, tile), F32),
            pltpu.VMEM((HEADS_PER_BLOCK, kw, tile), F32),
            pltpu.VMEM((HEADS_PER_BLOCK, kw, tile), BF16),
            pltpu.VMEM((HEADS_PER_BLOCK, kw, tile), BF16),
            pltpu.VMEM((2, HEADS_PER_BLOCK, kw, tile), BF16),
            pltpu.VMEM((2, HEADS_PER_BLOCK, kw, tile), F32),
            pltpu.VMEM((2, HEADS_PER_BLOCK, kw + 2 * SUBLANES, tile), F32),
            pltpu.VMEM((HEADS_PER_BLOCK, 2 * LANES, tile), BF16),
            pltpu.VMEM((4, kw, 2 * LANES), BF16),
            pltpu.VMEM((2, HEADS_PER_BLOCK, HEAD_DIM, kw), BF16),
            pltpu.VMEM((2, ppg, w, slots), F32),
            pltpu.VMEM((2, ppg, w, slots), F32),
            pltpu.SemaphoreType.DMA((2, 2, ppg)),
            pltpu.VMEM((w, ppg * slots), BF16),
            pltpu.VMEM((w, ppg * slots), BF16),
            pltpu.VMEM((rows_d, w), F32),
            pltpu.VMEM((rows_d, LANES), F32),
            pltpu.VMEM((slots, slots + LANES), BF16),
        ],
    )
    return pl.pallas_call(
        functools.partial(_attn_kernel, seq=seq, tile=tile, plan=plan),
        grid_spec=grid_spec,
        out_shape=[jax.ShapeDtypeStruct((m, w), BF16), jax.ShapeDtypeStruct((nb, t_new, w), F32)],
        compiler_params=pltpu.CompilerParams(dimension_semantics=("arbitrary", "arbitrary"),
                                             vmem_limit_bytes=ATTN_VMEM_LIMIT),
        name="attention",
    )(page_table, bias, qt, kt, vt, q_rows, bias_rows, kt_new, vt_new, cache_kt, cache_vt)


def _pool_tile(ext_ref, p_ref, tp, pos0):
    gc = ext_ref.shape[1] // POOL_GROUPS
    pos = pos0 + lax.broadcasted_iota(jnp.int32, (tp, 1), 0)
    for g, win in enumerate(POOL_WINDOWS):
        cols = slice(g * gc, (g + 1) * gc)
        u_new = ext_ref[pl.ds(POOL_HALO, tp), cols]
        win_sum = u_new
        for d in range(1, win):
            win_sum = win_sum + ext_ref[pl.ds(POOL_HALO - d, tp), cols]
        cnt = jnp.minimum(win, pos + 1).astype(F32)
        p_ref[:, cols] = (win_sum / cnt - u_new).astype(p_ref.dtype)


def _pool_prompt_kernel(u_ref, halo_ref, p_ref, ext_ref, *, tp):
    i = pl.program_id(1)
    ext_ref[pl.ds(0, POOL_HALO), :] = jnp.where(i == 0, 0.0, halo_ref[...])
    ext_ref[pl.ds(POOL_HALO, tp), :] = u_ref[...]
    _pool_tile(ext_ref, p_ref, tp, i * tp)


def _pool_prompt(u, batch, seq, tp_cap=512):
    m, w = u.shape
    tp = _row_tile(seq, tp_cap)
    nt = seq // tp
    hb = tp // POOL_HALO
    return pl.pallas_call(
        functools.partial(_pool_prompt_kernel, tp=tp),
        grid=(batch, nt),
        in_specs=[
            pl.BlockSpec((tp, w), lambda b, i: (b * nt + i, 0)),
            pl.BlockSpec((POOL_HALO, w), lambda b, i: (jnp.maximum((b * nt + i) * hb - 1, 0), 0)),
        ],
        out_specs=pl.BlockSpec((tp, w), lambda b, i: (b * nt + i, 0)),
        out_shape=jax.ShapeDtypeStruct((m, w), BF16),
        scratch_shapes=[pltpu.VMEM((POOL_HALO + tp, w), F32)],
        compiler_params=_cparams("parallel", "arbitrary"),
        name="pool_prompt",
    )(u, u)


def _pool_sample_kernel(u_ref, halo_ref, p_ref, ext_ref, *, tp, pos0):
    ext_ref[pl.ds(0, POOL_HALO), :] = halo_ref[...]
    ext_ref[pl.ds(POOL_HALO, tp), :] = u_ref[...]
    _pool_tile(ext_ref, p_ref, tp, pos0)


def _pool_sample(u3, halo3, pos0):
    nb, tp, w = u3.shape
    return pl.pallas_call(
        functools.partial(_pool_sample_kernel, tp=tp, pos0=pos0),
        grid=(nb,),
        in_specs=[
            pl.BlockSpec((None, tp, w), lambda b: (b, 0, 0)),
            pl.BlockSpec((None, POOL_HALO, w), lambda b: (b, 0, 0)),
        ],
        out_specs=pl.BlockSpec((None, tp, w), lambda b: (b, 0, 0)),
        out_shape=jax.ShapeDtypeStruct((nb, tp, w), F32),
        scratch_shapes=[pltpu.VMEM((POOL_HALO + tp, w), F32)],
        compiler_params=_cparams("parallel"),
        name="pool_sample",
    )(u3, halo3)


def _inproj(xns, w_in, col0, ncols, *, mode, out_dtype, tn, head_gain=None, seq=0):
    d = w_in.shape[0]
    jb = col0 // tn
    assert col0 % tn == 0
    consts, epilogue = (), {"plain": _epi_plain, "sigmoid": _epi_sigmoid, "headnorm": _epi_headnorm}[mode]
    if mode == "headnorm":
        hid = jnp.arange(HEADNORM_GROUP, dtype=jnp.int32) // HEAD_DIM
        bd = (hid[:, None] == hid[None, :]).astype(BF16)
        gain = jnp.tile(head_gain.astype(F32), tn // HEAD_DIM).reshape(1, tn)
        consts = ((bd, bd.shape, lambda j: (0, 0)), (gain, (1, tn), lambda j: (0, 0)))
    row_sets = [_Rows(lhs=((xn, d, lambda j: 0),), head_major_seq=seq if s == 0 else 0)
                for s, xn in enumerate(xns)]
    return _ws_matmul("inproj_" + mode + ("_t" if seq else ""), ((w_in, d, lambda j: (0, j + jb)),), (0,),
                      consts, row_sets, epilogue, ncols, tn, out_dtype, tm_cap=1024)


def _project(xns, wts, seq):
    w_in = wts["w_in"]
    d = w_in.shape[0]
    aw, pw = ATTN_WIDTH, d // 2
    tn = INPROJ_COLS
    q = _inproj(xns, w_in, 0, aw, mode="headnorm", out_dtype=BF16, tn=tn,
                head_gain=wts["q_norm_g"] * (HEAD_DIM ** -0.5 * LOG2E), seq=seq)
    k = _inproj(xns, w_in, aw, aw, mode="headnorm", out_dtype=F32, tn=tn, head_gain=wts["k_norm_g"], seq=seq)
    v = _inproj(xns, w_in, 2 * aw, aw, mode="plain", out_dtype=F32, tn=tn, seq=seq)
    u = _inproj(xns, w_in, 3 * aw, pw, mode="plain", out_dtype=F32, tn=tn)
    gates = _inproj(xns, w_in, 3 * aw + pw, 2 * d, mode="sigmoid", out_dtype=BF16, tn=tn)
    return q, k, v, u, gates


def _mix_kernel(o_ref, p_ref, ga_ref, gb_ref, x_ref, wap_ref, wp_ref, ps_ref, wo_ref, g2_ref, h_ref, hn_ref):
    a = jnp.dot(o_ref[...], wap_ref[...], preferred_element_type=F32)
    groups, gc, _ = wp_ref.shape
    op = jnp.concatenate(
        [jnp.dot(p_ref[:, g * gc:(g + 1) * gc], wp_ref[g], preferred_element_type=F32) for g in range(groups)],
        axis=1)
    merged = ga_ref[...].astype(F32) * a + gb_ref[...].astype(F32) * (op * ps_ref[...])
    h = x_ref[...] + jnp.dot(merged.astype(BF16), wo_ref[...], preferred_element_type=F32)
    h_ref[...] = h
    ms = jnp.mean(h * h, axis=-1, keepdims=True)
    hn_ref[...] = ((h * lax.rsqrt(ms + RMS_EPS)) * g2_ref[...]).astype(hn_ref.dtype)


def _mix(x, o_attn, p, gates, w_ap, w_pool, pool_scale, w_out, norm2_g, tm_cap=256):
    m, d = x.shape
    aw, pw = o_attn.shape[1], p.shape[1]
    tm = _row_tile(m, tm_cap)
    whole = lambda a: pl.BlockSpec(a.shape, lambda i: (0,) * a.ndim)
    ps2, g2 = pool_scale.reshape(1, d), norm2_g.reshape(1, d)
    return pl.pallas_call(
        _mix_kernel,
        grid=(m // tm,),
        in_specs=[
            pl.BlockSpec((tm, aw), lambda i: (i, 0)),
            pl.BlockSpec((tm, pw), lambda i: (i, 0)),
            pl.BlockSpec((tm, d), lambda i: (i, 0)),
            pl.BlockSpec((tm, d), lambda i: (i, 1)),
            pl.BlockSpec((tm, d), lambda i: (i, 0)),
            whole(w_ap), whole(w_pool), whole(ps2), whole(w_out), whole(g2),
        ],
        out_specs=[pl.BlockSpec((tm, d), lambda i: (i, 0)), pl.BlockSpec((tm, d), lambda i: (i, 0))],
        out_shape=[jax.ShapeDtypeStruct((m, d), F32), jax.ShapeDtypeStruct((m, d), BF16)],
        compiler_params=_cparams("parallel"),
        name="mix",
    )(o_attn, p, gates, gates, x, w_ap, w_pool, ps2, w_out, g2)


def _mix_and_ffn(xs, o_attns, ps, gates, wts):
    w_gu, w_down = wts["w_gate_up"], wts["w_down"]
    dff, d = w_down.shape
    w_ap, w_pool, w_out = (wts[n].astype(BF16) for n in ("w_attn_proj", "w_pool", "w_out"))
    hs, hns = zip(*[_mix(x, o, p, gt, w_ap, w_pool, wts["pool_scale"], w_out, wts["norm2_g"])
                    for x, o, p, gt in zip(xs, o_attns, ps, gates)])
    nj = dff // 512
    acts = _ws_matmul(
        "gate_up", ((w_gu, d, lambda j: (0, j)), (w_gu, d, lambda j: (0, j + nj))), (0, 0), (),
        [_Rows(lhs=((hn, d, lambda j: 0),)) for hn in hns],
        _epi_swiglu, dff, 512, BF16, tm_cap=1024)
    return _ws_matmul(
        "down_proj", ((w_down, dff, lambda j: (0, j)),), (0,), (),
        [_Rows(lhs=((act, dff, lambda j: 0),), extras=((h, lambda j: j),)) for act, h in zip(acts, hs)],
        _epi_residual, d, 512, F32, tm_cap=512)


def kernel(x_prompt, x_sample, cache_k, cache_v, state_pool, page_table, norm1_g, w_in,
           q_norm_g, k_norm_g, sb_bias, w_attn_proj, w_pool, pool_scale, w_out, norm2_g, w_gate_up, w_down):
    b_p, seq, d = x_prompt.shape
    b_s, t_s = x_sample.shape[:2]
    depth, n_pool, page = cache_k.shape[:3]
    n_pages = page_table.shape[1]
    past_len = n_pages * page
    pw = d // 2

    xp = x_prompt.reshape(b_p * seq, d)
    xs = x_sample.reshape(b_s * t_s, d)
    outs = {name: [] for name in ("kp", "vp", "up", "ks", "vs", "us")}
    for l in range(depth):
        wts = {
            "w_in": w_in[l], "q_norm_g": q_norm_g[l], "k_norm_g": k_norm_g[l], "w_attn_proj": w_attn_proj[l],
            "w_pool": w_pool[l], "pool_scale": pool_scale[l], "w_out": w_out[l], "norm2_g": norm2_g[l],
            "w_gate_up": w_gate_up[l], "w_down": w_down[l],
        }
        bias = sb_bias[l].astype(F32) * LOG2E

        xns = [_rmsnorm(xp, norm1_g[l]), _rmsnorm(xs, norm1_g[l])]
        (qt_p, q_s), (kt_p, k_s), (vt_p, v_s), (u_p, u_s), (gates_p, gates_s) = _project(xns, wts, seq)

        q4 = q_s.reshape(b_s, t_s, 1, N_HEADS, HEAD_DIM)
        eye = jnp.eye(N_HEADS, dtype=BF16).reshape(1, 1, N_HEADS, N_HEADS, 1)
        q_rows = (q4 * eye).reshape(b_s, t_s * N_HEADS, ATTN_WIDTH)
        pad = ((0, 0), (0, 0), (0, page - t_s))
        kt_new = jnp.pad(k_s.reshape(b_s, t_s, ATTN_WIDTH).transpose(0, 2, 1), pad)
        vt_new = jnp.pad(v_s.reshape(b_s, t_s, ATTN_WIDTH).transpose(0, 2, 1), pad)
        cache_kt = cache_k[l].transpose(0, 2, 3, 1).reshape(n_pool, ATTN_WIDTH, page)
        cache_vt = cache_v[l].transpose(0, 2, 3, 1).reshape(n_pool, ATTN_WIDTH, page)
        o_attn_p, o_attn_s = _attention(qt_p, kt_p, vt_p, bias, q_rows, jnp.tile(bias, t_s), kt_new, vt_new,
                                        cache_kt, cache_vt, page_table, t_s)
        o_attn_s = o_attn_s.reshape(b_s * t_s, ATTN_WIDTH).astype(BF16)
        outs["kp"].append(kt_p.transpose(0, 3, 1, 2))
        outs["vp"].append(vt_p.transpose(0, 3, 1, 2))

        p_p = _pool_prompt(u_p, b_p, seq)
        outs["up"].append(u_p.reshape(b_p, seq, pw)[:, seq - POOL_STATE:])
        u3 = u_s.reshape(b_s, t_s, pw)
        halo = jnp.pad(state_pool[l], ((0, 0), (POOL_HALO - POOL_STATE, 0), (0, 0)))
        p_s = _pool_sample(u3, halo, past_len).reshape(b_s * t_s, pw).astype(BF16)
        outs["ks"].append(k_s.reshape(b_s, t_s, N_HEADS, HEAD_DIM))
        outs["vs"].append(v_s.reshape(b_s, t_s, N_HEADS, HEAD_DIM))
        outs["us"].append(jnp.concatenate([state_pool[l], u3], axis=1)[:, -POOL_STATE:])

        xp, xs = _mix_and_ffn([xp, xs], [o_attn_p, o_attn_s], [p_p, p_s], [gates_p, gates_s], wts)

    st = lambda name: jnp.stack(outs[name], axis=0)
    return (xp.reshape(b_p, seq, d), xs.reshape(b_s, t_s, d),
            st("kp"), st("vp"), st("up"), st("ks"), st("vs"), st("us"))
```

```python
import functools
from typing import NamedTuple

import jax
import jax.numpy as jnp
from jax import lax
from jax.experimental import pallas as pl
from jax.experimental.pallas import tpu as pltpu

F32 = jnp.float32
BF16 = jnp.bfloat16

N_HEADS = 16
HEAD_DIM = 64
ATTN_WIDTH = N_HEADS * HEAD_DIM
POOL_WINDOWS = (2, 4, 8, 16)
POOL_GROUPS = len(POOL_WINDOWS)
POOL_STATE = max(POOL_WINDOWS) - 1
POOL_HALO = 16
RMS_EPS = 1e-6

LANES = 128
SUBLANES = 8
HEADS_PER_BLOCK = LANES // HEAD_DIM
VMEM_LIMIT = 52 * 1024 * 1024
ATTN_VMEM_LIMIT = 57 * 1024 * 1024
ATTN_TILE = 512
CUMSUM_GROUP = 256
BIAS_TERMS = 3
BIAS_ROWS = 16
WEIGHTS_CHUNK = 32
DECODE_PAGES_PER_STEP = 8
INPROJ_COLS = 1024
HEADNORM_GROUP = 256
LOG2E = 1.4426950408889634

NT_DIMS = (((1,), (1,)), ((), ()))


def _cparams(*sem):
    return pltpu.CompilerParams(dimension_semantics=sem, vmem_limit_bytes=VMEM_LIMIT)


def _row_tile(m, cap):
    t = min(m, cap)
    while m % t:
        t //= 2
    return t


def _rmsnorm_kernel(x_ref, g_ref, o_ref):
    x = x_ref[...]
    ms = jnp.mean(x * x, axis=-1, keepdims=True)
    o_ref[...] = ((x * lax.rsqrt(ms + RMS_EPS)) * g_ref[...]).astype(o_ref.dtype)


def _rmsnorm(x2d, g, tm_cap=512):
    m, d = x2d.shape
    tm = _row_tile(m, tm_cap)
    return pl.pallas_call(
        _rmsnorm_kernel,
        grid=(m // tm,),
        in_specs=[pl.BlockSpec((tm, d), lambda i: (i, 0)), pl.BlockSpec((1, d), lambda i: (0, 0))],
        out_specs=pl.BlockSpec((tm, d), lambda i: (i, 0)),
        out_shape=jax.ShapeDtypeStruct((m, d), BF16),
        compiler_params=_cparams("parallel"),
        name="rmsnorm",
    )(x2d, g.reshape(1, d))


class _Rows(NamedTuple):
    lhs: tuple
    extras: tuple = ()
    head_major_seq: int = 0


def _ws_kernel(*refs, lhs_of_w, n_const, sets, epilogue):
    n_w = len(lhs_of_w)
    w_refs, refs = refs[:n_w], refs[n_w:]
    const_refs, refs = refs[:n_const], refs[n_const:]
    set_refs = []
    for n_lhs, n_extra, _ in sets:
        set_refs.append((refs[:n_lhs], refs[n_lhs:n_lhs + n_extra]))
        refs = refs[n_lhs + n_extra:]
    out_refs, wb_refs = refs[:len(sets)], refs[len(sets):]

    def compute(lhs_refs, extra_refs, o_ref, head_major):
        accs = [jnp.dot(lhs_refs[lhs_of_w[k]][...], wb_refs[k][...], preferred_element_type=F32)
                for k in range(n_w)]
        r = epilogue(accs, [e[...] for e in extra_refs], [c[...] for c in const_refs])
        if head_major:
            r = r.T.reshape(o_ref.shape)
        o_ref[...] = r.astype(o_ref.dtype)

    @pl.when(pl.program_id(1) == 0)
    def _():
        for w_ref, wb_ref in zip(w_refs, wb_refs):
            wb_ref[...] = w_ref[...].astype(BF16)
        for (lhs_refs, extra_refs), o_ref, (_, _, head_major) in zip(set_refs[1:], out_refs[1:], sets[1:]):
            compute(lhs_refs, extra_refs, o_ref, head_major)

    compute(*set_refs[0], out_refs[0], sets[0][2])


def _ws_matmul(name, weights, lhs_of_w, consts, row_sets, epilogue, n_cols, tn, out_dtype, tm_cap):
    assert n_cols % tn == 0
    m0 = row_sets[0].lhs[0][0].shape[0]
    tm0 = _row_tile(row_sets[0].head_major_seq or m0, tm_cap)
    in_specs, args = [], []
    for w, kb, idx in weights:
        in_specs.append(pl.BlockSpec((kb, tn), lambda j, i, idx=idx: idx(j)))
        args.append(w)
    for c, shape, idx in consts:
        in_specs.append(pl.BlockSpec(shape, lambda j, i, idx=idx: idx(j)))
        args.append(c)
    out_specs, out_shapes, sets = [], [], []
    for s, rows in enumerate(row_sets):
        m = rows.lhs[0][0].shape[0]
        tm = tm0 if s == 0 else m
        row = (lambda i: i) if s == 0 else (lambda i: 0)
        for arr, kb, kidx in rows.lhs:
            in_specs.append(pl.BlockSpec((tm, kb), lambda j, i, row=row, kidx=kidx: (row(i), kidx(j))))
            args.append(arr)
        for arr, cidx in rows.extras:
            in_specs.append(pl.BlockSpec((tm, tn), lambda j, i, row=row, cidx=cidx: (row(i), cidx(j))))
            args.append(arr)
        if rows.head_major_seq:
            assert s == 0
            seq = rows.head_major_seq
            nt, hpt = seq // tm, tn // HEAD_DIM
            out_specs.append(pl.BlockSpec((None, hpt, HEAD_DIM, tm), lambda j, i, nt=nt: (i // nt, j, 0, i % nt)))
            out_shapes.append(jax.ShapeDtypeStruct((m // seq, n_cols // HEAD_DIM, HEAD_DIM, seq), out_dtype))
        else:
            out_specs.append(pl.BlockSpec((tm, tn), lambda j, i, row=row: (row(i), j)))
            out_shapes.append(jax.ShapeDtypeStruct((m, n_cols), out_dtype))
        sets.append((len(rows.lhs), len(rows.extras), bool(rows.head_major_seq)))
    return pl.pallas_call(
        functools.partial(_ws_kernel, lhs_of_w=tuple(lhs_of_w), n_const=len(consts), sets=tuple(sets),
                          epilogue=epilogue),
        grid=(n_cols // tn, m0 // tm0),
        in_specs=in_specs,
        out_specs=out_specs,
        out_shape=out_shapes,
        scratch_shapes=[pltpu.VMEM((kb, tn), BF16) for _, kb, _ in weights],
        compiler_params=_cparams("parallel", "arbitrary"),
        name=name,
    )(*args)


def _epi_plain(accs, extras, consts):
    return accs[0]


def _epi_headnorm(accs, extras, consts):
    h, (bd, gain) = accs[0], consts
    w = bd.shape[0]
    hh = (h * h).astype(BF16)
    ss = [jnp.dot(hh[:, c:c + w], bd, preferred_element_type=F32) for c in range(0, h.shape[1], w)]
    ss = ss[0] if len(ss) == 1 else jnp.concatenate(ss, axis=1)
    return (h * lax.rsqrt(ss * (1.0 / HEAD_DIM) + RMS_EPS)) * gain


def _epi_sigmoid(accs, extras, consts):
    return 1.0 / (1.0 + jnp.exp(-accs[0]))


def _epi_residual(accs, extras, consts):
    return extras[0] + accs[0]


def _epi_swiglu(accs, extras, consts):
    gate, up = accs
    return (gate / (1.0 + jnp.exp(-gate))) * up


def _cumsum_rhs(g):
    r = lax.broadcasted_iota(jnp.int32, (g, g + LANES), 0)
    c = lax.broadcasted_iota(jnp.int32, (g, g + LANES), 1)
    return jnp.where((r > c) | (c >= g), 1.0, 0.0).astype(BF16)


def _sb_weights(z, carry, u, mask):
    g = u.shape[0]
    neg_abs = lax.bitcast_convert_type(lax.bitcast_convert_type(z, jnp.uint32) | jnp.uint32(1 << 31), F32)
    sp = jnp.maximum(z, 0.0) + jnp.log(1.0 + jnp.exp2(neg_abs)) * LOG2E
    if mask is not None:
        sp = jnp.where(mask, sp, 0.0)
    log_beta = z - sp
    sp16 = sp.astype(BF16)
    parts = []
    for c in reversed(range(z.shape[1] // g)):
        cols = slice(c * g, (c + 1) * g)
        cum = jnp.dot(sp16[:, cols], u, preferred_element_type=F32)
        for l in reversed(range(0, g, LANES)):
            parts.append(jnp.exp2(log_beta[:, c * g + l:c * g + l + LANES] - cum[:, l:l + LANES] - carry))
        carry = carry + cum[:, g:]
    a = parts[0] if len(parts) == 1 else jnp.concatenate(parts[::-1], axis=1)
    if mask is not None:
        a = jnp.where(mask, a, 0.0)
    return carry, a.astype(BF16)


def _cumsum_lhs(g):
    r = lax.broadcasted_iota(jnp.int32, (g + 2 * SUBLANES, g), 0)
    c = lax.broadcasted_iota(jnp.int32, (g + 2 * SUBLANES, g), 1)
    return jnp.where((c > r) | (r >= g), 1.0, 0.0).astype(BF16)


def _softplus_t(z_ref, sp_ref, lb_ref, mask):
    ch = WEIGHTS_CHUNK
    for r0 in range(0, z_ref.shape[0], ch):
        rows = pl.ds(r0, ch)
        zt = z_ref[rows, :]
        neg_abs = lax.bitcast_convert_type(lax.bitcast_convert_type(zt, jnp.uint32) | jnp.uint32(1 << 31), F32)
        sp = jnp.maximum(zt, 0.0) + jnp.log(1.0 + jnp.exp2(neg_abs)) * LOG2E
        if mask is not None:
            sp = jnp.where(mask[r0:r0 + ch], sp, 0.0)
        sp_ref[rows, :] = sp.astype(BF16)
        lb_ref[rows, :] = zt - sp


def _stick_weights_t(lb_ref, cum_ref, carry, a_ref, mask):
    g, nq = lb_ref.shape
    ch = WEIGHTS_CHUNK
    for r0 in range(0, g, ch):
        rows = pl.ds(r0, ch)
        x = (lb_ref[rows, :] - cum_ref[rows, :]).reshape(ch // SUBLANES, SUBLANES, nq) - carry[None]
        a = jnp.exp2(x).reshape(ch, nq)
        if mask is not None:
            a = jnp.where(mask[r0:r0 + ch], a, 0.0)
        a_ref[rows, :] = a.astype(BF16)
    return carry + cum_ref[pl.ds(g, SUBLANES), :]


def _sb_block(z, carry, u, mask, vt):
    carry, a = _sb_weights(z, carry, u, mask)
    return carry, lax.dot_general(a, vt, NT_DIMS, preferred_element_type=F32)


class _DecodePlan(NamedTuple):
    n_batch: int
    t_new: int
    n_pages: int
    pages_per_group: int
    n_groups: int
    steps_per_batch: int
    slots_per_step: int
    groups_per_slot: int
    every_slot_full: bool


def _attn_kernel(pt_ref, bias_ref, qt_ref, kt_ref, vt_ref, qd_ref, biasd_ref, kn_ref, vn_ref, ck_hbm, cv_hbm,
                 o_ref, od_ref, ka_ref, qz_ref, vz_ref, lcum_ref, carry_ref, acc_ref, za_ref, zb_ref, aa_ref, ab_ref,
                 sp_ref, lb_ref, cum_ref, qs_ref, ks_ref, vs_ref, pk_ref, pv_ref, sem, kb_ref, vb_ref, dacc_ref, dcarry_ref, du_ref, *, seq, tile, plan):
    pair = pl.program_id(1)
    step = pl.program_id(0) * pl.num_programs(1) + pair
    kw = tile // 2
    nkb = seq // kw
    lcum_ref[...] = _cumsum_lhs(lcum_ref.shape[1])
    ka_ref[:, pl.ds(LANES, LANES)] = jnp.ones((seq, LANES), BF16)
    for cb in range(nkb):
        cols = pl.ds(cb * kw, kw)
        ka_ref[cols, pl.ds(0, LANES)] = kt_ref[:, :, cols].reshape(LANES, kw).T.astype(BF16)
    qz_ref[...] = jnp.zeros(qz_ref.shape, BF16)
    term_row = lax.broadcasted_iota(jnp.int32, (BIAS_ROWS, tile), 0)
    for h in range(HEADS_PER_BLOCK):
        rows = pl.ds(h * HEAD_DIM, HEAD_DIM)
        rest = jnp.full((BIAS_ROWS, tile), bias_ref[pair * HEADS_PER_BLOCK + h], F32)
        terms = jnp.zeros((BIAS_ROWS, tile), F32)
        for i in range(BIAS_TERMS):
            term = rest.astype(BF16).astype(F32)
            terms = jnp.where(term_row == i, term, terms)
            rest = rest - term
        for qb in range(seq // tile):
            cols = pl.ds(qb * tile, tile)
            qz_ref[h, qb, rows, :] = qt_ref[h, :, cols]
            qz_ref[h, qb, pl.ds(LANES, BIAS_ROWS), :] = terms.astype(BF16)
        for cb in range(nkb):
            vz_ref[h, cb] = vt_ref[h, :, pl.ds(cb * kw, kw)].astype(BF16)

    rk = lax.broadcasted_iota(jnp.int32, (tile, tile), 0)
    cq = lax.broadcasted_iota(jnp.int32, (tile, tile), 1)
    diag_mask = rk < cq

    heads = range(HEADS_PER_BLOCK)

    rows_d, slots = qd_ref.shape[0], kn_ref.shape[1]
    n_all_groups = plan.n_batch * plan.n_groups
    step_in_batch = step % plan.steps_per_batch
    batch_d = step // plan.steps_per_batch

    def group_copies(grp):
        bd, n, buf = grp // plan.n_groups, grp % plan.n_groups, grp % 2
        copies = []
        for r in range(plan.pages_per_group):
            page = pt_ref[bd, plan.n_pages - 1 - n * plan.pages_per_group - r]
            copies.append(pltpu.make_async_copy(ck_hbm.at[page], pk_ref.at[buf, r], sem.at[buf, 0, r]))
            copies.append(pltpu.make_async_copy(cv_hbm.at[page], pv_ref.at[buf, r], sem.at[buf, 1, r]))
        return copies

    def start_group(grp):
        for c in group_copies(grp):
            c.start()

    def decode_group(grp):
        buf = grp % 2
        for c in group_copies(grp):
            c.wait()
        for r in range(plan.pages_per_group):
            cols = pl.ds((plan.pages_per_group - 1 - r) * slots, slots)
            kb_ref[:, cols] = pk_ref[buf, r].astype(BF16)
            vb_ref[:, cols] = pv_ref[buf, r].astype(BF16)

        @pl.when(grp + 2 < n_all_groups)
        def _():
            start_group(grp + 2)

        s = jnp.dot(qd_ref[...], kb_ref[...], preferred_element_type=F32)
        dcarry_ref[...], pv = _sb_block(s + biasd_ref[...], dcarry_ref[...], du_ref[...], None, vb_ref[...])
        dacc_ref[...] += pv

    def decode_new_tokens():
        du_ref[...] = _cumsum_rhs(slots)
        r = lax.broadcasted_iota(jnp.int32, (rows_d, slots), 0)
        c = lax.broadcasted_iota(jnp.int32, (rows_d, slots), 1)
        s = jnp.dot(qd_ref[...], kn_ref[...].astype(BF16), preferred_element_type=F32)
        dcarry_ref[...], dacc_ref[...] = _sb_block(
            s + biasd_ref[:, pl.ds(0, slots)], jnp.zeros((rows_d, LANES), F32), du_ref[...], c < (r // N_HEADS),
            vn_ref[...].astype(BF16))

    def decode_finish():
        r = lax.broadcasted_iota(jnp.int32, dacc_ref.shape, 0)
        l = lax.broadcasted_iota(jnp.int32, dacc_ref.shape, 1)
        own = jnp.where((r % N_HEADS) == (l // HEAD_DIM), dacc_ref[...], 0.0)
        for t in range(plan.t_new):
            od_ref[pl.ds(t, 1), :] = jnp.sum(own[t * N_HEADS:(t + 1) * N_HEADS], axis=0, keepdims=True)

    def decode_slot(qb):
        slot = step_in_batch * plan.slots_per_step + qb

        @pl.when((step == 0) & (qb == 0))
        def _():
            start_group(0)
            if n_all_groups > 1:
                start_group(1)

        pl.when(slot == 0)(decode_new_tokens)
        for t in range(plan.groups_per_slot):
            n = slot * plan.groups_per_slot + t
            grp = batch_d * plan.n_groups + n
            if plan.every_slot_full:
                decode_group(grp)
            else:
                pl.when(n < plan.n_groups)(functools.partial(decode_group, grp))
        if plan.every_slot_full:
            decode_finish()
        else:
            pl.when(slot == plan.steps_per_batch * plan.slots_per_step - 1)(decode_finish)

    def qbody(qb, _):
        decode_slot(qb)
        qoff = pl.multiple_of(qb * tile, tile)
        k0 = 2 * qb

        for h in heads:
            qs_ref[h] = qz_ref[h, qb]

        def stage_keys(slot, kb):
            ks_ref[slot] = ka_ref[pl.ds(pl.multiple_of(kb * kw, kw), kw), :]

        def stage_values(slot, kb):
            for h in heads:
                vs_ref[slot, h] = vz_ref[h, kb]

        def logits(slot, z_ref):
            for h in heads:
                z_ref[h] = jnp.dot(ks_ref[slot], qs_ref[h], preferred_element_type=F32)

        def softplus(z_ref, half, mask=None):
            for h in heads:
                _softplus_t(z_ref.at[h], sp_ref.at[half, h], lb_ref.at[half, h], mask)

        def cumsums(half):
            for h in heads:
                cum_ref[half, h] = jnp.dot(lcum_ref[...], sp_ref[half, h], preferred_element_type=F32)

        def stick(half, a_ref, mask=None, first=False):
            for h in heads:
                carry = jnp.zeros((SUBLANES, tile), F32) if first else carry_ref[h]
                carry_ref[h] = _stick_weights_t(lb_ref.at[half, h], cum_ref.at[half, h], carry, a_ref.at[h], mask)

        def values(a_ref, slot, first=False):
            for h in heads:
                pv = jnp.dot(vs_ref[slot, h], a_ref[h], preferred_element_type=F32)
                acc_ref[h] = pv if first else acc_ref[h] + pv


        def weights_and_next_logits(mask_a=None, mask_b=None, first=False):
            softplus(za_ref, 0, mask_a)
            cumsums(0)
            logits(0, za_ref)
            softplus(zb_ref, 1, mask_b)
            logits(1, zb_ref)
            cumsums(1)
            stick(0, aa_ref, mask_a, first)
            stick(1, ab_ref, mask_b)

        acc_ref[...] = jnp.zeros(acc_ref.shape, F32)
        stage_keys(2, k0 + 1)
        stage_keys(3, k0)
        stage_keys(0, jnp.maximum(k0 - 1, 0))
        stage_keys(1, jnp.maximum(k0 - 2, 0))
        logits(2, za_ref)
        logits(3, zb_ref)
        weights_and_next_logits(diag_mask[kw:], diag_mask[:kw], first=True)

        def kbody(i, _):
            ka = k0 - 1 - 2 * i
            stage_keys(0, jnp.maximum(ka - 2, 0))
            stage_keys(1, jnp.maximum(ka - 3, 0))
            stage_values(0, ka + 2)
            stage_values(1, ka + 1)
            values(aa_ref, 0)
            values(ab_ref, 1)
            weights_and_next_logits()
            return 0

        lax.fori_loop(0, qb, kbody, 0)
        stage_values(0, 1)
        stage_values(1, 0)
        values(aa_ref, 0)
        values(ab_ref, 1)
        out = jnp.concatenate([acc_ref[h] for h in heads], axis=0)
        o_ref[pl.ds(qoff, tile), :] = out.T.astype(o_ref.dtype)
        return 0

    lax.fori_loop(0, seq // tile, qbody, 0)


def _attention(qt, kt, vt, bias, q_rows, row_bias, kt_new, vt_new, cache_kt, cache_vt, page_table, t_new):
    batch, n_heads, _, seq = qt.shape
    m, w = batch * seq, n_heads * HEAD_DIM
    npairs = w // LANES
    tile = _row_tile(seq, ATTN_TILE)
    kw = tile // 2
    assert kw <= CUMSUM_GROUP
    group = kw
    nb, rows_d, _ = q_rows.shape
    n_pages, slots = page_table.shape[1], cache_kt.shape[2]
    ppg = max(p for p in (DECODE_PAGES_PER_STEP, 4, 2, 1) if n_pages % p == 0)
    n_steps = batch * npairs
    assert n_steps % nb == 0, (n_steps, nb)
    n_slots = (n_steps // nb) * (seq // tile)
    n_groups = n_pages // ppg
    gps = -(-n_groups // n_slots)
    plan = _DecodePlan(n_batch=nb, t_new=t_new, n_pages=n_pages, pages_per_group=ppg, n_groups=n_groups,
                       steps_per_batch=n_steps // nb, slots_per_step=seq // tile, groups_per_slot=gps,
                       every_slot_full=gps * n_slots == n_groups)
    bias_rows = jnp.broadcast_to(row_bias[:, None], (rows_d, ppg * slots))
    spb = plan.steps_per_batch
    kv_spec = pl.BlockSpec((None, HEADS_PER_BLOCK, HEAD_DIM, seq), lambda b, p, pt: (b, p, 0, 0))
    per_d = lambda b, p, pt: ((b * npairs + p) // spb, 0, 0)
    grid_spec = pltpu.PrefetchScalarGridSpec(
        num_scalar_prefetch=1,
        grid=(batch, npairs),
        in_specs=[
            pl.BlockSpec(memory_space=pltpu.SMEM),
            kv_spec,
            kv_spec,
            kv_spec,
            pl.BlockSpec((None, rows_d, w), per_d),
            pl.BlockSpec((rows_d, ppg * slots), lambda b, p, pt: (0, 0)),
            pl.BlockSpec((None, w, slots), per_d),
            pl.BlockSpec((None, w, slots), per_d),
            pl.BlockSpec(memory_space=pl.ANY),
            pl.BlockSpec(memory_space=pl.ANY),
        ],
        out_specs=[pl.BlockSpec((seq, LANES), lambda b, p, pt: (b, p)),
                   pl.BlockSpec((None, t_new, w), per_d)],
        scratch_shapes=[
            pltpu.VMEM((seq, 2 * LANES), BF16),
            pltpu.VMEM((HEADS_PER_BLOCK, seq // tile, 2 * LANES, tile), BF16),
            pltpu.VMEM((HEADS_PER_BLOCK, seq // kw, HEAD_DIM, kw), BF16),
            pltpu.VMEM((group + 2 * SUBLANES, group), BF16),
            pltpu.VMEM((HEADS_PER_BLOCK, SUBLANES, tile), F32),
            pltpu.VMEM((HEADS_PER_BLOCK, HEAD_DIM, tile), F32),
            pltpu.VMEM((HEADS_PER_BLOCK, kw, tile), F32),
            pltpu.VMEM((HEADS_PER_BLOCK, kw, tile), F32),
            pltpu.VMEM((HEADS_PER_BLOCK, kw, tile), BF16),
            pltpu.VMEM((HEADS_PER_BLOCK, kw, tile), BF16),
            pltpu.VMEM((2, HEADS_PER_BLOCK, kw, tile), BF16),
            pltpu.VMEM((2, HEADS_PER_BLOCK, kw, tile), F32),
            pltpu.VMEM((2, HEADS_PER_BLOCK, kw + 2 * SUBLANES, tile), F32),
            pltpu.VMEM((HEADS_PER_BLOCK, 2 * LANES, tile), BF16),
            pltpu.VMEM((4, kw, 2 * LANES), BF16),
            pltpu.VMEM((2, HEADS_PER_BLOCK, HEAD_DIM, kw), BF16),
            pltpu.VMEM((2, ppg, w, slots), F32),
            pltpu.VMEM((2, ppg, w, slots), F32),
            pltpu.SemaphoreType.DMA((2, 2, ppg)),
            pltpu.VMEM((w, ppg * slots), BF16),
            pltpu.VMEM((w, ppg * slots), BF16),
            pltpu.VMEM((rows_d, w), F32),
            pltpu.VMEM((rows_d, LANES), F32),
            pltpu.VMEM((slots, slots + LANES), BF16),
        ],
    )
    return pl.pallas_call(
        functools.partial(_attn_kernel, seq=seq, tile=tile, plan=plan),
        grid_spec=grid_spec,
        out_shape=[jax.ShapeDtypeStruct((m, w), BF16), jax.ShapeDtypeStruct((nb, t_new, w), F32)],
        compiler_params=pltpu.CompilerParams(dimension_semantics=("arbitrary", "arbitrary"),
                                             vmem_limit_bytes=ATTN_VMEM_LIMIT),
        name="attention",
    )(page_table, bias, qt, kt, vt, q_rows, bias_rows, kt_new, vt_new, cache_kt, cache_vt)


def _pool_tile(ext_ref, p_ref, tp, pos0):
    gc = ext_ref.shape[1] // POOL_GROUPS
    pos = pos0 + lax.broadcasted_iota(jnp.int32, (tp, 1), 0)
    for g, win in enumerate(POOL_WINDOWS):
        cols = slice(g * gc, (g + 1) * gc)
        u_new = ext_ref[pl.ds(POOL_HALO, tp), cols]
        win_sum = u_new
        for d in range(1, win):
            win_sum = win_sum + ext_ref[pl.ds(POOL_HALO - d, tp), cols]
        cnt = jnp.minimum(win, pos + 1).astype(F32)
        p_ref[:, cols] = (win_sum / cnt - u_new).astype(p_ref.dtype)


def _pool_prompt_kernel(u_ref, halo_ref, p_ref, ext_ref, *, tp):
    i = pl.program_id(1)
    ext_ref[pl.ds(0, POOL_HALO), :] = jnp.where(i == 0, 0.0, halo_ref[...])
    ext_ref[pl.ds(POOL_HALO, tp), :] = u_ref[...]
    _pool_tile(ext_ref, p_ref, tp, i * tp)


def _pool_prompt(u, batch, seq, tp_cap=512):
    m, w = u.shape
    tp = _row_tile(seq, tp_cap)
    nt = seq // tp
    hb = tp // POOL_HALO
    return pl.pallas_call(
        functools.partial(_pool_prompt_kernel, tp=tp),
        grid=(batch, nt),
        in_specs=[
            pl.BlockSpec((tp, w), lambda b, i: (b * nt + i, 0)),
            pl.BlockSpec((POOL_HALO, w), lambda b, i: (jnp.maximum((b * nt + i) * hb - 1, 0), 0)),
        ],
        out_specs=pl.BlockSpec((tp, w), lambda b, i: (b * nt + i, 0)),
        out_shape=jax.ShapeDtypeStruct((m, w), BF16),
        scratch_shapes=[pltpu.VMEM((POOL_HALO + tp, w), F32)],
        compiler_params=_cparams("parallel", "arbitrary"),
        name="pool_prompt",
    )(u, u)


def _pool_sample_kernel(u_ref, halo_ref, p_ref, ext_ref, *, tp, pos0):
    ext_ref[pl.ds(0, POOL_HALO), :] = halo_ref[...]
    ext_ref[pl.ds(POOL_HALO, tp), :] = u_ref[...]
    _pool_tile(ext_ref, p_ref, tp, pos0)


def _pool_sample(u3, halo3, pos0):
    nb, tp, w = u3.shape
    return pl.pallas_call(
        functools.partial(_pool_sample_kernel, tp=tp, pos0=pos0),
        grid=(nb,),
        in_specs=[
            pl.BlockSpec((None, tp, w), lambda b: (b, 0, 0)),
            pl.BlockSpec((None, POOL_HALO, w), lambda b: (b, 0, 0)),
        ],
        out_specs=pl.BlockSpec((None, tp, w), lambda b: (b, 0, 0)),
        out_shape=jax.ShapeDtypeStruct((nb, tp, w), F32),
        scratch_shapes=[pltpu.VMEM((POOL_HALO + tp, w), F32)],
        compiler_params=_cparams("parallel"),
        name="pool_sample",
    )(u3, halo3)


def _inproj(xns, w_in, col0, ncols, *, mode, out_dtype, tn, head_gain=None, seq=0):
    d = w_in.shape[0]
    jb = col0 // tn
    assert col0 % tn == 0
    consts, epilogue = (), {"plain": _epi_plain, "sigmoid": _epi_sigmoid, "headnorm": _epi_headnorm}[mode]
    if mode == "headnorm":
        hid = jnp.arange(HEADNORM_GROUP, dtype=jnp.int32) // HEAD_DIM
        bd = (hid[:, None] == hid[None, :]).astype(BF16)
        gain = jnp.tile(head_gain.astype(F32), tn // HEAD_DIM).reshape(1, tn)
        consts = ((bd, bd.shape, lambda j: (0, 0)), (gain, (1, tn), lambda j: (0, 0)))
    row_sets = [_Rows(lhs=((xn, d, lambda j: 0),), head_major_seq=seq if s == 0 else 0)
                for s, xn in enumerate(xns)]
    return _ws_matmul("inproj_" + mode + ("_t" if seq else ""), ((w_in, d, lambda j: (0, j + jb)),), (0,),
                      consts, row_sets, epilogue, ncols, tn, out_dtype, tm_cap=1024)


def _project(xns, wts, seq):
    w_in = wts["w_in"]
    d = w_in.shape[0]
    aw, pw = ATTN_WIDTH, d // 2
    tn = INPROJ_COLS
    q = _inproj(xns, w_in, 0, aw, mode="headnorm", out_dtype=BF16, tn=tn,
                head_gain=wts["q_norm_g"] * (HEAD_DIM ** -0.5 * LOG2E), seq=seq)
    k = _inproj(xns, w_in, aw, aw, mode="headnorm", out_dtype=F32, tn=tn, head_gain=wts["k_norm_g"], seq=seq)
    v = _inproj(xns, w_in, 2 * aw, aw, mode="plain", out_dtype=F32, tn=tn, seq=seq)
    u = _inproj(xns, w_in, 3 * aw, pw, mode="plain", out_dtype=F32, tn=tn)
    gates = _inproj(xns, w_in, 3 * aw + pw, 2 * d, mode="sigmoid", out_dtype=BF16, tn=tn)
    return q, k, v, u, gates


def _mix_kernel(o_ref, p_ref, ga_ref, gb_ref, x_ref, wap_ref, wp_ref, ps_ref, wo_ref, g2_ref, h_ref, hn_ref):
    a = jnp.dot(o_ref[...], wap_ref[...], preferred_element_type=F32)
    groups, gc, _ = wp_ref.shape
    op = jnp.concatenate(
        [jnp.dot(p_ref[:, g * gc:(g + 1) * gc], wp_ref[g], preferred_element_type=F32) for g in range(groups)],
        axis=1)
    merged = ga_ref[...].astype(F32) * a + gb_ref[...].astype(F32) * (op * ps_ref[...])
    h = x_ref[...] + jnp.dot(merged.astype(BF16), wo_ref[...], preferred_element_type=F32)
    h_ref[...] = h
    ms = jnp.mean(h * h, axis=-1, keepdims=True)
    hn_ref[...] = ((h * lax.rsqrt(ms + RMS_EPS)) * g2_ref[...]).astype(hn_ref.dtype)


def _mix(x, o_attn, p, gates, w_ap, w_pool, pool_scale, w_out, norm2_g, tm_cap=256):
    m, d = x.shape
    aw, pw = o_attn.shape[1], p.shape[1]
    tm = _row_tile(m, tm_cap)
    whole = lambda a: pl.BlockSpec(a.shape, lambda i: (0,) * a.ndim)
    ps2, g2 = pool_scale.reshape(1, d), norm2_g.reshape(1, d)
    return pl.pallas_call(
        _mix_kernel,
        grid=(m // tm,),
        in_specs=[
            pl.BlockSpec((tm, aw), lambda i: (i, 0)),
            pl.BlockSpec((tm, pw), lambda i: (i, 0)),
            pl.BlockSpec((tm, d), lambda i: (i, 0)),
            pl.BlockSpec((tm, d), lambda i: (i, 1)),
            pl.BlockSpec((tm, d), lambda i: (i, 0)),
            whole(w_ap), whole(w_pool), whole(ps2), whole(w_out), whole(g2),
        ],
        out_specs=[pl.BlockSpec((tm, d), lambda i: (i, 0)), pl.BlockSpec((tm, d), lambda i: (i, 0))],
        out_shape=[jax.ShapeDtypeStruct((m, d), F32), jax.ShapeDtypeStruct((m, d), BF16)],
        compiler_params=_cparams("parallel"),
        name="mix",
    )(o_attn, p, gates, gates, x, w_ap, w_pool, ps2, w_out, g2)


def _mix_and_ffn(xs, o_attns, ps, gates, wts):
    w_gu, w_down = wts["w_gate_up"], wts["w_down"]
    dff, d = w_down.shape
    w_ap, w_pool, w_out = (wts[n].astype(BF16) for n in ("w_attn_proj", "w_pool", "w_out"))
    hs, hns = zip(*[_mix(x, o, p, gt, w_ap, w_pool, wts["pool_scale"], w_out, wts["norm2_g"])
                    for x, o, p, gt in zip(xs, o_attns, ps, gates)])
    nj = dff // 512
    acts = _ws_matmul(
        "gate_up", ((w_gu, d, lambda j: (0, j)), (w_gu, d, lambda j: (0, j + nj))), (0, 0), (),
        [_Rows(lhs=((hn, d, lambda j: 0),)) for hn in hns],
        _epi_swiglu, dff, 512, BF16, tm_cap=1024)
    return _ws_matmul(
        "down_proj", ((w_down, dff, lambda j: (0, j)),), (0,), (),
        [_Rows(lhs=((act, dff, lambda j: 0),), extras=((h, lambda j: j),)) for act, h in zip(acts, hs)],
        _epi_residual, d, 512, F32, tm_cap=512)


def kernel(x_prompt, x_sample, cache_k, cache_v, state_pool, page_table, norm1_g, w_in,
           q_norm_g, k_norm_g, sb_bias, w_attn_proj, w_pool, pool_scale, w_out, norm2_g, w_gate_up, w_down):
    b_p, seq, d = x_prompt.shape
    b_s, t_s = x_sample.shape[:2]
    depth, n_pool, page = cache_k.shape[:3]
    n_pages = page_table.shape[1]
    past_len = n_pages * page
    pw = d // 2

    xp = x_prompt.reshape(b_p * seq, d)
    xs = x_sample.reshape(b_s * t_s, d)
    outs = {name: [] for name in ("kp", "vp", "up", "ks", "vs", "us")}
    for l in range(depth):
        wts = {
            "w_in": w_in[l], "q_norm_g": q_norm_g[l], "k_norm_g": k_norm_g[l], "w_attn_proj": w_attn_proj[l],
            "w_pool": w_pool[l], "pool_scale": pool_scale[l], "w_out": w_out[l], "norm2_g": norm2_g[l],
            "w_gate_up": w_gate_up[l], "w_down": w_down[l],
        }
        bias = sb_bias[l].astype(F32) * LOG2E

        xns = [_rmsnorm(xp, norm1_g[l]), _rmsnorm(xs, norm1_g[l])]
        (qt_p, q_s), (kt_p, k_s), (vt_p, v_s), (u_p, u_s), (gates_p, gates_s) = _project(xns, wts, seq)

        q4 = q_s.reshape(b_s, t_s, 1, N_HEADS, HEAD_DIM)
        eye = jnp.eye(N_HEADS, dtype=BF16).reshape(1, 1, N_HEADS, N_HEADS, 1)
        q_rows = (q4 * eye).reshape(b_s, t_s * N_HEADS, ATTN_WIDTH)
        pad = ((0, 0), (0, 0), (0, page - t_s))
        kt_new = jnp.pad(k_s.reshape(b_s, t_s, ATTN_WIDTH).transpose(0, 2, 1), pad)
        vt_new = jnp.pad(v_s.reshape(b_s, t_s, ATTN_WIDTH).transpose(0, 2, 1), pad)
        cache_kt = cache_k[l].transpose(0, 2, 3, 1).reshape(n_pool, ATTN_WIDTH, page)
        cache_vt = cache_v[l].transpose(0, 2, 3, 1).reshape(n_pool, ATTN_WIDTH, page)
        o_attn_p, o_attn_s = _attention(qt_p, kt_p, vt_p, bias, q_rows, jnp.tile(bias, t_s), kt_new, vt_new,
                                        cache_kt, cache_vt, page_table, t_s)
        o_attn_s = o_attn_s.reshape(b_s * t_s, ATTN_WIDTH).astype(BF16)
        outs["kp"].append(kt_p.transpose(0, 3, 1, 2))
        outs["vp"].append(vt_p.transpose(0, 3, 1, 2))

        p_p = _pool_prompt(u_p, b_p, seq)
        outs["up"].append(u_p.reshape(b_p, seq, pw)[:, seq - POOL_STATE:])
        u3 = u_s.reshape(b_s, t_s, pw)
        halo = jnp.pad(state_pool[l], ((0, 0), (POOL_HALO - POOL_STATE, 0), (0, 0)))
        p_s = _pool_sample(u3, halo, past_len).reshape(b_s * t_s, pw).astype(BF16)
        outs["ks"].append(k_s.reshape(b_s, t_s, N_HEADS, HEAD_DIM))
        outs["vs"].append(v_s.reshape(b_s, t_s, N_HEADS, HEAD_DIM))
        outs["us"].append(jnp.concatenate([state_pool[l], u3], axis=1)[:, -POOL_STATE:])

        xp, xs = _mix_and_ffn([xp, xs], [o_attn_p, o_attn_s], [p_p, p_s], [gates_p, gates_s], wts)

    st = lambda name: jnp.stack(outs[name], axis=0)
    return (xp.reshape(b_p, seq, d), xs.reshape(b_s, t_s, d),
            st("kp"), st("vp"), st("up"), st("ks"), st("vs"), st("us"))
```

```python
import functools
from typing import NamedTuple

import jax
import jax.numpy as jnp
from jax import lax
from jax.experimental import pallas as pl
from jax.experimental.pallas import tpu as pltpu

F32 = jnp.float32
BF16 = jnp.bfloat16

N_HEADS = 16
HEAD_DIM = 64
ATTN_WIDTH = N_HEADS * HEAD_DIM
POOL_WINDOWS = (2, 4, 8, 16)
POOL_GROUPS = len(POOL_WINDOWS)
POOL_STATE = max(POOL_WINDOWS) - 1
POOL_HALO = 16
RMS_EPS = 1e-6

LANES = 128
SUBLANES = 8
HEADS_PER_BLOCK = LANES // HEAD_DIM
VMEM_LIMIT = 52 * 1024 * 1024
ATTN_VMEM_LIMIT = 57 * 1024 * 1024
ATTN_TILE = 512
CUMSUM_GROUP = 256
BIAS_TERMS = 3
BIAS_ROWS = 16
WEIGHTS_CHUNK = 32
DECODE_PAGES_PER_STEP = 8
INPROJ_COLS = 1024
HEADNORM_GROUP = 256
LOG2E = 1.4426950408889634

NT_DIMS = (((1,), (1,)), ((), ()))


def _cparams(*sem):
    return pltpu.CompilerParams(dimension_semantics=sem, vmem_limit_bytes=VMEM_LIMIT)


def _row_tile(m, cap):
    t = min(m, cap)
    while m % t:
        t //= 2
    return t


def _rmsnorm_kernel(x_ref, g_ref, o_ref):
    x = x_ref[...]
    ms = jnp.mean(x * x, axis=-1, keepdims=True)
    o_ref[...] = ((x * lax.rsqrt(ms + RMS_EPS)) * g_ref[...]).astype(o_ref.dtype)


def _rmsnorm(x2d, g, tm_cap=512):
    m, d = x2d.shape
    tm = _row_tile(m, tm_cap)
    return pl.pallas_call(
        _rmsnorm_kernel,
        grid=(m // tm,),
        in_specs=[pl.BlockSpec((tm, d), lambda i: (i, 0)), pl.BlockSpec((1, d), lambda i: (0, 0))],
        out_specs=pl.BlockSpec((tm, d), lambda i: (i, 0)),
        out_shape=jax.ShapeDtypeStruct((m, d), BF16),
        compiler_params=_cparams("parallel"),
        name="rmsnorm",
    )(x2d, g.reshape(1, d))


class _Rows(NamedTuple):
    lhs: tuple
    extras: tuple = ()
    head_major_seq: int = 0


def _ws_kernel(*refs, lhs_of_w, n_const, sets, epilogue):
    n_w = len(lhs_of_w)
    w_refs, refs = refs[:n_w], refs[n_w:]
    const_refs, refs = refs[:n_const], refs[n_const:]
    set_refs = []
    for n_lhs, n_extra, _ in sets:
        set_refs.append((refs[:n_lhs], refs[n_lhs:n_lhs + n_extra]))
        refs = refs[n_lhs + n_extra:]
    out_refs, wb_refs = refs[:len(sets)], refs[len(sets):]

    def compute(lhs_refs, extra_refs, o_ref, head_major):
        accs = [jnp.dot(lhs_refs[lhs_of_w[k]][...], wb_refs[k][...], preferred_element_type=F32)
                for k in range(n_w)]
        r = epilogue(accs, [e[...] for e in extra_refs], [c[...] for c in const_refs])
        if head_major:
            r = r.T.reshape(o_ref.shape)
        o_ref[...] = r.astype(o_ref.dtype)

    @pl.when(pl.program_id(1) == 0)
    def _():
        for w_ref, wb_ref in zip(w_refs, wb_refs):
            wb_ref[...] = w_ref[...].astype(BF16)
        for (lhs_refs, extra_refs), o_ref, (_, _, head_major) in zip(set_refs[1:], out_refs[1:], sets[1:]):
            compute(lhs_refs, extra_refs, o_ref, head_major)

    compute(*set_refs[0], out_refs[0], sets[0][2])


def _ws_matmul(name, weights, lhs_of_w, consts, row_sets, epilogue, n_cols, tn, out_dtype, tm_cap):
    assert n_cols % tn == 0
    m0 = row_sets[0].lhs[0][0].shape[0]
    tm0 = _row_tile(row_sets[0].head_major_seq or m0, tm_cap)
    in_specs, args = [], []
    for w, kb, idx in weights:
        in_specs.append(pl.BlockSpec((kb, tn), lambda j, i, idx=idx: idx(j)))
        args.append(w)
    for c, shape, idx in consts:
        in_specs.append(pl.BlockSpec(shape, lambda j, i, idx=idx: idx(j)))
        args.append(c)
    out_specs, out_shapes, sets = [], [], []
    for s, rows in enumerate(row_sets):
        m = rows.lhs[0][0].shape[0]
        tm = tm0 if s == 0 else m
        row = (lambda i: i) if s == 0 else (lambda i: 0)
        for arr, kb, kidx in rows.lhs:
            in_specs.append(pl.BlockSpec((tm, kb), lambda j, i, row=row, kidx=kidx: (row(i), kidx(j))))
            args.append(arr)
        for arr, cidx in rows.extras:
            in_specs.append(pl.BlockSpec((tm, tn), lambda j, i, row=row, cidx=cidx: (row(i), cidx(j))))
            args.append(arr)
        if rows.head_major_seq:
            assert s == 0
            seq = rows.head_major_seq
            nt, hpt = seq // tm, tn // HEAD_DIM
            out_specs.append(pl.BlockSpec((None, hpt, HEAD_DIM, tm), lambda j, i, nt=nt: (i // nt, j, 0, i % nt)))
            out_shapes.append(jax.ShapeDtypeStruct((m // seq, n_cols // HEAD_DIM, HEAD_DIM, seq), out_dtype))
        else:
            out_specs.append(pl.BlockSpec((tm, tn), lambda j, i, row=row: (row(i), j)))
            out_shapes.append(jax.ShapeDtypeStruct((m, n_cols), out_dtype))
        sets.append((len(rows.lhs), len(rows.extras), bool(rows.head_major_seq)))
    return pl.pallas_call(
        functools.partial(_ws_kernel, lhs_of_w=tuple(lhs_of_w), n_const=len(consts), sets=tuple(sets),
                          epilogue=epilogue),
        grid=(n_cols // tn, m0 // tm0),
        in_specs=in_specs,
        out_specs=out_specs,
        out_shape=out_shapes,
        scratch_shapes=[pltpu.VMEM((kb, tn), BF16) for _, kb, _ in weights],
        compiler_params=_cparams("parallel", "arbitrary"),
        name=name,
    )(*args)


def _epi_plain(accs, extras, consts):
    return accs[0]


def _epi_headnorm(accs, extras, consts):
    h, (bd, gain) = accs[0], consts
    w = bd.shape[0]
    hh = (h * h).astype(BF16)
    ss = [jnp.dot(hh[:, c:c + w], bd, preferred_element_type=F32) for c in range(0, h.shape[1], w)]
    ss = ss[0] if len(ss) == 1 else jnp.concatenate(ss, axis=1)
    return (h * lax.rsqrt(ss * (1.0 / HEAD_DIM) + RMS_EPS)) * gain


def _epi_sigmoid(accs, extras, consts):
    return 1.0 / (1.0 + jnp.exp(-accs[0]))


def _epi_residual(accs, extras, consts):
    return extras[0] + accs[0]


def _epi_swiglu(accs, extras, consts):
    gate, up = accs
    return (gate / (1.0 + jnp.exp(-gate))) * up


def _cumsum_rhs(g):
    r = lax.broadcasted_iota(jnp.int32, (g, g + LANES), 0)
    c = lax.broadcasted_iota(jnp.int32, (g, g + LANES), 1)
    return jnp.where((r > c) | (c >= g), 1.0, 0.0).astype(BF16)


def _sb_weights(z, carry, u, mask):
    g = u.shape[0]
    neg_abs = lax.bitcast_convert_type(lax.bitcast_convert_type(z, jnp.uint32) | jnp.uint32(1 << 31), F32)
    sp = jnp.maximum(z, 0.0) + jnp.log(1.0 + jnp.exp2(neg_abs)) * LOG2E
    if mask is not None:
        sp = jnp.where(mask, sp, 0.0)
    log_beta = z - sp
    sp16 = sp.astype(BF16)
    parts = []
    for c in reversed(range(z.shape[1] // g)):
        cols = slice(c * g, (c + 1) * g)
        cum = jnp.dot(sp16[:, cols], u, preferred_element_type=F32)
        for l in reversed(range(0, g, LANES)):
            parts.append(jnp.exp2(log_beta[:, c * g + l:c * g + l + LANES] - cum[:, l:l + LANES] - carry))
        carry = carry + cum[:, g:]
    a = parts[0] if len(parts) == 1 else jnp.concatenate(parts[::-1], axis=1)
    if mask is not None:
        a = jnp.where(mask, a, 0.0)
    return carry, a.astype(BF16)


def _cumsum_lhs(g):
    r = lax.broadcasted_iota(jnp.int32, (g + 2 * SUBLANES, g), 0)
    c = lax.broadcasted_iota(jnp.int32, (g + 2 * SUBLANES, g), 1)
    return jnp.where((c > r) | (r >= g), 1.0, 0.0).astype(BF16)


def _softplus_t(z, sp_ref, lb_ref, mask):
    ch = WEIGHTS_CHUNK
    for r0 in range(0, z.shape[0], ch):
        rows = pl.ds(r0, ch)
        zt = z[r0:r0 + ch]
        neg_abs = lax.bitcast_convert_type(lax.bitcast_convert_type(zt, jnp.uint32) | jnp.uint32(1 << 31), F32)
        sp = jnp.maximum(zt, 0.0) + jnp.log(1.0 + jnp.exp2(neg_abs)) * LOG2E
        if mask is not None:
            sp = jnp.where(mask[r0:r0 + ch], sp, 0.0)
        sp_ref[rows, :] = sp.astype(BF16)
        lb_ref[rows, :] = zt - sp


def _stick_weights_t(lb_ref, cum_ref, carry, a_ref, mask):
    g, nq = lb_ref.shape
    ch = WEIGHTS_CHUNK
    for r0 in range(0, g, ch):
        rows = pl.ds(r0, ch)
        x = (lb_ref[rows, :] - cum_ref[rows, :]).reshape(ch // SUBLANES, SUBLANES, nq) - carry[None]
        a = jnp.exp2(x).reshape(ch, nq)
        if mask is not None:
            a = jnp.where(mask[r0:r0 + ch], a, 0.0)
        a_ref[rows, :] = a.astype(BF16)
    return carry + cum_ref[pl.ds(g, SUBLANES), :]


def _sb_block(z, carry, u, mask, vt):
    carry, a = _sb_weights(z, carry, u, mask)
    return carry, lax.dot_general(a, vt, NT_DIMS, preferred_element_type=F32)


class _DecodePlan(NamedTuple):
    n_batch: int
    t_new: int
    n_pages: int
    pages_per_group: int
    n_groups: int
    steps_per_batch: int
    slots_per_step: int
    groups_per_slot: int
    every_slot_full: bool


def _attn_kernel(pt_ref, bias_ref, qt_ref, kt_ref, vt_ref, qd_ref, biasd_ref, kn_ref, vn_ref, ck_hbm, cv_hbm,
                 o_ref, od_ref, ka_ref, qz_ref, vz_ref, lcum_ref, carry_ref, acc_ref, aa_ref, ab_ref,
                 sp_ref, lb_ref, cum_ref, qs_ref, ks_ref, vs_ref, pk_ref, pv_ref, sem, kb_ref, vb_ref, dacc_ref, dcarry_ref, du_ref, *, seq, tile, plan):
    pair = pl.program_id(1)
    step = pl.program_id(0) * pl.num_programs(1) + pair
    kw = tile // 2
    nkb = seq // kw
    lcum_ref[...] = _cumsum_lhs(lcum_ref.shape[1])
    ka_ref[:, pl.ds(LANES, LANES)] = jnp.ones((seq, LANES), BF16)
    for cb in range(nkb):
        cols = pl.ds(cb * kw, kw)
        ka_ref[cols, pl.ds(0, LANES)] = kt_ref[:, :, cols].reshape(LANES, kw).T.astype(BF16)
    qz_ref[...] = jnp.zeros(qz_ref.shape, BF16)
    term_row = lax.broadcasted_iota(jnp.int32, (BIAS_ROWS, tile), 0)
    for h in range(HEADS_PER_BLOCK):
        rows = pl.ds(h * HEAD_DIM, HEAD_DIM)
        rest = jnp.full((BIAS_ROWS, tile), bias_ref[pair * HEADS_PER_BLOCK + h], F32)
        terms = jnp.zeros((BIAS_ROWS, tile), F32)
        for i in range(BIAS_TERMS):
            term = rest.astype(BF16).astype(F32)
            terms = jnp.where(term_row == i, term, terms)
            rest = rest - term
        for qb in range(seq // tile):
            cols = pl.ds(qb * tile, tile)
            qz_ref[h, qb, rows, :] = qt_ref[h, :, cols]
            qz_ref[h, qb, pl.ds(LANES, BIAS_ROWS), :] = terms.astype(BF16)
        for cb in range(nkb):
            vz_ref[h, cb] = vt_ref[h, :, pl.ds(cb * kw, kw)].astype(BF16)

    rk = lax.broadcasted_iota(jnp.int32, (tile, tile), 0)
    cq = lax.broadcasted_iota(jnp.int32, (tile, tile), 1)
    diag_mask = rk < cq

    heads = range(HEADS_PER_BLOCK)

    rows_d, slots = qd_ref.shape[0], kn_ref.shape[1]
    n_all_groups = plan.n_batch * plan.n_groups
    step_in_batch = step % plan.steps_per_batch
    batch_d = step // plan.steps_per_batch

    def group_copies(grp):
        bd, n, buf = grp // plan.n_groups, grp % plan.n_groups, grp % 2
        copies = []
        for r in range(plan.pages_per_group):
            page = pt_ref[bd, plan.n_pages - 1 - n * plan.pages_per_group - r]
            copies.append(pltpu.make_async_copy(ck_hbm.at[page], pk_ref.at[buf, r], sem.at[buf, 0, r]))
            copies.append(pltpu.make_async_copy(cv_hbm.at[page], pv_ref.at[buf, r], sem.at[buf, 1, r]))
        return copies

    def start_group(grp):
        for c in group_copies(grp):
            c.start()

    def decode_group(grp):
        buf = grp % 2
        for c in group_copies(grp):
            c.wait()

        @pl.when(grp + 1 < n_all_groups)
        def _():
            start_group(grp + 1)

        for r in range(plan.pages_per_group):
            cols = pl.ds((plan.pages_per_group - 1 - r) * slots, slots)
            kb_ref[:, cols] = pk_ref[buf, r].astype(BF16)
            vb_ref[:, cols] = pv_ref[buf, r].astype(BF16)
        s = jnp.dot(qd_ref[...], kb_ref[...], preferred_element_type=F32)
        dcarry_ref[...], pv = _sb_block(s + biasd_ref[...], dcarry_ref[...], du_ref[...], None, vb_ref[...])
        dacc_ref[...] += pv

    def decode_new_tokens():
        du_ref[...] = _cumsum_rhs(slots)
        r = lax.broadcasted_iota(jnp.int32, (rows_d, slots), 0)
        c = lax.broadcasted_iota(jnp.int32, (rows_d, slots), 1)
        s = jnp.dot(qd_ref[...], kn_ref[...].astype(BF16), preferred_element_type=F32)
        dcarry_ref[...], dacc_ref[...] = _sb_block(
            s + biasd_ref[:, pl.ds(0, slots)], jnp.zeros((rows_d, LANES), F32), du_ref[...], c < (r // N_HEADS),
            vn_ref[...].astype(BF16))

    def decode_finish():
        r = lax.broadcasted_iota(jnp.int32, dacc_ref.shape, 0)
        l = lax.broadcasted_iota(jnp.int32, dacc_ref.shape, 1)
        own = jnp.where((r % N_HEADS) == (l // HEAD_DIM), dacc_ref[...], 0.0)
        for t in range(plan.t_new):
            od_ref[pl.ds(t, 1), :] = jnp.sum(own[t * N_HEADS:(t + 1) * N_HEADS], axis=0, keepdims=True)

    def decode_slot(qb):
        slot = step_in_batch * plan.slots_per_step + qb

        @pl.when((step == 0) & (qb == 0))
        def _():
            start_group(0)

        pl.when(slot == 0)(decode_new_tokens)
        for t in range(plan.groups_per_slot):
            n = slot * plan.groups_per_slot + t
            grp = batch_d * plan.n_groups + n
            if plan.every_slot_full:
                decode_group(grp)
            else:
                pl.when(n < plan.n_groups)(functools.partial(decode_group, grp))
        if plan.every_slot_full:
            decode_finish()
        else:
            pl.when(slot == plan.steps_per_batch * plan.slots_per_step - 1)(decode_finish)

    def qbody(qb, _):
        decode_slot(qb)
        qoff = pl.multiple_of(qb * tile, tile)
        k0 = 2 * qb

        for h in heads:
            qs_ref[h] = qz_ref[h, qb]

        def stage_keys(slot, kb):
            ks_ref[slot] = ka_ref[pl.ds(pl.multiple_of(kb * kw, kw), kw), :]

        def stage_values(slot, kb):
            for h in heads:
                vs_ref[slot, h] = vz_ref[h, kb]

        def softplus(half, mask=None):
            for h in heads:
                z = jnp.dot(ks_ref[half], qs_ref[h], preferred_element_type=F32)
                _softplus_t(z, sp_ref.at[half, h], lb_ref.at[half, h], mask)

        def cumsums(half):
            for h in heads:
                cum_ref[half, h] = jnp.dot(lcum_ref[...], sp_ref[half, h], preferred_element_type=F32)

        def stick(half, a_ref, mask=None, first=False):
            for h in heads:
                carry = jnp.zeros((SUBLANES, tile), F32) if first else carry_ref[h]
                carry_ref[h] = _stick_weights_t(lb_ref.at[half, h], cum_ref.at[half, h], carry, a_ref.at[h], mask)

        def values(a_ref, slot, first=False):
            for h in heads:
                pv = jnp.dot(vs_ref[slot, h], a_ref[h], preferred_element_type=F32)
                acc_ref[h] = pv if first else acc_ref[h] + pv


        def weights(mask_a=None, mask_b=None, first=False):
            softplus(0, mask_a)
            cumsums(0)
            softplus(1, mask_b)
            cumsums(1)
            stick(0, aa_ref, mask_a, first)
            stick(1, ab_ref, mask_b)

        acc_ref[...] = jnp.zeros(acc_ref.shape, F32)
        stage_keys(0, k0 + 1)
        stage_keys(1, k0)
        weights(diag_mask[kw:], diag_mask[:kw], first=True)

        def kbody(i, _):
            ka = k0 - 1 - 2 * i
            stage_keys(0, ka)
            stage_keys(1, ka - 1)
            stage_values(0, ka + 2)
            stage_values(1, ka + 1)
            values(aa_ref, 0)
            values(ab_ref, 1)
            weights()
            return 0

        lax.fori_loop(0, qb, kbody, 0)
        stage_values(0, 1)
        stage_values(1, 0)
        values(aa_ref, 0)
        values(ab_ref, 1)
        out = jnp.concatenate([acc_ref[h] for h in heads], axis=0)
        o_ref[pl.ds(qoff, tile), :] = out.T.astype(o_ref.dtype)
        return 0

    lax.fori_loop(0, seq // tile, qbody, 0)


def _attention(qt, kt, vt, bias, q_rows, row_bias, kt_new, vt_new, cache_kt, cache_vt, page_table, t_new):
    batch, n_heads, _, seq = qt.shape
    m, w = batch * seq, n_heads * HEAD_DIM
    npairs = w // LANES
    tile = _row_tile(seq, ATTN_TILE)
    kw = tile // 2
    assert kw <= CUMSUM_GROUP
    group = kw
    nb, rows_d, _ = q_rows.shape
    n_pages, slots = page_table.shape[1], cache_kt.shape[2]
    ppg = max(p for p in (DECODE_PAGES_PER_STEP, 4, 2, 1) if n_pages % p == 0)
    n_steps = batch * npairs
    assert n_steps % nb == 0, (n_steps, nb)
    n_slots = (n_steps // nb) * (seq // tile)
    n_groups = n_pages // ppg
    gps = -(-n_groups // n_slots)
    plan = _DecodePlan(n_batch=nb, t_new=t_new, n_pages=n_pages, pages_per_group=ppg, n_groups=n_groups,
                       steps_per_batch=n_steps // nb, slots_per_step=seq // tile, groups_per_slot=gps,
                       every_slot_full=gps * n_slots == n_groups)
    bias_rows = jnp.broadcast_to(row_bias[:, None], (rows_d, ppg * slots))
    spb = plan.steps_per_batch
    kv_spec = pl.BlockSpec((None, HEADS_PER_BLOCK, HEAD_DIM, seq), lambda b, p, pt: (b, p, 0, 0))
    per_d = lambda b, p, pt: ((b * npairs + p) // spb, 0, 0)
    grid_spec = pltpu.PrefetchScalarGridSpec(
        num_scalar_prefetch=1,
        grid=(batch, npairs),
        in_specs=[
            pl.BlockSpec(memory_space=pltpu.SMEM),
            kv_spec,
            kv_spec,
            kv_spec,
            pl.BlockSpec((None, rows_d, w), per_d),
            pl.BlockSpec((rows_d, ppg * slots), lambda b, p, pt: (0, 0)),
            pl.BlockSpec((None, w, slots), per_d),
            pl.BlockSpec((None, w, slots), per_d),
            pl.BlockSpec(memory_space=pl.ANY),
            pl.BlockSpec(memory_space=pl.ANY),
        ],
        out_specs=[pl.BlockSpec((seq, LANES), lambda b, p, pt: (b, p)),
                   pl.BlockSpec((None, t_new, w), per_d)],
        scratch_shapes=[
            pltpu.VMEM((seq, 2 * LANES), BF16),
            pltpu.VMEM((HEADS_PER_BLOCK, seq // tile, 2 * LANES, tile), BF16),
            pltpu.VMEM((HEADS_PER_BLOCK, seq // kw, HEAD_DIM, kw), BF16),
            pltpu.VMEM((group + 2 * SUBLANES, group), BF16),
            pltpu.VMEM((HEADS_PER_BLOCK, SUBLANES, tile), F32),
            pltpu.VMEM((HEADS_PER_BLOCK, HEAD_DIM, tile), F32),
            pltpu.VMEM((HEADS_PER_BLOCK, kw, tile), BF16),
            pltpu.VMEM((HEADS_PER_BLOCK, kw, tile), BF16),
            pltpu.VMEM((2, HEADS_PER_BLOCK, kw, tile), BF16),
            pltpu.VMEM((2, HEADS_PER_BLOCK, kw, tile), F32),
            pltpu.VMEM((2, HEADS_PER_BLOCK, kw + 2 * SUBLANES, tile), F32),
            pltpu.VMEM((HEADS_PER_BLOCK, 2 * LANES, tile), BF16),
            pltpu.VMEM((2, kw, 2 * LANES), BF16),
            pltpu.VMEM((2, HEADS_PER_BLOCK, HEAD_DIM, kw), BF16),
            pltpu.VMEM((2, ppg, w, slots), F32),
            pltpu.VMEM((2, ppg, w, slots), F32),
            pltpu.SemaphoreType.DMA((2, 2, ppg)),
            pltpu.VMEM((w, ppg * slots), BF16),
            pltpu.VMEM((w, ppg * slots), BF16),
            pltpu.VMEM((rows_d, w), F32),
            pltpu.VMEM((rows_d, LANES), F32),
            pltpu.VMEM((slots, slots + LANES), BF16),
        ],
    )
    return pl.pallas_call(
        functools.partial(_attn_kernel, seq=seq, tile=tile, plan=plan),
        grid_spec=grid_spec,
        out_shape=[jax.ShapeDtypeStruct((m, w), BF16), jax.ShapeDtypeStruct((nb, t_new, w), F32)],
        compiler_params=pltpu.CompilerParams(dimension_semantics=("arbitrary", "arbitrary"),
                                             vmem_limit_bytes=ATTN_VMEM_LIMIT),
        name="attention",
    )(page_table, bias, qt, kt, vt, q_rows, bias_rows, kt_new, vt_new, cache_kt, cache_vt)


def _pool_tile(ext_ref, p_ref, tp, pos0):
    gc = ext_ref.shape[1] // POOL_GROUPS
    pos = pos0 + lax.broadcasted_iota(jnp.int32, (tp, 1), 0)
    for g, win in enumerate(POOL_WINDOWS):
        cols = slice(g * gc, (g + 1) * gc)
        u_new = ext_ref[pl.ds(POOL_HALO, tp), cols]
        win_sum = u_new
        for d in range(1, win):
            win_sum = win_sum + ext_ref[pl.ds(POOL_HALO - d, tp), cols]
        cnt = jnp.minimum(win, pos + 1).astype(F32)
        p_ref[:, cols] = (win_sum / cnt - u_new).astype(p_ref.dtype)


def _pool_prompt_kernel(u_ref, halo_ref, p_ref, ext_ref, *, tp):
    i = pl.program_id(1)
    ext_ref[pl.ds(0, POOL_HALO), :] = jnp.where(i == 0, 0.0, halo_ref[...])
    ext_ref[pl.ds(POOL_HALO, tp), :] = u_ref[...]
    _pool_tile(ext_ref, p_ref, tp, i * tp)


def _pool_prompt(u, batch, seq, tp_cap=512):
    m, w = u.shape
    tp = _row_tile(seq, tp_cap)
    nt = seq // tp
    hb = tp // POOL_HALO
    return pl.pallas_call(
        functools.partial(_pool_prompt_kernel, tp=tp),
        grid=(batch, nt),
        in_specs=[
            pl.BlockSpec((tp, w), lambda b, i: (b * nt + i, 0)),
            pl.BlockSpec((POOL_HALO, w), lambda b, i: (jnp.maximum((b * nt + i) * hb - 1, 0), 0)),
        ],
        out_specs=pl.BlockSpec((tp, w), lambda b, i: (b * nt + i, 0)),
        out_shape=jax.ShapeDtypeStruct((m, w), BF16),
        scratch_shapes=[pltpu.VMEM((POOL_HALO + tp, w), F32)],
        compiler_params=_cparams("parallel", "arbitrary"),
        name="pool_prompt",
    )(u, u)


def _pool_sample_kernel(u_ref, halo_ref, p_ref, ext_ref, *, tp, pos0):
    ext_ref[pl.ds(0, POOL_HALO), :] = halo_ref[...]
    ext_ref[pl.ds(POOL_HALO, tp), :] = u_ref[...]
    _pool_tile(ext_ref, p_ref, tp, pos0)


def _pool_sample(u3, halo3, pos0):
    nb, tp, w = u3.shape
    return pl.pallas_call(
        functools.partial(_pool_sample_kernel, tp=tp, pos0=pos0),
        grid=(nb,),
        in_specs=[
            pl.BlockSpec((None, tp, w), lambda b: (b, 0, 0)),
            pl.BlockSpec((None, POOL_HALO, w), lambda b: (b, 0, 0)),
        ],
        out_specs=pl.BlockSpec((None, tp, w), lambda b: (b, 0, 0)),
        out_shape=jax.ShapeDtypeStruct((nb, tp, w), F32),
        scratch_shapes=[pltpu.VMEM((POOL_HALO + tp, w), F32)],
        compiler_params=_cparams("parallel"),
        name="pool_sample",
    )(u3, halo3)


def _inproj(xns, w_in, col0, ncols, *, mode, out_dtype, tn, head_gain=None, seq=0):
    d = w_in.shape[0]
    jb = col0 // tn
    assert col0 % tn == 0
    consts, epilogue = (), {"plain": _epi_plain, "sigmoid": _epi_sigmoid, "headnorm": _epi_headnorm}[mode]
    if mode == "headnorm":
        hid = jnp.arange(HEADNORM_GROUP, dtype=jnp.int32) // HEAD_DIM
        bd = (hid[:, None] == hid[None, :]).astype(BF16)
        gain = jnp.tile(head_gain.astype(F32), tn // HEAD_DIM).reshape(1, tn)
        consts = ((bd, bd.shape, lambda j: (0, 0)), (gain, (1, tn), lambda j: (0, 0)))
    row_sets = [_Rows(lhs=((xn, d, lambda j: 0),), head_major_seq=seq if s == 0 else 0)
                for s, xn in enumerate(xns)]
    return _ws_matmul("inproj_" + mode + ("_t" if seq else ""), ((w_in, d, lambda j: (0, j + jb)),), (0,),
                      consts, row_sets, epilogue, ncols, tn, out_dtype, tm_cap=1024)


def _project(xns, wts, seq):
    w_in = wts["w_in"]
    d = w_in.shape[0]
    aw, pw = ATTN_WIDTH, d // 2
    tn = INPROJ_COLS
    q = _inproj(xns, w_in, 0, aw, mode="headnorm", out_dtype=BF16, tn=tn,
                head_gain=wts["q_norm_g"] * (HEAD_DIM ** -0.5 * LOG2E), seq=seq)
    k = _inproj(xns, w_in, aw, aw, mode="headnorm", out_dtype=F32, tn=tn, head_gain=wts["k_norm_g"], seq=seq)
    v = _inproj(xns, w_in, 2 * aw, aw, mode="plain", out_dtype=F32, tn=tn, seq=seq)
    u = _inproj(xns, w_in, 3 * aw, pw, mode="plain", out_dtype=F32, tn=tn)
    gates = _inproj(xns, w_in, 3 * aw + pw, 2 * d, mode="sigmoid", out_dtype=BF16, tn=tn)
    return q, k, v, u, gates


def _mix_kernel(o_ref, p_ref, ga_ref, gb_ref, x_ref, wap_ref, wp_ref, ps_ref, wo_ref, g2_ref, h_ref, hn_ref):
    a = jnp.dot(o_ref[...], wap_ref[...], preferred_element_type=F32)
    groups, gc, _ = wp_ref.shape
    op = jnp.concatenate(
        [jnp.dot(p_ref[:, g * gc:(g + 1) * gc], wp_ref[g], preferred_element_type=F32) for g in range(groups)],
        axis=1)
    merged = ga_ref[...].astype(F32) * a + gb_ref[...].astype(F32) * (op * ps_ref[...])
    h = x_ref[...] + jnp.dot(merged.astype(BF16), wo_ref[...], preferred_element_type=F32)
    h_ref[...] = h
    ms = jnp.mean(h * h, axis=-1, keepdims=True)
    hn_ref[...] = ((h * lax.rsqrt(ms + RMS_EPS)) * g2_ref[...]).astype(hn_ref.dtype)


def _mix(x, o_attn, p, gates, w_ap, w_pool, pool_scale, w_out, norm2_g, tm_cap=256):
    m, d = x.shape
    aw, pw = o_attn.shape[1], p.shape[1]
    tm = _row_tile(m, tm_cap)
    whole = lambda a: pl.BlockSpec(a.shape, lambda i: (0,) * a.ndim)
    ps2, g2 = pool_scale.reshape(1, d), norm2_g.reshape(1, d)
    return pl.pallas_call(
        _mix_kernel,
        grid=(m // tm,),
        in_specs=[
            pl.BlockSpec((tm, aw), lambda i: (i, 0)),
            pl.BlockSpec((tm, pw), lambda i: (i, 0)),
            pl.BlockSpec((tm, d), lambda i: (i, 0)),
            pl.BlockSpec((tm, d), lambda i: (i, 1)),
            pl.BlockSpec((tm, d), lambda i: (i, 0)),
            whole(w_ap), whole(w_pool), whole(ps2), whole(w_out), whole(g2),
        ],
        out_specs=[pl.BlockSpec((tm, d), lambda i: (i, 0)), pl.BlockSpec((tm, d), lambda i: (i, 0))],
        out_shape=[jax.ShapeDtypeStruct((m, d), F32), jax.ShapeDtypeStruct((m, d), BF16)],
        compiler_params=_cparams("parallel"),
        name="mix",
    )(o_attn, p, gates, gates, x, w_ap, w_pool, ps2, w_out, g2)


def _mix_and_ffn(xs, o_attns, ps, gates, wts):
    w_gu, w_down = wts["w_gate_up"], wts["w_down"]
    dff, d = w_down.shape
    w_ap, w_pool, w_out = (wts[n].astype(BF16) for n in ("w_attn_proj", "w_pool", "w_out"))
    hs, hns = zip(*[_mix(x, o, p, gt, w_ap, w_pool, wts["pool_scale"], w_out, wts["norm2_g"])
                    for x, o, p, gt in zip(xs, o_attns, ps, gates)])
    nj = dff // 512
    acts = _ws_matmul(
        "gate_up", ((w_gu, d, lambda j: (0, j)), (w_gu, d, lambda j: (0, j + nj))), (0, 0), (),
        [_Rows(lhs=((hn, d, lambda j: 0),)) for hn in hns],
        _epi_swiglu, dff, 512, BF16, tm_cap=1024)
    return _ws_matmul(
        "down_proj", ((w_down, dff, lambda j: (0, j)),), (0,), (),
        [_Rows(lhs=((act, dff, lambda j: 0),), extras=((h, lambda j: j),)) for act, h in zip(acts, hs)],
        _epi_residual, d, 512, F32, tm_cap=512)


def kernel(x_prompt, x_sample, cache_k, cache_v, state_pool, page_table, norm1_g, w_in,
           q_norm_g, k_norm_g, sb_bias, w_attn_proj, w_pool, pool_scale, w_out, norm2_g, w_gate_up, w_down):
    b_p, seq, d = x_prompt.shape
    b_s, t_s = x_sample.shape[:2]
    depth, n_pool, page = cache_k.shape[:3]
    n_pages = page_table.shape[1]
    past_len = n_pages * page
    pw = d // 2

    xp = x_prompt.reshape(b_p * seq, d)
    xs = x_sample.reshape(b_s * t_s, d)
    outs = {name: [] for name in ("kp", "vp", "up", "ks", "vs", "us")}
    for l in range(depth):
        wts = {
            "w_in": w_in[l], "q_norm_g": q_norm_g[l], "k_norm_g": k_norm_g[l], "w_attn_proj": w_attn_proj[l],
            "w_pool": w_pool[l], "pool_scale": pool_scale[l], "w_out": w_out[l], "norm2_g": norm2_g[l],
            "w_gate_up": w_gate_up[l], "w_down": w_down[l],
        }
        bias = sb_bias[l].astype(F32) * LOG2E

        xns = [_rmsnorm(xp, norm1_g[l]), _rmsnorm(xs, norm1_g[l])]
        (qt_p, q_s), (kt_p, k_s), (vt_p, v_s), (u_p, u_s), (gates_p, gates_s) = _project(xns, wts, seq)

        q4 = q_s.reshape(b_s, t_s, 1, N_HEADS, HEAD_DIM)
        eye = jnp.eye(N_HEADS, dtype=BF16).reshape(1, 1, N_HEADS, N_HEADS, 1)
        q_rows = (q4 * eye).reshape(b_s, t_s * N_HEADS, ATTN_WIDTH)
        pad = ((0, 0), (0, 0), (0, page - t_s))
        kt_new = jnp.pad(k_s.reshape(b_s, t_s, ATTN_WIDTH).transpose(0, 2, 1), pad)
        vt_new = jnp.pad(v_s.reshape(b_s, t_s, ATTN_WIDTH).transpose(0, 2, 1), pad)
        cache_kt = cache_k[l].transpose(0, 2, 3, 1).reshape(n_pool, ATTN_WIDTH, page)
        cache_vt = cache_v[l].transpose(0, 2, 3, 1).reshape(n_pool, ATTN_WIDTH, page)
        o_attn_p, o_attn_s = _attention(qt_p, kt_p, vt_p, bias, q_rows, jnp.tile(bias, t_s), kt_new, vt_new,
                                        cache_kt, cache_vt, page_table, t_s)
        o_attn_s = o_attn_s.reshape(b_s * t_s, ATTN_WIDTH).astype(BF16)
        outs["kp"].append(kt_p.transpose(0, 3, 1, 2))
        outs["vp"].append(vt_p.transpose(0, 3, 1, 2))

        p_p = _pool_prompt(u_p, b_p, seq)
        outs["up"].append(u_p.reshape(b_p, seq, pw)[:, seq - POOL_STATE:])
        u3 = u_s.reshape(b_s, t_s, pw)
        halo = jnp.pad(state_pool[l], ((0, 0), (POOL_HALO - POOL_STATE, 0), (0, 0)))
        p_s = _pool_sample(u3, halo, past_len).reshape(b_s * t_s, pw).astype(BF16)
        outs["ks"].append(k_s.reshape(b_s, t_s, N_HEADS, HEAD_DIM))
        outs["vs"].append(v_s.reshape(b_s, t_s, N_HEADS, HEAD_DIM))
        outs["us"].append(jnp.concatenate([state_pool[l], u3], axis=1)[:, -POOL_STATE:])

        xp, xs = _mix_and_ffn([xp, xs], [o_attn_p, o_attn_s], [p_p, p_s], [gates_p, gates_s], wts)

    st = lambda name: jnp.stack(outs[name], axis=0)
    return (xp.reshape(b_p, seq, d), xs.reshape(b_s, t_s, d),
            st("kp"), st("vp"), st("up"), st("ks"), st("vs"), st("us"))
```

```python
import functools
from typing import NamedTuple

import jax
import jax.numpy as jnp
from jax import lax
from jax.experimental import pallas as pl
from jax.experimental.pallas import tpu as pltpu

F32 = jnp.float32
BF16 = jnp.bfloat16

N_HEADS = 16
HEAD_DIM = 64
ATTN_WIDTH = N_HEADS * HEAD_DIM
POOL_WINDOWS = (2, 4, 8, 16)
POOL_GROUPS = len(POOL_WINDOWS)
POOL_STATE = max(POOL_WINDOWS) - 1
POOL_HALO = 16
RMS_EPS = 1e-6

LANES = 128
SUBLANES = 8
HEADS_PER_BLOCK = LANES // HEAD_DIM
VMEM_LIMIT = 52 * 1024 * 1024
LARGE_VMEM_LIMIT = 57 * 1024 * 1024
ATTN_TILE = 512
CUMSUM_GROUP = 256
BIAS_TERMS = 3
BIAS_ROWS = 16
WEIGHTS_CHUNK = 32
DECODE_PAGES_PER_STEP = 8
INPROJ_COLS = 1024
HEADNORM_GROUP = 256
LOG2E = 1.4426950408889634

NT_DIMS = (((1,), (1,)), ((), ()))


def _cparams(*sem):
    return pltpu.CompilerParams(dimension_semantics=sem, vmem_limit_bytes=VMEM_LIMIT)


def _row_tile(m, cap):
    t = min(m, cap)
    while m % t:
        t //= 2
    return t


class _Rows(NamedTuple):
    lhs: tuple
    extras: tuple = ()
    head_major_seq: int = 0


def _ws_kernel(*refs, lhs_of_w, n_const, sets, epilogue, normalize):
    n_w = len(lhs_of_w)
    w_refs, refs = refs[:n_w], refs[n_w:]
    const_refs, refs = refs[:n_const], refs[n_const:]
    if normalize:
        gain_ref, refs = refs[0], refs[1:]
    set_refs = []
    for n_lhs, n_extra, _ in sets:
        set_refs.append((refs[:n_lhs], refs[n_lhs:n_lhs + n_extra]))
        refs = refs[n_lhs + n_extra:]
    out_refs, refs = refs[:len(sets)], refs[len(sets):]
    xn_refs, wb_refs = (refs[:len(sets)], refs[len(sets):]) if normalize else ([None] * len(sets), refs)

    def compute(lhs_refs, extra_refs, o_ref, head_major, xn_ref):
        lhs = [r[...] for r in lhs_refs]
        if normalize:
            x = lhs[0]
            ms = jnp.mean(x * x, axis=-1, keepdims=True)
            lhs = [((x * lax.rsqrt(ms + RMS_EPS)) * gain_ref[...]).astype(BF16)]
            xn_ref[...] = lhs[0]
        accs = [jnp.dot(lhs[lhs_of_w[k]], wb_refs[k][...], preferred_element_type=F32) for k in range(n_w)]
        r = epilogue(accs, [e[...] for e in extra_refs], [c[...] for c in const_refs])
        if head_major:
            r = r.T.reshape(o_ref.shape)
        o_ref[...] = r.astype(o_ref.dtype)

    @pl.when(pl.program_id(1) == 0)
    def _():
        for w_ref, wb_ref in zip(w_refs, wb_refs):
            wb_ref[...] = w_ref[...].astype(BF16)
        for s in range(1, len(sets)):
            compute(*set_refs[s], out_refs[s], sets[s][2], xn_refs[s])

    compute(*set_refs[0], out_refs[0], sets[0][2], xn_refs[0])


def _ws_matmul(name, weights, lhs_of_w, consts, row_sets, epilogue, n_cols, tn, out_dtype, tm_cap,
               norm_gain=None):
    assert n_cols % tn == 0
    m0 = row_sets[0].lhs[0][0].shape[0]
    tm0 = _row_tile(row_sets[0].head_major_seq or m0, tm_cap)
    in_specs, args = [], []
    for w, kb, idx in weights:
        in_specs.append(pl.BlockSpec((kb, tn), lambda j, i, idx=idx: idx(j)))
        args.append(w)
    for c, shape, idx in consts:
        in_specs.append(pl.BlockSpec(shape, lambda j, i, idx=idx: idx(j)))
        args.append(c)
    if norm_gain is not None:
        assert n_cols == tn and all(len(rows.lhs) == 1 for rows in row_sets)
        in_specs.append(pl.BlockSpec((1, norm_gain.shape[0]), lambda j, i: (0, 0)))
        args.append(norm_gain.reshape(1, -1))
    out_specs, out_shapes, sets, xn_specs, xn_shapes = [], [], [], [], []
    for s, rows in enumerate(row_sets):
        m = rows.lhs[0][0].shape[0]
        tm = tm0 if s == 0 else m
        row = (lambda i: i) if s == 0 else (lambda i: 0)
        for arr, kb, kidx in rows.lhs:
            in_specs.append(pl.BlockSpec((tm, kb), lambda j, i, row=row, kidx=kidx: (row(i), kidx(j))))
            args.append(arr)
        for arr, cidx in rows.extras:
            in_specs.append(pl.BlockSpec((tm, tn), lambda j, i, row=row, cidx=cidx: (row(i), cidx(j))))
            args.append(arr)
        if rows.head_major_seq:
            assert s == 0
            seq = rows.head_major_seq
            nt, hpt = seq // tm, tn // HEAD_DIM
            out_specs.append(pl.BlockSpec((None, hpt, HEAD_DIM, tm), lambda j, i, nt=nt: (i // nt, j, 0, i % nt)))
            out_shapes.append(jax.ShapeDtypeStruct((m // seq, n_cols // HEAD_DIM, HEAD_DIM, seq), out_dtype))
        else:
            out_specs.append(pl.BlockSpec((tm, tn), lambda j, i, row=row: (row(i), j)))
            out_shapes.append(jax.ShapeDtypeStruct((m, n_cols), out_dtype))
        sets.append((len(rows.lhs), len(rows.extras), bool(rows.head_major_seq)))
        if norm_gain is not None:
            kd = rows.lhs[0][0].shape[1]
            xn_specs.append(pl.BlockSpec((tm, kd), lambda j, i, row=row: (row(i), 0)))
            xn_shapes.append(jax.ShapeDtypeStruct((m, kd), BF16))
    outs = pl.pallas_call(
        functools.partial(_ws_kernel, lhs_of_w=tuple(lhs_of_w), n_const=len(consts), sets=tuple(sets),
                          epilogue=epilogue, normalize=norm_gain is not None),
        grid=(n_cols // tn, m0 // tm0),
        in_specs=in_specs,
        out_specs=out_specs + xn_specs,
        out_shape=out_shapes + xn_shapes,
        scratch_shapes=[pltpu.VMEM((kb, tn), BF16) for _, kb, _ in weights],
        compiler_params=_cparams("parallel", "arbitrary"),
        name=name,
    )(*args)
    return outs if norm_gain is None else (outs[:len(row_sets)], outs[len(row_sets):])


def _epi_plain(accs, extras, consts):
    return accs[0]


def _epi_headnorm(accs, extras, consts):
    h, (bd, gain) = accs[0], consts
    w = bd.shape[0]
    hh = (h * h).astype(BF16)
    ss = [jnp.dot(hh[:, c:c + w], bd, preferred_element_type=F32) for c in range(0, h.shape[1], w)]
    ss = ss[0] if len(ss) == 1 else jnp.concatenate(ss, axis=1)
    return (h * lax.rsqrt(ss * (1.0 / HEAD_DIM) + RMS_EPS)) * gain


def _epi_sigmoid(accs, extras, consts):
    return 1.0 / (1.0 + jnp.exp(-accs[0]))


def _epi_residual(accs, extras, consts):
    return extras[0] + accs[0]


def _epi_swiglu(accs, extras, consts):
    gate, up = accs
    return (gate / (1.0 + jnp.exp(-gate))) * up


def _cumsum_rhs(g):
    r = lax.broadcasted_iota(jnp.int32, (g, g + LANES), 0)
    c = lax.broadcasted_iota(jnp.int32, (g, g + LANES), 1)
    return jnp.where((r > c) | (c >= g), 1.0, 0.0).astype(BF16)


def _sb_weights(z, carry, u, mask):
    g = u.shape[0]
    neg_abs = lax.bitcast_convert_type(lax.bitcast_convert_type(z, jnp.uint32) | jnp.uint32(1 << 31), F32)
    sp = jnp.maximum(z, 0.0) + jnp.log(1.0 + jnp.exp2(neg_abs)) * LOG2E
    if mask is not None:
        sp = jnp.where(mask, sp, 0.0)
    log_beta = z - sp
    sp16 = sp.astype(BF16)
    parts = []
    for c in reversed(range(z.shape[1] // g)):
        cols = slice(c * g, (c + 1) * g)
        cum = jnp.dot(sp16[:, cols], u, preferred_element_type=F32)
        for l in reversed(range(0, g, LANES)):
            parts.append(jnp.exp2(log_beta[:, c * g + l:c * g + l + LANES] - cum[:, l:l + LANES] - carry))
        carry = carry + cum[:, g:]
    a = parts[0] if len(parts) == 1 else jnp.concatenate(parts[::-1], axis=1)
    if mask is not None:
        a = jnp.where(mask, a, 0.0)
    return carry, a.astype(BF16)


def _cumsum_lhs(g):
    r = lax.broadcasted_iota(jnp.int32, (g + 2 * SUBLANES, g), 0)
    c = lax.broadcasted_iota(jnp.int32, (g + 2 * SUBLANES, g), 1)
    return jnp.where((c > r) | (r >= g), 1.0, 0.0).astype(BF16)


def _softplus_t(z_ref, sp_ref, lb_ref, mask):
    ch = WEIGHTS_CHUNK
    for r0 in range(0, z_ref.shape[0], ch):
        rows = pl.ds(r0, ch)
        zt = z_ref[rows, :]
        neg_abs = lax.bitcast_convert_type(lax.bitcast_convert_type(zt, jnp.uint32) | jnp.uint32(1 << 31), F32)
        sp = jnp.maximum(zt, 0.0) + jnp.log(1.0 + jnp.exp2(neg_abs)) * LOG2E
        if mask is not None:
            sp = jnp.where(mask[r0:r0 + ch], sp, 0.0)
        sp_ref[rows, :] = sp.astype(BF16)
        lb_ref[rows, :] = zt - sp


def _stick_weights_t(lb_ref, cum_ref, carry, a_ref, mask):
    g, nq = lb_ref.shape
    ch = WEIGHTS_CHUNK
    for r0 in range(0, g, ch):
        rows = pl.ds(r0, ch)
        x = (lb_ref[rows, :] - cum_ref[rows, :]).reshape(ch // SUBLANES, SUBLANES, nq) - carry[None]
        a = jnp.exp2(x).reshape(ch, nq)
        if mask is not None:
            a = jnp.where(mask[r0:r0 + ch], a, 0.0)
        a_ref[rows, :] = a.astype(BF16)
    return carry + cum_ref[pl.ds(g, SUBLANES), :]


def _sb_block(z, carry, u, mask, vt):
    carry, a = _sb_weights(z, carry, u, mask)
    return carry, lax.dot_general(a, vt, NT_DIMS, preferred_element_type=F32)


class _DecodePlan(NamedTuple):
    n_batch: int
    t_new: int
    n_pages: int
    pages_per_group: int
    n_groups: int
    steps_per_batch: int
    slots_per_step: int
    groups_per_slot: int
    every_slot_full: bool


def _attn_kernel(pt_ref, bias_ref, qt_ref, kt_ref, vt_ref, qd_ref, biasd_ref, kn_ref, vn_ref, ck_hbm, cv_hbm,
                 o_ref, od_ref, ka_ref, qz_ref, vz_ref, lcum_ref, carry_ref, acc_ref, za_ref, zb_ref, aa_ref, ab_ref,
                 sp_ref, lb_ref, cum_ref, qs_ref, ks_ref, vs_ref, pk_ref, pv_ref, sem, kb_ref, vb_ref, dacc_ref, dcarry_ref, du_ref, *, seq, tile, plan):
    pair = pl.program_id(1)
    step = pl.program_id(0) * pl.num_programs(1) + pair
    kw = tile // 2
    nkb = seq // kw
    lcum_ref[...] = _cumsum_lhs(lcum_ref.shape[1])
    ka_ref[:, pl.ds(LANES, LANES)] = jnp.ones((seq, LANES), BF16)
    for cb in range(nkb):
        cols = pl.ds(cb * kw, kw)
        ka_ref[cols, pl.ds(0, LANES)] = kt_ref[:, :, cols].reshape(LANES, kw).T.astype(BF16)
    qz_ref[...] = jnp.zeros(qz_ref.shape, BF16)
    term_row = lax.broadcasted_iota(jnp.int32, (BIAS_ROWS, tile), 0)
    for h in range(HEADS_PER_BLOCK):
        rows = pl.ds(h * HEAD_DIM, HEAD_DIM)
        rest = jnp.full((BIAS_ROWS, tile), bias_ref[pair * HEADS_PER_BLOCK + h], F32)
        terms = jnp.zeros((BIAS_ROWS, tile), F32)
        for i in range(BIAS_TERMS):
            term = rest.astype(BF16).astype(F32)
            terms = jnp.where(term_row == i, term, terms)
            rest = rest - term
        for qb in range(seq // tile):
            cols = pl.ds(qb * tile, tile)
            qz_ref[h, qb, rows, :] = qt_ref[h, :, cols]
            qz_ref[h, qb, pl.ds(LANES, BIAS_ROWS), :] = terms.astype(BF16)
        for cb in range(nkb):
            vz_ref[h, cb] = vt_ref[h, :, pl.ds(cb * kw, kw)].astype(BF16)

    rk = lax.broadcasted_iota(jnp.int32, (tile, tile), 0)
    cq = lax.broadcasted_iota(jnp.int32, (tile, tile), 1)
    diag_mask = rk < cq

    heads = range(HEADS_PER_BLOCK)

    rows_d, slots = qd_ref.shape[0], kn_ref.shape[1]
    n_all_groups = plan.n_batch * plan.n_groups
    step_in_batch = step % plan.steps_per_batch
    batch_d = step // plan.steps_per_batch

    def group_copies(grp):
        bd, n, buf = grp // plan.n_groups, grp % plan.n_groups, grp % 2
        copies = []
        for r in range(plan.pages_per_group):
            page = pt_ref[bd, plan.n_pages - 1 - n * plan.pages_per_group - r]
            copies.append(pltpu.make_async_copy(ck_hbm.at[page], pk_ref.at[buf, r], sem.at[buf, 0, r]))
            copies.append(pltpu.make_async_copy(cv_hbm.at[page], pv_ref.at[buf, r], sem.at[buf, 1, r]))
        return copies

    def start_group(grp):
        for c in group_copies(grp):
            c.start()

    def decode_group(grp):
        buf = grp % 2
        for c in group_copies(grp):
            c.wait()

        @pl.when(grp + 1 < n_all_groups)
        def _():
            start_group(grp + 1)

        for r in range(plan.pages_per_group):
            cols = pl.ds((plan.pages_per_group - 1 - r) * slots, slots)
            kb_ref[:, cols] = pk_ref[buf, r].astype(BF16)
            vb_ref[:, cols] = pv_ref[buf, r].astype(BF16)
        s = jnp.dot(qd_ref[...], kb_ref[...], preferred_element_type=F32)
        dcarry_ref[...], pv = _sb_block(s + biasd_ref[...], dcarry_ref[...], du_ref[...], None, vb_ref[...])
        dacc_ref[...] += pv

    def decode_new_tokens():
        du_ref[...] = _cumsum_rhs(slots)
        r = lax.broadcasted_iota(jnp.int32, (rows_d, slots), 0)
        c = lax.broadcasted_iota(jnp.int32, (rows_d, slots), 1)
        s = jnp.dot(qd_ref[...], kn_ref[...].astype(BF16), preferred_element_type=F32)
        dcarry_ref[...], dacc_ref[...] = _sb_block(
            s + biasd_ref[:, pl.ds(0, slots)], jnp.zeros((rows_d, LANES), F32), du_ref[...], c < (r // N_HEADS),
            vn_ref[...].astype(BF16))

    def decode_finish():
        r = lax.broadcasted_iota(jnp.int32, dacc_ref.shape, 0)
        l = lax.broadcasted_iota(jnp.int32, dacc_ref.shape, 1)
        own = jnp.where((r % N_HEADS) == (l // HEAD_DIM), dacc_ref[...], 0.0)
        for t in range(plan.t_new):
            od_ref[pl.ds(t, 1), :] = jnp.sum(own[t * N_HEADS:(t + 1) * N_HEADS], axis=0, keepdims=True)

    def decode_slot(qb):
        slot = step_in_batch * plan.slots_per_step + qb

        @pl.when((step == 0) & (qb == 0))
        def _():
            start_group(0)

        pl.when(slot == 0)(decode_new_tokens)
        for t in range(plan.groups_per_slot):
            n = slot * plan.groups_per_slot + t
            grp = batch_d * plan.n_groups + n
            if plan.every_slot_full:
                decode_group(grp)
            else:
                pl.when(n < plan.n_groups)(functools.partial(decode_group, grp))
        if plan.every_slot_full:
            decode_finish()
        else:
            pl.when(slot == plan.steps_per_batch * plan.slots_per_step - 1)(decode_finish)

    def qbody(qb, _):
        decode_slot(qb)
        qoff = pl.multiple_of(qb * tile, tile)
        k0 = 2 * qb

        for h in heads:
            qs_ref[h] = qz_ref[h, qb]

        def stage_keys(slot, kb):
            ks_ref[slot] = ka_ref[pl.ds(pl.multiple_of(kb * kw, kw), kw), :]

        def stage_values(slot, kb):
            for h in heads:
                vs_ref[slot, h] = vz_ref[h, kb]

        def logits(slot, z_ref):
            for h in heads:
                z_ref[h] = jnp.dot(ks_ref[slot], qs_ref[h], preferred_element_type=F32)

        def softplus(z_ref, half, mask=None):
            for h in heads:
                _softplus_t(z_ref.at[h], sp_ref.at[half, h], lb_ref.at[half, h], mask)

        def cumsums(half):
            for h in heads:
                cum_ref[half, h] = jnp.dot(lcum_ref[...], sp_ref[half, h], preferred_element_type=F32)

        def stick(half, a_ref, mask=None, first=False):
            for h in heads:
                carry = jnp.zeros((SUBLANES, tile), F32) if first else carry_ref[h]
                carry_ref[h] = _stick_weights_t(lb_ref.at[half, h], cum_ref.at[half, h], carry, a_ref.at[h], mask)

        def values(a_ref, slot, first=False):
            for h in heads:
                pv = jnp.dot(vs_ref[slot, h], a_ref[h], preferred_element_type=F32)
                acc_ref[h] = pv if first else acc_ref[h] + pv


        def weights_and_next_logits(mask_a=None, mask_b=None, first=False):
            softplus(za_ref, 0, mask_a)
            cumsums(0)
            logits(0, za_ref)
            softplus(zb_ref, 1, mask_b)
            logits(1, zb_ref)
            cumsums(1)
            stick(0, aa_ref, mask_a, first)
            stick(1, ab_ref, mask_b)

        acc_ref[...] = jnp.zeros(acc_ref.shape, F32)
        stage_keys(2, k0 + 1)
        stage_keys(3, k0)
        stage_keys(0, jnp.maximum(k0 - 1, 0))
        stage_keys(1, jnp.maximum(k0 - 2, 0))
        logits(2, za_ref)
        logits(3, zb_ref)
        weights_and_next_logits(diag_mask[kw:], diag_mask[:kw], first=True)

        def kbody(i, _):
            ka = k0 - 1 - 2 * i
            stage_keys(0, jnp.maximum(ka - 2, 0))
            stage_keys(1, jnp.maximum(ka - 3, 0))
            stage_values(0, ka + 2)
            stage_values(1, ka + 1)
            values(aa_ref, 0)
            values(ab_ref, 1)
            weights_and_next_logits()
            return 0

        lax.fori_loop(0, qb, kbody, 0)
        stage_values(0, 1)
        stage_values(1, 0)
        values(aa_ref, 0)
        values(ab_ref, 1)
        out = jnp.concatenate([acc_ref[h] for h in heads], axis=0)
        o_ref[pl.ds(qoff, tile), :] = out.T.astype(o_ref.dtype)
        return 0

    lax.fori_loop(0, seq // tile, qbody, 0)


def _attention(qt, kt, vt, bias, q_rows, row_bias, kt_new, vt_new, cache_kt, cache_vt, page_table, t_new):
    batch, n_heads, _, seq = qt.shape
    m, w = batch * seq, n_heads * HEAD_DIM
    npairs = w // LANES
    tile = _row_tile(seq, ATTN_TILE)
    kw = tile // 2
    assert kw <= CUMSUM_GROUP
    group = kw
    nb, rows_d, _ = q_rows.shape
    n_pages, slots = page_table.shape[1], cache_kt.shape[2]
    ppg = max(p for p in (DECODE_PAGES_PER_STEP, 4, 2, 1) if n_pages % p == 0)
    n_steps = batch * npairs
    assert n_steps % nb == 0, (n_steps, nb)
    n_slots = (n_steps // nb) * (seq // tile)
    n_groups = n_pages // ppg
    gps = -(-n_groups // n_slots)
    plan = _DecodePlan(n_batch=nb, t_new=t_new, n_pages=n_pages, pages_per_group=ppg, n_groups=n_groups,
                       steps_per_batch=n_steps // nb, slots_per_step=seq // tile, groups_per_slot=gps,
                       every_slot_full=gps * n_slots == n_groups)
    bias_rows = jnp.broadcast_to(row_bias[:, None], (rows_d, ppg * slots))
    spb = plan.steps_per_batch
    kv_spec = pl.BlockSpec((None, HEADS_PER_BLOCK, HEAD_DIM, seq), lambda b, p, pt: (b, p, 0, 0))
    per_d = lambda b, p, pt: ((b * npairs + p) // spb, 0, 0)
    grid_spec = pltpu.PrefetchScalarGridSpec(
        num_scalar_prefetch=1,
        grid=(batch, npairs),
        in_specs=[
            pl.BlockSpec(memory_space=pltpu.SMEM),
            kv_spec,
            kv_spec,
            kv_spec,
            pl.BlockSpec((None, rows_d, w), per_d),
            pl.BlockSpec((rows_d, ppg * slots), lambda b, p, pt: (0, 0)),
            pl.BlockSpec((None, w, slots), per_d),
            pl.BlockSpec((None, w, slots), per_d),
            pl.BlockSpec(memory_space=pl.ANY),
            pl.BlockSpec(memory_space=pl.ANY),
        ],
        out_specs=[pl.BlockSpec((seq, LANES), lambda b, p, pt: (b, p)),
                   pl.BlockSpec((None, t_new, w), per_d)],
        scratch_shapes=[
            pltpu.VMEM((seq, 2 * LANES), BF16),
            pltpu.VMEM((HEADS_PER_BLOCK, seq // tile, 2 * LANES, tile), BF16),
            pltpu.VMEM((HEADS_PER_BLOCK, seq // kw, HEAD_DIM, kw), BF16),
            pltpu.VMEM((group + 2 * SUBLANES, group), BF16),
            pltpu.VMEM((HEADS_PER_BLOCK, SUBLANES, tile), F32),
            pltpu.VMEM((HEADS_PER_BLOCK, HEAD_DIM, tile), F32),
            pltpu.VMEM((HEADS_PER_BLOCK, kw, tile), F32),
            pltpu.VMEM((HEADS_PER_BLOCK, kw, tile), F32),
            pltpu.VMEM((HEADS_PER_BLOCK, kw, tile), BF16),
            pltpu.VMEM((HEADS_PER_BLOCK, kw, tile), BF16),
            pltpu.VMEM((2, HEADS_PER_BLOCK, kw, tile), BF16),
            pltpu.VMEM((2, HEADS_PER_BLOCK, kw, tile), F32),
            pltpu.VMEM((2, HEADS_PER_BLOCK, kw + 2 * SUBLANES, tile), F32),
            pltpu.VMEM((HEADS_PER_BLOCK, 2 * LANES, tile), BF16),
            pltpu.VMEM((4, kw, 2 * LANES), BF16),
            pltpu.VMEM((2, HEADS_PER_BLOCK, HEAD_DIM, kw), BF16),
            pltpu.VMEM((2, ppg, w, slots), F32),
            pltpu.VMEM((2, ppg, w, slots), F32),
            pltpu.SemaphoreType.DMA((2, 2, ppg)),
            pltpu.VMEM((w, ppg * slots), BF16),
            pltpu.VMEM((w, ppg * slots), BF16),
            pltpu.VMEM((rows_d, w), F32),
            pltpu.VMEM((rows_d, LANES), F32),
            pltpu.VMEM((slots, slots + LANES), BF16),
        ],
    )
    return pl.pallas_call(
        functools.partial(_attn_kernel, seq=seq, tile=tile, plan=plan),
        grid_spec=grid_spec,
        out_shape=[jax.ShapeDtypeStruct((m, w), BF16), jax.ShapeDtypeStruct((nb, t_new, w), F32)],
        compiler_params=pltpu.CompilerParams(dimension_semantics=("arbitrary", "arbitrary"),
                                             vmem_limit_bytes=LARGE_VMEM_LIMIT),
        name="attention",
    )(page_table, bias, qt, kt, vt, q_rows, bias_rows, kt_new, vt_new, cache_kt, cache_vt)


def _pool_tile(ext_ref, p_ref, tp, pos0):
    gc = ext_ref.shape[1] // POOL_GROUPS
    pos = pos0 + lax.broadcasted_iota(jnp.int32, (tp, 1), 0)
    for g, win in enumerate(POOL_WINDOWS):
        cols = slice(g * gc, (g + 1) * gc)
        u_new = ext_ref[pl.ds(POOL_HALO, tp), cols]
        win_sum = u_new
        for d in range(1, win):
            win_sum = win_sum + ext_ref[pl.ds(POOL_HALO - d, tp), cols]
        cnt = jnp.minimum(win, pos + 1).astype(F32)
        p_ref[:, cols] = (win_sum / cnt - u_new).astype(p_ref.dtype)


def _pool_prompt_kernel(u_ref, halo_ref, p_ref, ext_ref, *, tp):
    i = pl.program_id(1)
    ext_ref[pl.ds(0, POOL_HALO), :] = jnp.where(i == 0, 0.0, halo_ref[...])
    ext_ref[pl.ds(POOL_HALO, tp), :] = u_ref[...]
    _pool_tile(ext_ref, p_ref, tp, i * tp)


def _pool_prompt(u, batch, seq, tp_cap=512):
    m, w = u.shape
    tp = _row_tile(seq, tp_cap)
    nt = seq // tp
    hb = tp // POOL_HALO
    return pl.pallas_call(
        functools.partial(_pool_prompt_kernel, tp=tp),
        grid=(batch, nt),
        in_specs=[
            pl.BlockSpec((tp, w), lambda b, i: (b * nt + i, 0)),
            pl.BlockSpec((POOL_HALO, w), lambda b, i: (jnp.maximum((b * nt + i) * hb - 1, 0), 0)),
        ],
        out_specs=pl.BlockSpec((tp, w), lambda b, i: (b * nt + i, 0)),
        out_shape=jax.ShapeDtypeStruct((m, w), BF16),
        scratch_shapes=[pltpu.VMEM((POOL_HALO + tp, w), F32)],
        compiler_params=_cparams("parallel", "arbitrary"),
        name="pool_prompt",
    )(u, u)


def _pool_sample_kernel(u_ref, halo_ref, p_ref, ext_ref, *, tp, pos0):
    ext_ref[pl.ds(0, POOL_HALO), :] = halo_ref[...]
    ext_ref[pl.ds(POOL_HALO, tp), :] = u_ref[...]
    _pool_tile(ext_ref, p_ref, tp, pos0)


def _pool_sample(u3, halo3, pos0):
    nb, tp, w = u3.shape
    return pl.pallas_call(
        functools.partial(_pool_sample_kernel, tp=tp, pos0=pos0),
        grid=(nb,),
        in_specs=[
            pl.BlockSpec((None, tp, w), lambda b: (b, 0, 0)),
            pl.BlockSpec((None, POOL_HALO, w), lambda b: (b, 0, 0)),
        ],
        out_specs=pl.BlockSpec((None, tp, w), lambda b: (b, 0, 0)),
        out_shape=jax.ShapeDtypeStruct((nb, tp, w), F32),
        scratch_shapes=[pltpu.VMEM((POOL_HALO + tp, w), F32)],
        compiler_params=_cparams("parallel"),
        name="pool_sample",
    )(u3, halo3)


def _inproj(xns, w_in, col0, ncols, *, mode, out_dtype, tn, head_gain=None, seq=0, norm_gain=None, tm_cap=1024):
    d = w_in.shape[0]
    jb = col0 // tn
    assert col0 % tn == 0
    consts, epilogue = (), {"plain": _epi_plain, "sigmoid": _epi_sigmoid, "headnorm": _epi_headnorm}[mode]
    if mode == "headnorm":
        hid = jnp.arange(HEADNORM_GROUP, dtype=jnp.int32) // HEAD_DIM
        bd = (hid[:, None] == hid[None, :]).astype(BF16)
        gain = jnp.tile(head_gain.astype(F32), tn // HEAD_DIM).reshape(1, tn)
        consts = ((bd, bd.shape, lambda j: (0, 0)), (gain, (1, tn), lambda j: (0, 0)))
    row_sets = [_Rows(lhs=((xn, d, lambda j: 0),), head_major_seq=seq if s == 0 else 0)
                for s, xn in enumerate(xns)]
    return _ws_matmul("inproj_" + mode + ("_t" if seq else ""), ((w_in, d, lambda j: (0, j + jb)),), (0,),
                      consts, row_sets, epilogue, ncols, tn, out_dtype, tm_cap=tm_cap, norm_gain=norm_gain)


def _project(xs, norm_g, wts, seq):
    w_in = wts["w_in"]
    d = w_in.shape[0]
    aw, pw = ATTN_WIDTH, d // 2
    tn = INPROJ_COLS
    assert aw == tn
    q, xns = _inproj(xs, w_in, 0, aw, mode="headnorm", out_dtype=BF16, tn=tn,
                     head_gain=wts["q_norm_g"] * (HEAD_DIM ** -0.5 * LOG2E), seq=seq, norm_gain=norm_g,
                     tm_cap=512)
    k = _inproj(xns, w_in, aw, aw, mode="headnorm", out_dtype=F32, tn=tn, head_gain=wts["k_norm_g"], seq=seq)
    v = _inproj(xns, w_in, 2 * aw, aw, mode="plain", out_dtype=F32, tn=tn, seq=seq)
    u = _inproj(xns, w_in, 3 * aw, pw, mode="plain", out_dtype=F32, tn=tn)
    gates = _inproj(xns, w_in, 3 * aw + pw, 2 * d, mode="sigmoid", out_dtype=BF16, tn=tn)
    return q, k, v, u, gates


def _mix_kernel(o_ref, p_ref, ga_ref, gb_ref, x_ref, wap_ref, wp_ref, ps_ref, wo_ref, g2_ref, h_ref, hn_ref):
    a = jnp.dot(o_ref[...], wap_ref[...], preferred_element_type=F32)
    groups, gc, _ = wp_ref.shape
    op = jnp.concatenate(
        [jnp.dot(p_ref[:, g * gc:(g + 1) * gc], wp_ref[g], preferred_element_type=F32) for g in range(groups)],
        axis=1)
    merged = ga_ref[...].astype(F32) * a + gb_ref[...].astype(F32) * (op * ps_ref[...])
    h = x_ref[...] + jnp.dot(merged.astype(BF16), wo_ref[...], preferred_element_type=F32)
    h_ref[...] = h
    ms = jnp.mean(h * h, axis=-1, keepdims=True)
    hn_ref[...] = ((h * lax.rsqrt(ms + RMS_EPS)) * g2_ref[...]).astype(hn_ref.dtype)


def _mix(x, o_attn, p, gates, w_ap, w_pool, pool_scale, w_out, norm2_g, tm_cap=512):
    m, d = x.shape
    aw, pw = o_attn.shape[1], p.shape[1]
    tm = _row_tile(m, tm_cap)
    whole = lambda a: pl.BlockSpec(a.shape, lambda i: (0,) * a.ndim, pipeline_mode=pl.Buffered(1))
    ps2, g2 = pool_scale.reshape(1, d), norm2_g.reshape(1, d)
    return pl.pallas_call(
        _mix_kernel,
        grid=(m // tm,),
        in_specs=[
            pl.BlockSpec((tm, aw), lambda i: (i, 0)),
            pl.BlockSpec((tm, pw), lambda i: (i, 0)),
            pl.BlockSpec((tm, d), lambda i: (i, 0)),
            pl.BlockSpec((tm, d), lambda i: (i, 1)),
            pl.BlockSpec((tm, d), lambda i: (i, 0)),
            whole(w_ap), whole(w_pool), whole(ps2), whole(w_out), whole(g2),
        ],
        out_specs=[pl.BlockSpec((tm, d), lambda i: (i, 0)), pl.BlockSpec((tm, d), lambda i: (i, 0))],
        out_shape=[jax.ShapeDtypeStruct((m, d), F32), jax.ShapeDtypeStruct((m, d), BF16)],
        compiler_params=pltpu.CompilerParams(dimension_semantics=("parallel",), vmem_limit_bytes=LARGE_VMEM_LIMIT),
        name="mix",
    )(o_attn, p, gates, gates, x, w_ap, w_pool, ps2, w_out, g2)


def _mix_and_ffn(xs, o_attns, ps, gates, wts):
    w_gu, w_down = wts["w_gate_up"], wts["w_down"]
    dff, d = w_down.shape
    w_ap, w_pool, w_out = (wts[n].astype(BF16) for n in ("w_attn_proj", "w_pool", "w_out"))
    hs, hns = zip(*[_mix(x, o, p, gt, w_ap, w_pool, wts["pool_scale"], w_out, wts["norm2_g"])
                    for x, o, p, gt in zip(xs, o_attns, ps, gates)])
    nj = dff // 512
    acts = _ws_matmul(
        "gate_up", ((w_gu, d, lambda j: (0, j)), (w_gu, d, lambda j: (0, j + nj))), (0, 0), (),
        [_Rows(lhs=((hn, d, lambda j: 0),)) for hn in hns],
        _epi_swiglu, dff, 512, BF16, tm_cap=1024)
    return _ws_matmul(
        "down_proj", ((w_down, dff, lambda j: (0, j)),), (0,), (),
        [_Rows(lhs=((act, dff, lambda j: 0),), extras=((h, lambda j: j),)) for act, h in zip(acts, hs)],
        _epi_residual, d, 512, F32, tm_cap=512)


def kernel(x_prompt, x_sample, cache_k, cache_v, state_pool, page_table, norm1_g, w_in,
           q_norm_g, k_norm_g, sb_bias, w_attn_proj, w_pool, pool_scale, w_out, norm2_g, w_gate_up, w_down):
    b_p, seq, d = x_prompt.shape
    b_s, t_s = x_sample.shape[:2]
    depth, n_pool, page = cache_k.shape[:3]
    n_pages = page_table.shape[1]
    past_len = n_pages * page
    pw = d // 2

    xp = x_prompt.reshape(b_p * seq, d)
    xs = x_sample.reshape(b_s * t_s, d)
    outs = {name: [] for name in ("kp", "vp", "up", "ks", "vs", "us")}
    for l in range(depth):
        wts = {
            "w_in": w_in[l], "q_norm_g": q_norm_g[l], "k_norm_g": k_norm_g[l], "w_attn_proj": w_attn_proj[l],
            "w_pool": w_pool[l], "pool_scale": pool_scale[l], "w_out": w_out[l], "norm2_g": norm2_g[l],
            "w_gate_up": w_gate_up[l], "w_down": w_down[l],
        }
        bias = sb_bias[l].astype(F32) * LOG2E

        (qt_p, q_s), (kt_p, k_s), (vt_p, v_s), (u_p, u_s), (gates_p, gates_s) = _project(
            [xp, xs], norm1_g[l], wts, seq)

        q4 = q_s.reshape(b_s, t_s, 1, N_HEADS, HEAD_DIM)
        eye = jnp.eye(N_HEADS, dtype=BF16).reshape(1, 1, N_HEADS, N_HEADS, 1)
        q_rows = (q4 * eye).reshape(b_s, t_s * N_HEADS, ATTN_WIDTH)
        pad = ((0, 0), (0, 0), (0, page - t_s))
        kt_new = jnp.pad(k_s.reshape(b_s, t_s, ATTN_WIDTH).transpose(0, 2, 1), pad)
        vt_new = jnp.pad(v_s.reshape(b_s, t_s, ATTN_WIDTH).transpose(0, 2, 1), pad)
        cache_kt = cache_k[l].transpose(0, 2, 3, 1).reshape(n_pool, ATTN_WIDTH, page)
        cache_vt = cache_v[l].transpose(0, 2, 3, 1).reshape(n_pool, ATTN_WIDTH, page)
        o_attn_p, o_attn_s = _attention(qt_p, kt_p, vt_p, bias, q_rows, jnp.tile(bias, t_s), kt_new, vt_new,
                                        cache_kt, cache_vt, page_table, t_s)
        o_attn_s = o_attn_s.reshape(b_s * t_s, ATTN_WIDTH).astype(BF16)
        outs["kp"].append(kt_p.transpose(0, 3, 1, 2))
        outs["vp"].append(vt_p.transpose(0, 3, 1, 2))

        p_p = _pool_prompt(u_p, b_p, seq)
        outs["up"].append(u_p.reshape(b_p, seq, pw)[:, seq - POOL_STATE:])
        u3 = u_s.reshape(b_s, t_s, pw)
        halo = jnp.pad(state_pool[l], ((0, 0), (POOL_HALO - POOL_STATE, 0), (0, 0)))
        p_s = _pool_sample(u3, halo, past_len).reshape(b_s * t_s, pw).astype(BF16)
        outs["ks"].append(k_s.reshape(b_s, t_s, N_HEADS, HEAD_DIM))
        outs["vs"].append(v_s.reshape(b_s, t_s, N_HEADS, HEAD_DIM))
        outs["us"].append(jnp.concatenate([state_pool[l], u3], axis=1)[:, -POOL_STATE:])

        xp, xs = _mix_and_ffn([xp, xs], [o_attn_p, o_attn_s], [p_p, p_s], [gates_p, gates_s], wts)

    st = lambda name: jnp.stack(outs[name], axis=0)
    return (xp.reshape(b_p, seq, d), xs.reshape(b_s, t_s, d),
            st("kp"), st("vp"), st("up"), st("ks"), st("vs"), st("us"))
```

```python
import functools
from typing import NamedTuple

import jax
import jax.numpy as jnp
from jax import lax
from jax.experimental import pallas as pl
from jax.experimental.pallas import tpu as pltpu

F32 = jnp.float32
BF16 = jnp.bfloat16

N_HEADS = 16
HEAD_DIM = 64
ATTN_WIDTH = N_HEADS * HEAD_DIM
POOL_WINDOWS = (2, 4, 8, 16)
POOL_GROUPS = len(POOL_WINDOWS)
POOL_STATE = max(POOL_WINDOWS) - 1
POOL_HALO = 16
RMS_EPS = 1e-6

LANES = 128
SUBLANES = 8
HEADS_PER_BLOCK = LANES // HEAD_DIM
VMEM_LIMIT = 52 * 1024 * 1024
LARGE_VMEM_LIMIT = 57 * 1024 * 1024
ATTN_TILE = 512
CUMSUM_GROUP = 256
BIAS_TERMS = 3
BIAS_ROWS = 16
WEIGHTS_CHUNK = 32
DECODE_PAGES_PER_STEP = 8
INPROJ_COLS = 1024
HEADNORM_GROUP = 256
LOG2E = 1.4426950408889634

NT_DIMS = (((1,), (1,)), ((), ()))


def _cparams(*sem):
    return pltpu.CompilerParams(dimension_semantics=sem, vmem_limit_bytes=VMEM_LIMIT)


def _row_tile(m, cap):
    t = min(m, cap)
    while m % t:
        t //= 2
    return t


class _Rows(NamedTuple):
    lhs: tuple
    extras: tuple = ()
    head_major_seq: int = 0


def _ws_kernel(*refs, lhs_of_w, n_const, sets, epilogue, normalize, pool_seq):
    n_w = len(lhs_of_w)
    w_refs, refs = refs[:n_w], refs[n_w:]
    const_refs, refs = refs[:n_const], refs[n_const:]
    if normalize:
        gain_ref, refs = refs[0], refs[1:]
    set_refs = []
    for n_lhs, n_extra, _ in sets:
        set_refs.append((refs[:n_lhs], refs[n_lhs:n_lhs + n_extra]))
        refs = refs[n_lhs + n_extra:]
    out_refs, refs = refs[:len(sets)], refs[len(sets):]
    xn_refs, refs = (refs[:len(sets)], refs[len(sets):]) if normalize else ([None] * len(sets), refs)
    if pool_seq:
        p_ref, refs, ext_ref = refs[0], refs[1:-1], refs[-1]
    wb_refs = refs

    def pool(r):
        tm = r.shape[0]
        tile_in_seq = pl.program_id(1) % (pool_seq // tm)
        last_rows = ext_ref[pl.ds(tm, POOL_HALO), :]
        ext_ref[pl.ds(POOL_HALO, tm), :] = r
        ext_ref[pl.ds(0, POOL_HALO), :] = jnp.where(tile_in_seq == 0, 0.0, last_rows)
        _pool_tile(ext_ref, p_ref, tm, tile_in_seq * tm)

    def compute(lhs_refs, extra_refs, o_ref, head_major, xn_ref, pooled=False):
        lhs = [r[...] for r in lhs_refs]
        if normalize:
            x = lhs[0]
            ms = jnp.mean(x * x, axis=-1, keepdims=True)
            lhs = [((x * lax.rsqrt(ms + RMS_EPS)) * gain_ref[...]).astype(BF16)]
            xn_ref[...] = lhs[0]
        accs = [jnp.dot(lhs[lhs_of_w[k]], wb_refs[k][...], preferred_element_type=F32) for k in range(n_w)]
        r = epilogue(accs, [e[...] for e in extra_refs], [c[...] for c in const_refs])
        if pooled:
            pool(r)
        if head_major:
            r = r.T.reshape(o_ref.shape)
        o_ref[...] = r.astype(o_ref.dtype)

    @pl.when(pl.program_id(1) == 0)
    def _():
        for w_ref, wb_ref in zip(w_refs, wb_refs):
            wb_ref[...] = w_ref[...].astype(BF16)
        for s in range(1, len(sets)):
            compute(*set_refs[s], out_refs[s], sets[s][2], xn_refs[s])

    compute(*set_refs[0], out_refs[0], sets[0][2], xn_refs[0], pooled=bool(pool_seq))


def _ws_matmul(name, weights, lhs_of_w, consts, row_sets, epilogue, n_cols, tn, out_dtype, tm_cap,
               norm_gain=None, pool_seq=0):
    assert n_cols % tn == 0
    m0 = row_sets[0].lhs[0][0].shape[0]
    tm0 = _row_tile(row_sets[0].head_major_seq or m0, tm_cap)
    in_specs, args = [], []
    for w, kb, idx in weights:
        in_specs.append(pl.BlockSpec((kb, tn), lambda j, i, idx=idx: idx(j)))
        args.append(w)
    for c, shape, idx in consts:
        in_specs.append(pl.BlockSpec(shape, lambda j, i, idx=idx: idx(j)))
        args.append(c)
    if norm_gain is not None:
        assert n_cols == tn and all(len(rows.lhs) == 1 for rows in row_sets)
        in_specs.append(pl.BlockSpec((1, norm_gain.shape[0]), lambda j, i: (0, 0)))
        args.append(norm_gain.reshape(1, -1))
    out_specs, out_shapes, sets, xn_specs, xn_shapes = [], [], [], [], []
    for s, rows in enumerate(row_sets):
        m = rows.lhs[0][0].shape[0]
        tm = tm0 if s == 0 else m
        row = (lambda i: i) if s == 0 else (lambda i: 0)
        for arr, kb, kidx in rows.lhs:
            in_specs.append(pl.BlockSpec((tm, kb), lambda j, i, row=row, kidx=kidx: (row(i), kidx(j))))
            args.append(arr)
        for arr, cidx in rows.extras:
            in_specs.append(pl.BlockSpec((tm, tn), lambda j, i, row=row, cidx=cidx: (row(i), cidx(j))))
            args.append(arr)
        if rows.head_major_seq:
            assert s == 0
            seq = rows.head_major_seq
            nt, hpt = seq // tm, tn // HEAD_DIM
            out_specs.append(pl.BlockSpec((None, hpt, HEAD_DIM, tm), lambda j, i, nt=nt: (i // nt, j, 0, i % nt)))
            out_shapes.append(jax.ShapeDtypeStruct((m // seq, n_cols // HEAD_DIM, HEAD_DIM, seq), out_dtype))
        else:
            out_specs.append(pl.BlockSpec((tm, tn), lambda j, i, row=row: (row(i), j)))
            out_shapes.append(jax.ShapeDtypeStruct((m, n_cols), out_dtype))
        sets.append((len(rows.lhs), len(rows.extras), bool(rows.head_major_seq)))
        if norm_gain is not None:
            kd = rows.lhs[0][0].shape[1]
            xn_specs.append(pl.BlockSpec((tm, kd), lambda j, i, row=row: (row(i), 0)))
            xn_shapes.append(jax.ShapeDtypeStruct((m, kd), BF16))
    scratch = [pltpu.VMEM((kb, tn), BF16) for _, kb, _ in weights]
    if pool_seq:
        assert n_cols == tn and norm_gain is None and pool_seq % tm0 == 0 and not row_sets[0].head_major_seq
        xn_specs.append(pl.BlockSpec((tm0, tn), lambda j, i: (i, j)))
        xn_shapes.append(jax.ShapeDtypeStruct((m0, n_cols), BF16))
        scratch.append(pltpu.VMEM((POOL_HALO + tm0, tn), F32))
    outs = pl.pallas_call(
        functools.partial(_ws_kernel, lhs_of_w=tuple(lhs_of_w), n_const=len(consts), sets=tuple(sets),
                          epilogue=epilogue, normalize=norm_gain is not None, pool_seq=pool_seq),
        grid=(n_cols // tn, m0 // tm0),
        in_specs=in_specs,
        out_specs=out_specs + xn_specs,
        out_shape=out_shapes + xn_shapes,
        scratch_shapes=scratch,
        compiler_params=_cparams("parallel", "arbitrary"),
        name=name,
    )(*args)
    if pool_seq:
        return outs[:len(row_sets)], outs[len(row_sets)]
    return outs if norm_gain is None else (outs[:len(row_sets)], outs[len(row_sets):])


def _epi_plain(accs, extras, consts):
    return accs[0]


def _epi_headnorm(accs, extras, consts):
    h, (bd, gain) = accs[0], consts
    w = bd.shape[0]
    hh = (h * h).astype(BF16)
    ss = [jnp.dot(hh[:, c:c + w], bd, preferred_element_type=F32) for c in range(0, h.shape[1], w)]
    ss = ss[0] if len(ss) == 1 else jnp.concatenate(ss, axis=1)
    return (h * lax.rsqrt(ss * (1.0 / HEAD_DIM) + RMS_EPS)) * gain


def _epi_sigmoid(accs, extras, consts):
    return 1.0 / (1.0 + jnp.exp(-accs[0]))


def _epi_residual(accs, extras, consts):
    return extras[0] + accs[0]


def _epi_swiglu(accs, extras, consts):
    gate, up = accs
    return (gate / (1.0 + jnp.exp(-gate))) * up


def _cumsum_rhs(g):
    r = lax.broadcasted_iota(jnp.int32, (g, g + LANES), 0)
    c = lax.broadcasted_iota(jnp.int32, (g, g + LANES), 1)
    return jnp.where((r > c) | (c >= g), 1.0, 0.0).astype(BF16)


def _sb_weights(z, carry, u, mask):
    g = u.shape[0]
    neg_abs = lax.bitcast_convert_type(lax.bitcast_convert_type(z, jnp.uint32) | jnp.uint32(1 << 31), F32)
    sp = jnp.maximum(z, 0.0) + jnp.log(1.0 + jnp.exp2(neg_abs)) * LOG2E
    if mask is not None:
        sp = jnp.where(mask, sp, 0.0)
    log_beta = z - sp
    sp16 = sp.astype(BF16)
    parts = []
    for c in reversed(range(z.shape[1] // g)):
        cols = slice(c * g, (c + 1) * g)
        cum = jnp.dot(sp16[:, cols], u, preferred_element_type=F32)
        for l in reversed(range(0, g, LANES)):
            parts.append(jnp.exp2(log_beta[:, c * g + l:c * g + l + LANES] - cum[:, l:l + LANES] - carry))
        carry = carry + cum[:, g:]
    a = parts[0] if len(parts) == 1 else jnp.concatenate(parts[::-1], axis=1)
    if mask is not None:
        a = jnp.where(mask, a, 0.0)
    return carry, a.astype(BF16)


def _cumsum_lhs(g):
    r = lax.broadcasted_iota(jnp.int32, (g + 2 * SUBLANES, g), 0)
    c = lax.broadcasted_iota(jnp.int32, (g + 2 * SUBLANES, g), 1)
    return jnp.where((c > r) | (r >= g), 1.0, 0.0).astype(BF16)


def _softplus_t(z_ref, sp_ref, lb_ref, mask):
    ch = WEIGHTS_CHUNK
    for r0 in range(0, z_ref.shape[0], ch):
        rows = pl.ds(r0, ch)
        zt = z_ref[rows, :]
        neg_abs = lax.bitcast_convert_type(lax.bitcast_convert_type(zt, jnp.uint32) | jnp.uint32(1 << 31), F32)
        sp = jnp.maximum(zt, 0.0) + jnp.log(1.0 + jnp.exp2(neg_abs)) * LOG2E
        if mask is not None:
            sp = jnp.where(mask[r0:r0 + ch], sp, 0.0)
        sp_ref[rows, :] = sp.astype(BF16)
        lb_ref[rows, :] = zt - sp


def _stick_weights_t(lb_ref, cum_ref, carry, a_ref, mask):
    g, nq = lb_ref.shape
    ch = WEIGHTS_CHUNK
    for r0 in range(0, g, ch):
        rows = pl.ds(r0, ch)
        x = (lb_ref[rows, :] - cum_ref[rows, :]).reshape(ch // SUBLANES, SUBLANES, nq) - carry[None]
        a = jnp.exp2(x).reshape(ch, nq)
        if mask is not None:
            a = jnp.where(mask[r0:r0 + ch], a, 0.0)
        a_ref[rows, :] = a.astype(BF16)
    return carry + cum_ref[pl.ds(g, SUBLANES), :]


def _sb_block(z, carry, u, mask, vt):
    carry, a = _sb_weights(z, carry, u, mask)
    return carry, lax.dot_general(a, vt, NT_DIMS, preferred_element_type=F32)


class _DecodePlan(NamedTuple):
    n_batch: int
    t_new: int
    n_pages: int
    pages_per_group: int
    n_groups: int
    steps_per_batch: int
    slots_per_step: int
    groups_per_slot: int
    every_slot_full: bool


def _attn_kernel(pt_ref, bias_ref, qt_ref, kt_ref, vt_ref, qd_ref, biasd_ref, kn_ref, vn_ref, ck_hbm, cv_hbm,
                 o_ref, od_ref, ka_ref, qz_ref, vz_ref, lcum_ref, carry_ref, acc_ref, za_ref, zb_ref, aa_ref, ab_ref,
                 sp_ref, lb_ref, cum_ref, qs_ref, ks_ref, vs_ref, pk_ref, pv_ref, sem, kb_ref, vb_ref, dacc_ref, dcarry_ref, du_ref, *, seq, tile, plan):
    pair = pl.program_id(1)
    step = pl.program_id(0) * pl.num_programs(1) + pair
    kw = tile // 2
    nkb = seq // kw
    lcum_ref[...] = _cumsum_lhs(lcum_ref.shape[1])
    ka_ref[:, pl.ds(LANES, LANES)] = jnp.ones((seq, LANES), BF16)
    for cb in range(nkb):
        cols = pl.ds(cb * kw, kw)
        ka_ref[cols, pl.ds(0, LANES)] = kt_ref[:, :, cols].reshape(LANES, kw).T.astype(BF16)
    qz_ref[...] = jnp.zeros(qz_ref.shape, BF16)
    term_row = lax.broadcasted_iota(jnp.int32, (BIAS_ROWS, tile), 0)
    for h in range(HEADS_PER_BLOCK):
        rows = pl.ds(h * HEAD_DIM, HEAD_DIM)
        rest = jnp.full((BIAS_ROWS, tile), bias_ref[pair * HEADS_PER_BLOCK + h], F32)
        terms = jnp.zeros((BIAS_ROWS, tile), F32)
        for i in range(BIAS_TERMS):
            term = rest.astype(BF16).astype(F32)
            terms = jnp.where(term_row == i, term, terms)
            rest = rest - term
        for qb in range(seq // tile):
            cols = pl.ds(qb * tile, tile)
            qz_ref[h, qb, rows, :] = qt_ref[h, :, cols]
            qz_ref[h, qb, pl.ds(LANES, BIAS_ROWS), :] = terms.astype(BF16)
        for cb in range(nkb):
            vz_ref[h, cb] = vt_ref[h, :, pl.ds(cb * kw, kw)].astype(BF16)

    rk = lax.broadcasted_iota(jnp.int32, (tile, tile), 0)
    cq = lax.broadcasted_iota(jnp.int32, (tile, tile), 1)
    diag_mask = rk < cq

    heads = range(HEADS_PER_BLOCK)

    rows_d, slots = qd_ref.shape[0], kn_ref.shape[1]
    n_all_groups = plan.n_batch * plan.n_groups
    step_in_batch = step % plan.steps_per_batch
    batch_d = step // plan.steps_per_batch

    def group_copies(grp):
        bd, n, buf = grp // plan.n_groups, grp % plan.n_groups, grp % 2
        copies = []
        for r in range(plan.pages_per_group):
            page = pt_ref[bd, plan.n_pages - 1 - n * plan.pages_per_group - r]
            copies.append(pltpu.make_async_copy(ck_hbm.at[page], pk_ref.at[buf, r], sem.at[buf, 0, r]))
            copies.append(pltpu.make_async_copy(cv_hbm.at[page], pv_ref.at[buf, r], sem.at[buf, 1, r]))
        return copies

    def start_group(grp):
        for c in group_copies(grp):
            c.start()

    def decode_group(grp):
        buf = grp % 2
        for c in group_copies(grp):
            c.wait()

        @pl.when(grp + 1 < n_all_groups)
        def _():
            start_group(grp + 1)

        for r in range(plan.pages_per_group):
            cols = pl.ds((plan.pages_per_group - 1 - r) * slots, slots)
            kb_ref[:, cols] = pk_ref[buf, r].astype(BF16)
            vb_ref[:, cols] = pv_ref[buf, r].astype(BF16)
        s = jnp.dot(qd_ref[...], kb_ref[...], preferred_element_type=F32)
        dcarry_ref[...], pv = _sb_block(s + biasd_ref[...], dcarry_ref[...], du_ref[...], None, vb_ref[...])
        dacc_ref[...] += pv

    def decode_new_tokens():
        du_ref[...] = _cumsum_rhs(slots)
        r = lax.broadcasted_iota(jnp.int32, (rows_d, slots), 0)
        c = lax.broadcasted_iota(jnp.int32, (rows_d, slots), 1)
        s = jnp.dot(qd_ref[...], kn_ref[...].astype(BF16), preferred_element_type=F32)
        dcarry_ref[...], dacc_ref[...] = _sb_block(
            s + biasd_ref[:, pl.ds(0, slots)], jnp.zeros((rows_d, LANES), F32), du_ref[...], c < (r // N_HEADS),
            vn_ref[...].astype(BF16))

    def decode_finish():
        r = lax.broadcasted_iota(jnp.int32, dacc_ref.shape, 0)
        l = lax.broadcasted_iota(jnp.int32, dacc_ref.shape, 1)
        own = jnp.where((r % N_HEADS) == (l // HEAD_DIM), dacc_ref[...], 0.0)
        for t in range(plan.t_new):
            od_ref[pl.ds(t, 1), :] = jnp.sum(own[t * N_HEADS:(t + 1) * N_HEADS], axis=0, keepdims=True)

    def decode_slot(qb):
        slot = step_in_batch * plan.slots_per_step + qb

        @pl.when((step == 0) & (qb == 0))
        def _():
            start_group(0)

        pl.when(slot == 0)(decode_new_tokens)
        for t in range(plan.groups_per_slot):
            n = slot * plan.groups_per_slot + t
            grp = batch_d * plan.n_groups + n
            if plan.every_slot_full:
                decode_group(grp)
            else:
                pl.when(n < plan.n_groups)(functools.partial(decode_group, grp))
        if plan.every_slot_full:
            decode_finish()
        else:
            pl.when(slot == plan.steps_per_batch * plan.slots_per_step - 1)(decode_finish)

    def qbody(qb, _):
        decode_slot(qb)
        qoff = pl.multiple_of(qb * tile, tile)
        k0 = 2 * qb

        for h in heads:
            qs_ref[h] = qz_ref[h, qb]

        def stage_keys(slot, kb):
            ks_ref[slot] = ka_ref[pl.ds(pl.multiple_of(kb * kw, kw), kw), :]

        def stage_values(slot, kb):
            for h in heads:
                vs_ref[slot, h] = vz_ref[h, kb]

        def logits(slot, z_ref):
            for h in heads:
                z_ref[h] = jnp.dot(ks_ref[slot], qs_ref[h], preferred_element_type=F32)

        def softplus(z_ref, half, mask=None):
            for h in heads:
                _softplus_t(z_ref.at[h], sp_ref.at[half, h], lb_ref.at[half, h], mask)

        def cumsums(half):
            for h in heads:
                cum_ref[half, h] = jnp.dot(lcum_ref[...], sp_ref[half, h], preferred_element_type=F32)

        def stick(half, a_ref, mask=None, first=False):
            for h in heads:
                carry = jnp.zeros((SUBLANES, tile), F32) if first else carry_ref[h]
                carry_ref[h] = _stick_weights_t(lb_ref.at[half, h], cum_ref.at[half, h], carry, a_ref.at[h], mask)

        def values(a_ref, slot, first=False):
            for h in heads:
                pv = jnp.dot(vs_ref[slot, h], a_ref[h], preferred_element_type=F32)
                acc_ref[h] = pv if first else acc_ref[h] + pv


        def weights_and_next_logits(mask_a=None, mask_b=None, first=False):
            softplus(za_ref, 0, mask_a)
            cumsums(0)
            logits(0, za_ref)
            softplus(zb_ref, 1, mask_b)
            logits(1, zb_ref)
            cumsums(1)
            stick(0, aa_ref, mask_a, first)
            stick(1, ab_ref, mask_b)

        acc_ref[...] = jnp.zeros(acc_ref.shape, F32)
        stage_keys(2, k0 + 1)
        stage_keys(3, k0)
        stage_keys(0, jnp.maximum(k0 - 1, 0))
        stage_keys(1, jnp.maximum(k0 - 2, 0))
        logits(2, za_ref)
        logits(3, zb_ref)
        weights_and_next_logits(diag_mask[kw:], diag_mask[:kw], first=True)

        def kbody(i, _):
            ka = k0 - 1 - 2 * i
            stage_keys(0, jnp.maximum(ka - 2, 0))
            stage_keys(1, jnp.maximum(ka - 3, 0))
            stage_values(0, ka + 2)
            stage_values(1, ka + 1)
            values(aa_ref, 0)
            values(ab_ref, 1)
            weights_and_next_logits()
            return 0

        lax.fori_loop(0, qb, kbody, 0)
        stage_values(0, 1)
        stage_values(1, 0)
        values(aa_ref, 0)
        values(ab_ref, 1)
        out = jnp.concatenate([acc_ref[h] for h in heads], axis=0)
        o_ref[pl.ds(qoff, tile), :] = out.T.astype(o_ref.dtype)
        return 0

    lax.fori_loop(0, seq // tile, qbody, 0)


def _attention(qt, kt, vt, bias, q_rows, row_bias, kt_new, vt_new, cache_kt, cache_vt, page_table, t_new):
    batch, n_heads, _, seq = qt.shape
    m, w = batch * seq, n_heads * HEAD_DIM
    npairs = w // LANES
    tile = _row_tile(seq, ATTN_TILE)
    kw = tile // 2
    assert kw <= CUMSUM_GROUP
    group = kw
    nb, rows_d, _ = q_rows.shape
    n_pages, slots = page_table.shape[1], cache_kt.shape[2]
    ppg = max(p for p in (DECODE_PAGES_PER_STEP, 4, 2, 1) if n_pages % p == 0)
    n_steps = batch * npairs
    assert n_steps % nb == 0, (n_steps, nb)
    n_slots = (n_steps // nb) * (seq // tile)
    n_groups = n_pages // ppg
    gps = -(-n_groups // n_slots)
    plan = _DecodePlan(n_batch=nb, t_new=t_new, n_pages=n_pages, pages_per_group=ppg, n_groups=n_groups,
                       steps_per_batch=n_steps // nb, slots_per_step=seq // tile, groups_per_slot=gps,
                       every_slot_full=gps * n_slots == n_groups)
    bias_rows = jnp.broadcast_to(row_bias[:, None], (rows_d, ppg * slots))
    spb = plan.steps_per_batch
    kv_spec = pl.BlockSpec((None, HEADS_PER_BLOCK, HEAD_DIM, seq), lambda b, p, pt: (b, p, 0, 0))
    per_d = lambda b, p, pt: ((b * npairs + p) // spb, 0, 0)
    grid_spec = pltpu.PrefetchScalarGridSpec(
        num_scalar_prefetch=1,
        grid=(batch, npairs),
        in_specs=[
            pl.BlockSpec(memory_space=pltpu.SMEM),
            kv_spec,
            kv_spec,
            kv_spec,
            pl.BlockSpec((None, rows_d, w), per_d),
            pl.BlockSpec((rows_d, ppg * slots), lambda b, p, pt: (0, 0)),
            pl.BlockSpec((None, w, slots), per_d),
            pl.BlockSpec((None, w, slots), per_d),
            pl.BlockSpec(memory_space=pl.ANY),
            pl.BlockSpec(memory_space=pl.ANY),
        ],
        out_specs=[pl.BlockSpec((seq, LANES), lambda b, p, pt: (b, p)),
                   pl.BlockSpec((None, t_new, w), per_d)],
        scratch_shapes=[
            pltpu.VMEM((seq, 2 * LANES), BF16),
            pltpu.VMEM((HEADS_PER_BLOCK, seq // tile, 2 * LANES, tile), BF16),
            pltpu.VMEM((HEADS_PER_BLOCK, seq // kw, HEAD_DIM, kw), BF16),
            pltpu.VMEM((group + 2 * SUBLANES, group), BF16),
            pltpu.VMEM((HEADS_PER_BLOCK, SUBLANES, tile), F32),
            pltpu.VMEM((HEADS_PER_BLOCK, HEAD_DIM, tile), F32),
            pltpu.VMEM((HEADS_PER_BLOCK, kw, tile), F32),
            pltpu.VMEM((HEADS_PER_BLOCK, kw, tile), F32),
            pltpu.VMEM((HEADS_PER_BLOCK, kw, tile), BF16),
            pltpu.VMEM((HEADS_PER_BLOCK, kw, tile), BF16),
            pltpu.VMEM((2, HEADS_PER_BLOCK, kw, tile), BF16),
            pltpu.VMEM((2, HEADS_PER_BLOCK, kw, tile), F32),
            pltpu.VMEM((2, HEADS_PER_BLOCK, kw + 2 * SUBLANES, tile), F32),
            pltpu.VMEM((HEADS_PER_BLOCK, 2 * LANES, tile), BF16),
            pltpu.VMEM((4, kw, 2 * LANES), BF16),
            pltpu.VMEM((2, HEADS_PER_BLOCK, HEAD_DIM, kw), BF16),
            pltpu.VMEM((2, ppg, w, slots), F32),
            pltpu.VMEM((2, ppg, w, slots), F32),
            pltpu.SemaphoreType.DMA((2, 2, ppg)),
            pltpu.VMEM((w, ppg * slots), BF16),
            pltpu.VMEM((w, ppg * slots), BF16),
            pltpu.VMEM((rows_d, w), F32),
            pltpu.VMEM((rows_d, LANES), F32),
            pltpu.VMEM((slots, slots + LANES), BF16),
        ],
    )
    return pl.pallas_call(
        functools.partial(_attn_kernel, seq=seq, tile=tile, plan=plan),
        grid_spec=grid_spec,
        out_shape=[jax.ShapeDtypeStruct((m, w), BF16), jax.ShapeDtypeStruct((nb, t_new, w), F32)],
        compiler_params=pltpu.CompilerParams(dimension_semantics=("arbitrary", "arbitrary"),
                                             vmem_limit_bytes=LARGE_VMEM_LIMIT),
        name="attention",
    )(page_table, bias, qt, kt, vt, q_rows, bias_rows, kt_new, vt_new, cache_kt, cache_vt)


def _pool_tile(ext_ref, p_ref, tp, pos0):
    gc = ext_ref.shape[1] // POOL_GROUPS
    pos = pos0 + lax.broadcasted_iota(jnp.int32, (tp, 1), 0)
    for g, win in enumerate(POOL_WINDOWS):
        cols = slice(g * gc, (g + 1) * gc)
        u_new = ext_ref[pl.ds(POOL_HALO, tp), cols]
        win_sum = u_new
        for d in range(1, win):
            win_sum = win_sum + ext_ref[pl.ds(POOL_HALO - d, tp), cols]
        cnt = jnp.minimum(win, pos + 1).astype(F32)
        p_ref[:, cols] = (win_sum / cnt - u_new).astype(p_ref.dtype)


def _pool_sample_kernel(u_ref, halo_ref, p_ref, ext_ref, *, tp, pos0):
    ext_ref[pl.ds(0, POOL_HALO), :] = halo_ref[...]
    ext_ref[pl.ds(POOL_HALO, tp), :] = u_ref[...]
    _pool_tile(ext_ref, p_ref, tp, pos0)


def _pool_sample(u3, halo3, pos0):
    nb, tp, w = u3.shape
    return pl.pallas_call(
        functools.partial(_pool_sample_kernel, tp=tp, pos0=pos0),
        grid=(nb,),
        in_specs=[
            pl.BlockSpec((None, tp, w), lambda b: (b, 0, 0)),
            pl.BlockSpec((None, POOL_HALO, w), lambda b: (b, 0, 0)),
        ],
        out_specs=pl.BlockSpec((None, tp, w), lambda b: (b, 0, 0)),
        out_shape=jax.ShapeDtypeStruct((nb, tp, w), F32),
        scratch_shapes=[pltpu.VMEM((POOL_HALO + tp, w), F32)],
        compiler_params=_cparams("parallel"),
        name="pool_sample",
    )(u3, halo3)


def _inproj(xns, w_in, col0, ncols, *, mode, out_dtype, tn, head_gain=None, seq=0, norm_gain=None, tm_cap=1024,
            pool_seq=0):
    d = w_in.shape[0]
    jb = col0 // tn
    assert col0 % tn == 0
    consts, epilogue = (), {"plain": _epi_plain, "sigmoid": _epi_sigmoid, "headnorm": _epi_headnorm}[mode]
    if mode == "headnorm":
        hid = jnp.arange(HEADNORM_GROUP, dtype=jnp.int32) // HEAD_DIM
        bd = (hid[:, None] == hid[None, :]).astype(BF16)
        gain = jnp.tile(head_gain.astype(F32), tn // HEAD_DIM).reshape(1, tn)
        consts = ((bd, bd.shape, lambda j: (0, 0)), (gain, (1, tn), lambda j: (0, 0)))
    row_sets = [_Rows(lhs=((xn, d, lambda j: 0),), head_major_seq=seq if s == 0 else 0)
                for s, xn in enumerate(xns)]
    return _ws_matmul("inproj_" + mode + ("_t" if seq else ""), ((w_in, d, lambda j: (0, j + jb)),), (0,),
                      consts, row_sets, epilogue, ncols, tn, out_dtype, tm_cap=tm_cap, norm_gain=norm_gain,
                      pool_seq=pool_seq)


def _project(xs, norm_g, wts, seq):
    w_in = wts["w_in"]
    d = w_in.shape[0]
    aw, pw = ATTN_WIDTH, d // 2
    tn = INPROJ_COLS
    assert aw == tn
    q, xns = _inproj(xs, w_in, 0, aw, mode="headnorm", out_dtype=BF16, tn=tn,
                     head_gain=wts["q_norm_g"] * (HEAD_DIM ** -0.5 * LOG2E), seq=seq, norm_gain=norm_g,
                     tm_cap=512)
    k = _inproj(xns, w_in, aw, aw, mode="headnorm", out_dtype=F32, tn=tn, head_gain=wts["k_norm_g"], seq=seq)
    v = _inproj(xns, w_in, 2 * aw, aw, mode="plain", out_dtype=F32, tn=tn, seq=seq)
    assert pw == tn
    u, p_prompt = _inproj(xns, w_in, 3 * aw, pw, mode="plain", out_dtype=F32, tn=tn, pool_seq=seq)
    gates = _inproj(xns, w_in, 3 * aw + pw, 2 * d, mode="sigmoid", out_dtype=BF16, tn=tn)
    return q, k, v, u, gates, p_prompt


def _mix_kernel(o_ref, p_ref, ga_ref, gb_ref, x_ref, wap_ref, wp_ref, ps_ref, wo_ref, g2_ref, h_ref, hn_ref):
    a = jnp.dot(o_ref[...], wap_ref[...], preferred_element_type=F32)
    groups, gc, _ = wp_ref.shape
    op = jnp.concatenate(
        [jnp.dot(p_ref[:, g * gc:(g + 1) * gc], wp_ref[g], preferred_element_type=F32) for g in range(groups)],
        axis=1)
    merged = ga_ref[...].astype(F32) * a + gb_ref[...].astype(F32) * (op * ps_ref[...])
    h = x_ref[...] + jnp.dot(merged.astype(BF16), wo_ref[...], preferred_element_type=F32)
    h_ref[...] = h
    ms = jnp.mean(h * h, axis=-1, keepdims=True)
    hn_ref[...] = ((h * lax.rsqrt(ms + RMS_EPS)) * g2_ref[...]).astype(hn_ref.dtype)


def _mix(x, o_attn, p, gates, w_ap, w_pool, pool_scale, w_out, norm2_g, tm_cap=512):
    m, d = x.shape
    aw, pw = o_attn.shape[1], p.shape[1]
    tm = _row_tile(m, tm_cap)
    whole = lambda a: pl.BlockSpec(a.shape, lambda i: (0,) * a.ndim, pipeline_mode=pl.Buffered(1))
    ps2, g2 = pool_scale.reshape(1, d), norm2_g.reshape(1, d)
    return pl.pallas_call(
        _mix_kernel,
        grid=(m // tm,),
        in_specs=[
            pl.BlockSpec((tm, aw), lambda i: (i, 0)),
            pl.BlockSpec((tm, pw), lambda i: (i, 0)),
            pl.BlockSpec((tm, d), lambda i: (i, 0)),
            pl.BlockSpec((tm, d), lambda i: (i, 1)),
            pl.BlockSpec((tm, d), lambda i: (i, 0)),
            whole(w_ap), whole(w_pool), whole(ps2), whole(w_out), whole(g2),
        ],
        out_specs=[pl.BlockSpec((tm, d), lambda i: (i, 0)), pl.BlockSpec((tm, d), lambda i: (i, 0))],
        out_shape=[jax.ShapeDtypeStruct((m, d), F32), jax.ShapeDtypeStruct((m, d), BF16)],
        compiler_params=pltpu.CompilerParams(dimension_semantics=("parallel",), vmem_limit_bytes=LARGE_VMEM_LIMIT),
        name="mix",
    )(o_attn, p, gates, gates, x, w_ap, w_pool, ps2, w_out, g2)


def _mix_and_ffn(xs, o_attns, ps, gates, wts):
    w_gu, w_down = wts["w_gate_up"], wts["w_down"]
    dff, d = w_down.shape
    w_ap, w_pool, w_out = (wts[n].astype(BF16) for n in ("w_attn_proj", "w_pool", "w_out"))
    hs, hns = zip(*[_mix(x, o, p, gt, w_ap, w_pool, wts["pool_scale"], w_out, wts["norm2_g"])
                    for x, o, p, gt in zip(xs, o_attns, ps, gates)])
    nj = dff // 512
    acts = _ws_matmul(
        "gate_up", ((w_gu, d, lambda j: (0, j)), (w_gu, d, lambda j: (0, j + nj))), (0, 0), (),
        [_Rows(lhs=((hn, d, lambda j: 0),)) for hn in hns],
        _epi_swiglu, dff, 512, BF16, tm_cap=1024)
    return _ws_matmul(
        "down_proj", ((w_down, dff, lambda j: (0, j)),), (0,), (),
        [_Rows(lhs=((act, dff, lambda j: 0),), extras=((h, lambda j: j),)) for act, h in zip(acts, hs)],
        _epi_residual, d, 512, F32, tm_cap=512)


def kernel(x_prompt, x_sample, cache_k, cache_v, state_pool, page_table, norm1_g, w_in,
           q_norm_g, k_norm_g, sb_bias, w_attn_proj, w_pool, pool_scale, w_out, norm2_g, w_gate_up, w_down):
    b_p, seq, d = x_prompt.shape
    b_s, t_s = x_sample.shape[:2]
    depth, n_pool, page = cache_k.shape[:3]
    n_pages = page_table.shape[1]
    past_len = n_pages * page
    pw = d // 2

    xp = x_prompt.reshape(b_p * seq, d)
    xs = x_sample.reshape(b_s * t_s, d)
    outs = {name: [] for name in ("kp", "vp", "up", "ks", "vs", "us")}
    for l in range(depth):
        wts = {
            "w_in": w_in[l], "q_norm_g": q_norm_g[l], "k_norm_g": k_norm_g[l], "w_attn_proj": w_attn_proj[l],
            "w_pool": w_pool[l], "pool_scale": pool_scale[l], "w_out": w_out[l], "norm2_g": norm2_g[l],
            "w_gate_up": w_gate_up[l], "w_down": w_down[l],
        }
        bias = sb_bias[l].astype(F32) * LOG2E

        (qt_p, q_s), (kt_p, k_s), (vt_p, v_s), (u_p, u_s), (gates_p, gates_s), p_p = _project(
            [xp, xs], norm1_g[l], wts, seq)

        q4 = q_s.reshape(b_s, t_s, 1, N_HEADS, HEAD_DIM)
        eye = jnp.eye(N_HEADS, dtype=BF16).reshape(1, 1, N_HEADS, N_HEADS, 1)
        q_rows = (q4 * eye).reshape(b_s, t_s * N_HEADS, ATTN_WIDTH)
        pad = ((0, 0), (0, 0), (0, page - t_s))
        kt_new = jnp.pad(k_s.reshape(b_s, t_s, ATTN_WIDTH).transpose(0, 2, 1), pad)
        vt_new = jnp.pad(v_s.reshape(b_s, t_s, ATTN_WIDTH).transpose(0, 2, 1), pad)
        cache_kt = cache_k[l].transpose(0, 2, 3, 1).reshape(n_pool, ATTN_WIDTH, page)
        cache_vt = cache_v[l].transpose(0, 2, 3, 1).reshape(n_pool, ATTN_WIDTH, page)
        o_attn_p, o_attn_s = _attention(qt_p, kt_p, vt_p, bias, q_rows, jnp.tile(bias, t_s), kt_new, vt_new,
                                        cache_kt, cache_vt, page_table, t_s)
        o_attn_s = o_attn_s.reshape(b_s * t_s, ATTN_WIDTH).astype(BF16)
        outs["kp"].append(kt_p.transpose(0, 3, 1, 2))
        outs["vp"].append(vt_p.transpose(0, 3, 1, 2))

        outs["up"].append(u_p.reshape(b_p, seq, pw)[:, seq - POOL_STATE:])
        u3 = u_s.reshape(b_s, t_s, pw)
        halo = jnp.pad(state_pool[l], ((0, 0), (POOL_HALO - POOL_STATE, 0), (0, 0)))
        p_s = _pool_sample(u3, halo, past_len).reshape(b_s * t_s, pw).astype(BF16)
        outs["ks"].append(k_s.reshape(b_s, t_s, N_HEADS, HEAD_DIM))
        outs["vs"].append(v_s.reshape(b_s, t_s, N_HEADS, HEAD_DIM))
        outs["us"].append(jnp.concatenate([state_pool[l], u3], axis=1)[:, -POOL_STATE:])

        xp, xs = _mix_and_ffn([xp, xs], [o_attn_p, o_attn_s], [p_p, p_s], [gates_p, gates_s], wts)

    st = lambda name: jnp.stack(outs[name], axis=0)
    return (xp.reshape(b_p, seq, d), xs.reshape(b_s, t_s, d),
            st("kp"), st("vp"), st("up"), st("ks"), st("vs"), st("us"))
```

```python
import functools
from typing import NamedTuple

import jax
import jax.numpy as jnp
from jax import lax
from jax.experimental import pallas as pl
from jax.experimental.pallas import tpu as pltpu

F32 = jnp.float32
BF16 = jnp.bfloat16

N_HEADS = 16
HEAD_DIM = 64
ATTN_WIDTH = N_HEADS * HEAD_DIM
POOL_WINDOWS = (2, 4, 8, 16)
POOL_GROUPS = len(POOL_WINDOWS)
POOL_STATE = max(POOL_WINDOWS) - 1
POOL_HALO = 16
RMS_EPS = 1e-6

LANES = 128
SUBLANES = 8
HEADS_PER_BLOCK = LANES // HEAD_DIM
VMEM_LIMIT = 52 * 1024 * 1024
LARGE_VMEM_LIMIT = 57 * 1024 * 1024
ATTN_TILE = 512
CUMSUM_GROUP = 256
BIAS_TERMS = 3
BIAS_ROWS = 16
WEIGHTS_CHUNK = 32
DECODE_PAGES_PER_STEP = 8
INPROJ_COLS = 1024
HEADNORM_GROUP = 256
LOG2E = 1.4426950408889634

NT_DIMS = (((1,), (1,)), ((), ()))


def _cparams(*sem):
    return pltpu.CompilerParams(dimension_semantics=sem, vmem_limit_bytes=VMEM_LIMIT)


def _row_tile(m, cap):
    t = min(m, cap)
    while m % t:
        t //= 2
    return t


class _Rows(NamedTuple):
    lhs: tuple
    extras: tuple = ()
    head_major_seq: int = 0


def _ws_kernel(*refs, lhs_of_w, n_const, sets, epilogue, normalize, pool_seq):
    n_w = len(lhs_of_w)
    w_refs, refs = refs[:n_w], refs[n_w:]
    const_refs, refs = refs[:n_const], refs[n_const:]
    if normalize:
        gain_ref, refs = refs[0], refs[1:]
    set_refs = []
    for n_lhs, n_extra, _ in sets:
        set_refs.append((refs[:n_lhs], refs[n_lhs:n_lhs + n_extra]))
        refs = refs[n_lhs + n_extra:]
    out_refs, refs = refs[:len(sets)], refs[len(sets):]
    xn_refs, refs = (refs[:len(sets)], refs[len(sets):]) if normalize else ([None] * len(sets), refs)
    if pool_seq:
        p_ref, refs, ext_ref = refs[0], refs[1:-1], refs[-1]
    wb_refs = refs

    def pool(r):
        tm = r.shape[0]
        tile_in_seq = pl.program_id(1) % (pool_seq // tm)
        last_rows = ext_ref[pl.ds(tm, POOL_HALO), :]
        ext_ref[pl.ds(POOL_HALO, tm), :] = r
        ext_ref[pl.ds(0, POOL_HALO), :] = jnp.where(tile_in_seq == 0, 0.0, last_rows)
        _pool_tile(ext_ref, p_ref, tm, tile_in_seq * tm)

    def compute(lhs_refs, extra_refs, o_ref, head_major, xn_ref, pooled=False):
        lhs = [r[...] for r in lhs_refs]
        if normalize:
            x = lhs[0]
            ms = jnp.mean(x * x, axis=-1, keepdims=True)
            lhs = [((x * lax.rsqrt(ms + RMS_EPS)) * gain_ref[...]).astype(BF16)]
            xn_ref[...] = lhs[0]
        accs = [jnp.dot(lhs[lhs_of_w[k]], wb_refs[k][...], preferred_element_type=F32) for k in range(n_w)]
        r = epilogue(accs, [e[...] for e in extra_refs], [c[...] for c in const_refs])
        if pooled:
            pool(r)
        if head_major:
            r = r.T.reshape(o_ref.shape)
        o_ref[...] = r.astype(o_ref.dtype)

    @pl.when(pl.program_id(1) == 0)
    def _():
        for w_ref, wb_ref in zip(w_refs, wb_refs):
            wb_ref[...] = w_ref[...].astype(BF16)
        for s in range(1, len(sets)):
            compute(*set_refs[s], out_refs[s], sets[s][2], xn_refs[s])

    compute(*set_refs[0], out_refs[0], sets[0][2], xn_refs[0], pooled=bool(pool_seq))


def _ws_matmul(name, weights, lhs_of_w, consts, row_sets, epilogue, n_cols, tn, out_dtype, tm_cap,
               norm_gain=None, pool_seq=0, single_buffer_weights=False):
    assert n_cols % tn == 0
    m0 = row_sets[0].lhs[0][0].shape[0]
    tm0 = _row_tile(row_sets[0].head_major_seq or m0, tm_cap)
    in_specs, args = [], []
    w_mode = dict(pipeline_mode=pl.Buffered(1)) if single_buffer_weights else {}
    for w, kb, idx in weights:
        in_specs.append(pl.BlockSpec((kb, tn), lambda j, i, idx=idx: idx(j), **w_mode))
        args.append(w)
    for c, shape, idx in consts:
        in_specs.append(pl.BlockSpec(shape, lambda j, i, idx=idx: idx(j)))
        args.append(c)
    if norm_gain is not None:
        assert n_cols == tn and all(len(rows.lhs) == 1 for rows in row_sets)
        in_specs.append(pl.BlockSpec((1, norm_gain.shape[0]), lambda j, i: (0, 0)))
        args.append(norm_gain.reshape(1, -1))
    out_specs, out_shapes, sets, xn_specs, xn_shapes = [], [], [], [], []
    for s, rows in enumerate(row_sets):
        m = rows.lhs[0][0].shape[0]
        tm = tm0 if s == 0 else m
        row = (lambda i: i) if s == 0 else (lambda i: 0)
        for arr, kb, kidx in rows.lhs:
            in_specs.append(pl.BlockSpec((tm, kb), lambda j, i, row=row, kidx=kidx: (row(i), kidx(j))))
            args.append(arr)
        for arr, cidx in rows.extras:
            in_specs.append(pl.BlockSpec((tm, tn), lambda j, i, row=row, cidx=cidx: (row(i), cidx(j))))
            args.append(arr)
        if rows.head_major_seq:
            assert s == 0
            seq = rows.head_major_seq
            nt, hpt = seq // tm, tn // HEAD_DIM
            out_specs.append(pl.BlockSpec((None, hpt, HEAD_DIM, tm), lambda j, i, nt=nt: (i // nt, j, 0, i % nt)))
            out_shapes.append(jax.ShapeDtypeStruct((m // seq, n_cols // HEAD_DIM, HEAD_DIM, seq), out_dtype))
        else:
            out_specs.append(pl.BlockSpec((tm, tn), lambda j, i, row=row: (row(i), j)))
            out_shapes.append(jax.ShapeDtypeStruct((m, n_cols), out_dtype))
        sets.append((len(rows.lhs), len(rows.extras), bool(rows.head_major_seq)))
        if norm_gain is not None:
            kd = rows.lhs[0][0].shape[1]
            xn_specs.append(pl.BlockSpec((tm, kd), lambda j, i, row=row: (row(i), 0)))
            xn_shapes.append(jax.ShapeDtypeStruct((m, kd), BF16))
    scratch = [pltpu.VMEM((kb, tn), BF16) for _, kb, _ in weights]
    if pool_seq:
        assert n_cols == tn and norm_gain is None and pool_seq % tm0 == 0 and not row_sets[0].head_major_seq
        xn_specs.append(pl.BlockSpec((tm0, tn), lambda j, i: (i, j)))
        xn_shapes.append(jax.ShapeDtypeStruct((m0, n_cols), BF16))
        scratch.append(pltpu.VMEM((POOL_HALO + tm0, tn), F32))
    outs = pl.pallas_call(
        functools.partial(_ws_kernel, lhs_of_w=tuple(lhs_of_w), n_const=len(consts), sets=tuple(sets),
                          epilogue=epilogue, normalize=norm_gain is not None, pool_seq=pool_seq),
        grid=(n_cols // tn, m0 // tm0),
        in_specs=in_specs,
        out_specs=out_specs + xn_specs,
        out_shape=out_shapes + xn_shapes,
        scratch_shapes=scratch,
        compiler_params=pltpu.CompilerParams(
            dimension_semantics=("parallel", "arbitrary"),
            vmem_limit_bytes=LARGE_VMEM_LIMIT if single_buffer_weights else VMEM_LIMIT),
        name=name,
    )(*args)
    if pool_seq:
        return outs[:len(row_sets)], outs[len(row_sets)]
    return outs if norm_gain is None else (outs[:len(row_sets)], outs[len(row_sets):])


def _epi_plain(accs, extras, consts):
    return accs[0]


def _epi_headnorm(accs, extras, consts):
    h, (bd, gain) = accs[0], consts
    w = bd.shape[0]
    hh = (h * h).astype(BF16)
    ss = [jnp.dot(hh[:, c:c + w], bd, preferred_element_type=F32) for c in range(0, h.shape[1], w)]
    ss = ss[0] if len(ss) == 1 else jnp.concatenate(ss, axis=1)
    return (h * lax.rsqrt(ss * (1.0 / HEAD_DIM) + RMS_EPS)) * gain


def _epi_sigmoid(accs, extras, consts):
    return 1.0 / (1.0 + jnp.exp(-accs[0]))


def _epi_residual(accs, extras, consts):
    return extras[0] + accs[0]


def _epi_swiglu(accs, extras, consts):
    gate, up = accs
    return (gate / (1.0 + jnp.exp(-gate))) * up


def _cumsum_rhs(g):
    r = lax.broadcasted_iota(jnp.int32, (g, g + LANES), 0)
    c = lax.broadcasted_iota(jnp.int32, (g, g + LANES), 1)
    return jnp.where((r > c) | (c >= g), 1.0, 0.0).astype(BF16)


def _sb_weights(z, carry, u, mask):
    g = u.shape[0]
    neg_abs = lax.bitcast_convert_type(lax.bitcast_convert_type(z, jnp.uint32) | jnp.uint32(1 << 31), F32)
    sp = jnp.maximum(z, 0.0) + jnp.log(1.0 + jnp.exp2(neg_abs)) * LOG2E
    if mask is not None:
        sp = jnp.where(mask, sp, 0.0)
    log_beta = z - sp
    sp16 = sp.astype(BF16)
    parts = []
    for c in reversed(range(z.shape[1] // g)):
        cols = slice(c * g, (c + 1) * g)
        cum = jnp.dot(sp16[:, cols], u, preferred_element_type=F32)
        for l in reversed(range(0, g, LANES)):
            parts.append(jnp.exp2(log_beta[:, c * g + l:c * g + l + LANES] - cum[:, l:l + LANES] - carry))
        carry = carry + cum[:, g:]
    a = parts[0] if len(parts) == 1 else jnp.concatenate(parts[::-1], axis=1)
    if mask is not None:
        a = jnp.where(mask, a, 0.0)
    return carry, a.astype(BF16)


def _cumsum_lhs(g):
    r = lax.broadcasted_iota(jnp.int32, (g + 2 * SUBLANES, g), 0)
    c = lax.broadcasted_iota(jnp.int32, (g + 2 * SUBLANES, g), 1)
    return jnp.where((c > r) | (r >= g), 1.0, 0.0).astype(BF16)


def _softplus_t(z_ref, sp_ref, lb_ref, mask):
    ch = WEIGHTS_CHUNK
    for r0 in range(0, z_ref.shape[0], ch):
        rows = pl.ds(r0, ch)
        zt = z_ref[rows, :]
        neg_abs = lax.bitcast_convert_type(lax.bitcast_convert_type(zt, jnp.uint32) | jnp.uint32(1 << 31), F32)
        sp = jnp.maximum(zt, 0.0) + jnp.log(1.0 + jnp.exp2(neg_abs)) * LOG2E
        if mask is not None:
            sp = jnp.where(mask[r0:r0 + ch], sp, 0.0)
        sp_ref[rows, :] = sp.astype(BF16)
        lb_ref[rows, :] = zt - sp


def _stick_weights_t(lb_ref, cum_ref, carry, a_ref, mask):
    g, nq = lb_ref.shape
    ch = WEIGHTS_CHUNK
    for r0 in range(0, g, ch):
        rows = pl.ds(r0, ch)
        x = (lb_ref[rows, :] - cum_ref[rows, :]).reshape(ch // SUBLANES, SUBLANES, nq) - carry[None]
        a = jnp.exp2(x).reshape(ch, nq)
        if mask is not None:
            a = jnp.where(mask[r0:r0 + ch], a, 0.0)
        a_ref[rows, :] = a.astype(BF16)
    return carry + cum_ref[pl.ds(g, SUBLANES), :]


def _sb_block(z, carry, u, mask, vt):
    carry, a = _sb_weights(z, carry, u, mask)
    return carry, lax.dot_general(a, vt, NT_DIMS, preferred_element_type=F32)


class _DecodePlan(NamedTuple):
    n_batch: int
    t_new: int
    n_pages: int
    pages_per_group: int
    n_groups: int
    steps_per_batch: int
    slots_per_step: int
    groups_per_slot: int
    every_slot_full: bool


def _attn_kernel(pt_ref, bias_ref, qt_ref, kt_ref, vt_ref, qd_ref, biasd_ref, kn_ref, vn_ref, ck_hbm, cv_hbm,
                 o_ref, od_ref, ka_ref, qz_ref, vz_ref, lcum_ref, carry_ref, acc_ref, za_ref, zb_ref, aa_ref, ab_ref,
                 sp_ref, lb_ref, cum_ref, qs_ref, ks_ref, vs_ref, pk_ref, pv_ref, sem, kb_ref, vb_ref, dacc_ref, dcarry_ref, du_ref, *, seq, tile, plan):
    pair = pl.program_id(1)
    step = pl.program_id(0) * pl.num_programs(1) + pair
    kw = tile // 2
    nkb = seq // kw
    lcum_ref[...] = _cumsum_lhs(lcum_ref.shape[1])
    ka_ref[:, pl.ds(LANES, LANES)] = jnp.ones((seq, LANES), BF16)
    for cb in range(nkb):
        cols = pl.ds(cb * kw, kw)
        ka_ref[cols, pl.ds(0, LANES)] = kt_ref[:, :, cols].reshape(LANES, kw).T.astype(BF16)
    qz_ref[...] = jnp.zeros(qz_ref.shape, BF16)
    term_row = lax.broadcasted_iota(jnp.int32, (BIAS_ROWS, tile), 0)
    for h in range(HEADS_PER_BLOCK):
        rows = pl.ds(h * HEAD_DIM, HEAD_DIM)
        rest = jnp.full((BIAS_ROWS, tile), bias_ref[pair * HEADS_PER_BLOCK + h], F32)
        terms = jnp.zeros((BIAS_ROWS, tile), F32)
        for i in range(BIAS_TERMS):
            term = rest.astype(BF16).astype(F32)
            terms = jnp.where(term_row == i, term, terms)
            rest = rest - term
        for qb in range(seq // tile):
            cols = pl.ds(qb * tile, tile)
            qz_ref[h, qb, rows, :] = qt_ref[h, :, cols]
            qz_ref[h, qb, pl.ds(LANES, BIAS_ROWS), :] = terms.astype(BF16)
        for cb in range(nkb):
            vz_ref[h, cb] = vt_ref[h, :, pl.ds(cb * kw, kw)].astype(BF16)

    rk = lax.broadcasted_iota(jnp.int32, (tile, tile), 0)
    cq = lax.broadcasted_iota(jnp.int32, (tile, tile), 1)
    diag_mask = rk < cq

    heads = range(HEADS_PER_BLOCK)

    rows_d, slots = qd_ref.shape[0], kn_ref.shape[1]
    n_all_groups = plan.n_batch * plan.n_groups
    step_in_batch = step % plan.steps_per_batch
    batch_d = step // plan.steps_per_batch

    def group_copies(grp):
        bd, n, buf = grp // plan.n_groups, grp % plan.n_groups, grp % 2
        copies = []
        for r in range(plan.pages_per_group):
            page = pt_ref[bd, plan.n_pages - 1 - n * plan.pages_per_group - r]
            copies.append(pltpu.make_async_copy(ck_hbm.at[page], pk_ref.at[buf, r], sem.at[buf, 0, r]))
            copies.append(pltpu.make_async_copy(cv_hbm.at[page], pv_ref.at[buf, r], sem.at[buf, 1, r]))
        return copies

    def start_group(grp):
        for c in group_copies(grp):
            c.start()

    def decode_group(grp):
        buf = grp % 2
        for c in group_copies(grp):
            c.wait()

        @pl.when(grp + 1 < n_all_groups)
        def _():
            start_group(grp + 1)

        for r in range(plan.pages_per_group):
            cols = pl.ds((plan.pages_per_group - 1 - r) * slots, slots)
            kb_ref[:, cols] = pk_ref[buf, r].astype(BF16)
            vb_ref[:, cols] = pv_ref[buf, r].astype(BF16)
        s = jnp.dot(qd_ref[...], kb_ref[...], preferred_element_type=F32)
        dcarry_ref[...], pv = _sb_block(s + biasd_ref[...], dcarry_ref[...], du_ref[...], None, vb_ref[...])
        dacc_ref[...] += pv

    def decode_new_tokens():
        du_ref[...] = _cumsum_rhs(slots)
        r = lax.broadcasted_iota(jnp.int32, (rows_d, slots), 0)
        c = lax.broadcasted_iota(jnp.int32, (rows_d, slots), 1)
        s = jnp.dot(qd_ref[...], kn_ref[...].astype(BF16), preferred_element_type=F32)
        dcarry_ref[...], dacc_ref[...] = _sb_block(
            s + biasd_ref[:, pl.ds(0, slots)], jnp.zeros((rows_d, LANES), F32), du_ref[...], c < (r // N_HEADS),
            vn_ref[...].astype(BF16))

    def decode_finish():
        r = lax.broadcasted_iota(jnp.int32, dacc_ref.shape, 0)
        l = lax.broadcasted_iota(jnp.int32, dacc_ref.shape, 1)
        own = jnp.where((r % N_HEADS) == (l // HEAD_DIM), dacc_ref[...], 0.0)
        for t in range(plan.t_new):
            od_ref[pl.ds(t, 1), :] = jnp.sum(own[t * N_HEADS:(t + 1) * N_HEADS], axis=0, keepdims=True)

    def decode_slot(qb):
        slot = step_in_batch * plan.slots_per_step + qb

        @pl.when((step == 0) & (qb == 0))
        def _():
            start_group(0)

        pl.when(slot == 0)(decode_new_tokens)
        for t in range(plan.groups_per_slot):
            n = slot * plan.groups_per_slot + t
            grp = batch_d * plan.n_groups + n
            if plan.every_slot_full:
                decode_group(grp)
            else:
                pl.when(n < plan.n_groups)(functools.partial(decode_group, grp))
        if plan.every_slot_full:
            decode_finish()
        else:
            pl.when(slot == plan.steps_per_batch * plan.slots_per_step - 1)(decode_finish)

    def qbody(qb, _):
        decode_slot(qb)
        qoff = pl.multiple_of(qb * tile, tile)
        k0 = 2 * qb

        for h in heads:
            qs_ref[h] = qz_ref[h, qb]

        def stage_keys(slot, kb):
            ks_ref[slot] = ka_ref[pl.ds(pl.multiple_of(kb * kw, kw), kw), :]

        def stage_values(slot, kb):
            for h in heads:
                vs_ref[slot, h] = vz_ref[h, kb]

        def logits(slot, z_ref):
            for h in heads:
                z_ref[h] = jnp.dot(ks_ref[slot], qs_ref[h], preferred_element_type=F32)

        def softplus(z_ref, half, mask=None):
            for h in heads:
                _softplus_t(z_ref.at[h], sp_ref.at[half, h], lb_ref.at[half, h], mask)

        def cumsums(half):
            for h in heads:
                cum_ref[half, h] = jnp.dot(lcum_ref[...], sp_ref[half, h], preferred_element_type=F32)

        def stick(half, a_ref, mask=None, first=False):
            for h in heads:
                carry = jnp.zeros((SUBLANES, tile), F32) if first else carry_ref[h]
                carry_ref[h] = _stick_weights_t(lb_ref.at[half, h], cum_ref.at[half, h], carry, a_ref.at[h], mask)

        def values(a_ref, slot, first=False):
            for h in heads:
                pv = jnp.dot(vs_ref[slot, h], a_ref[h], preferred_element_type=F32)
                acc_ref[h] = pv if first else acc_ref[h] + pv


        def weights_and_next_logits(mask_a=None, mask_b=None, first=False):
            softplus(za_ref, 0, mask_a)
            cumsums(0)
            logits(0, za_ref)
            softplus(zb_ref, 1, mask_b)
            logits(1, zb_ref)
            cumsums(1)
            stick(0, aa_ref, mask_a, first)
            stick(1, ab_ref, mask_b)

        acc_ref[...] = jnp.zeros(acc_ref.shape, F32)
        stage_keys(2, k0 + 1)
        stage_keys(3, k0)
        stage_keys(0, jnp.maximum(k0 - 1, 0))
        stage_keys(1, jnp.maximum(k0 - 2, 0))
        logits(2, za_ref)
        logits(3, zb_ref)
        weights_and_next_logits(diag_mask[kw:], diag_mask[:kw], first=True)

        def kbody(i, _):
            ka = k0 - 1 - 2 * i
            stage_keys(0, jnp.maximum(ka - 2, 0))
            stage_keys(1, jnp.maximum(ka - 3, 0))
            stage_values(0, ka + 2)
            stage_values(1, ka + 1)
            values(aa_ref, 0)
            values(ab_ref, 1)
            weights_and_next_logits()
            return 0

        lax.fori_loop(0, qb, kbody, 0)
        stage_values(0, 1)
        stage_values(1, 0)
        values(aa_ref, 0)
        values(ab_ref, 1)
        out = jnp.concatenate([acc_ref[h] for h in heads], axis=0)
        o_ref[pl.ds(qoff, tile), :] = out.T.astype(o_ref.dtype)
        return 0

    lax.fori_loop(0, seq // tile, qbody, 0)


def _attention(qt, kt, vt, bias, q_rows, row_bias, kt_new, vt_new, cache_kt, cache_vt, page_table, t_new):
    batch, n_heads, _, seq = qt.shape
    m, w = batch * seq, n_heads * HEAD_DIM
    npairs = w // LANES
    tile = _row_tile(seq, ATTN_TILE)
    kw = tile // 2
    assert kw <= CUMSUM_GROUP
    group = kw
    nb, rows_d, _ = q_rows.shape
    n_pages, slots = page_table.shape[1], cache_kt.shape[2]
    ppg = max(p for p in (DECODE_PAGES_PER_STEP, 4, 2, 1) if n_pages % p == 0)
    n_steps = batch * npairs
    assert n_steps % nb == 0, (n_steps, nb)
    n_slots = (n_steps // nb) * (seq // tile)
    n_groups = n_pages // ppg
    gps = -(-n_groups // n_slots)
    plan = _DecodePlan(n_batch=nb, t_new=t_new, n_pages=n_pages, pages_per_group=ppg, n_groups=n_groups,
                       steps_per_batch=n_steps // nb, slots_per_step=seq // tile, groups_per_slot=gps,
                       every_slot_full=gps * n_slots == n_groups)
    bias_rows = jnp.broadcast_to(row_bias[:, None], (rows_d, ppg * slots))
    spb = plan.steps_per_batch
    kv_spec = pl.BlockSpec((None, HEADS_PER_BLOCK, HEAD_DIM, seq), lambda b, p, pt: (b, p, 0, 0))
    per_d = lambda b, p, pt: ((b * npairs + p) // spb, 0, 0)
    grid_spec = pltpu.PrefetchScalarGridSpec(
        num_scalar_prefetch=1,
        grid=(batch, npairs),
        in_specs=[
            pl.BlockSpec(memory_space=pltpu.SMEM),
            kv_spec,
            kv_spec,
            kv_spec,
            pl.BlockSpec((None, rows_d, w), per_d),
            pl.BlockSpec((rows_d, ppg * slots), lambda b, p, pt: (0, 0)),
            pl.BlockSpec((None, w, slots), per_d),
            pl.BlockSpec((None, w, slots), per_d),
            pl.BlockSpec(memory_space=pl.ANY),
            pl.BlockSpec(memory_space=pl.ANY),
        ],
        out_specs=[pl.BlockSpec((seq, LANES), lambda b, p, pt: (b, p)),
                   pl.BlockSpec((None, t_new, w), per_d)],
        scratch_shapes=[
            pltpu.VMEM((seq, 2 * LANES), BF16),
            pltpu.VMEM((HEADS_PER_BLOCK, seq // tile, 2 * LANES, tile), BF16),
            pltpu.VMEM((HEADS_PER_BLOCK, seq // kw, HEAD_DIM, kw), BF16),
            pltpu.VMEM((group + 2 * SUBLANES, group), BF16),
            pltpu.VMEM((HEADS_PER_BLOCK, SUBLANES, tile), F32),
            pltpu.VMEM((HEADS_PER_BLOCK, HEAD_DIM, tile), F32),
            pltpu.VMEM((HEADS_PER_BLOCK, kw, tile), F32),
            pltpu.VMEM((HEADS_PER_BLOCK, kw, tile), F32),
            pltpu.VMEM((HEADS_PER_BLOCK, kw, tile), BF16),
            pltpu.VMEM((HEADS_PER_BLOCK, kw, tile), BF16),
            pltpu.VMEM((2, HEADS_PER_BLOCK, kw, tile), BF16),
            pltpu.VMEM((2, HEADS_PER_BLOCK, kw, tile), F32),
            pltpu.VMEM((2, HEADS_PER_BLOCK, kw + 2 * SUBLANES, tile), F32),
            pltpu.VMEM((HEADS_PER_BLOCK, 2 * LANES, tile), BF16),
            pltpu.VMEM((4, kw, 2 * LANES), BF16),
            pltpu.VMEM((2, HEADS_PER_BLOCK, HEAD_DIM, kw), BF16),
            pltpu.VMEM((2, ppg, w, slots), F32),
            pltpu.VMEM((2, ppg, w, slots), F32),
            pltpu.SemaphoreType.DMA((2, 2, ppg)),
            pltpu.VMEM((w, ppg * slots), BF16),
            pltpu.VMEM((w, ppg * slots), BF16),
            pltpu.VMEM((rows_d, w), F32),
            pltpu.VMEM((rows_d, LANES), F32),
            pltpu.VMEM((slots, slots + LANES), BF16),
        ],
    )
    return pl.pallas_call(
        functools.partial(_attn_kernel, seq=seq, tile=tile, plan=plan),
        grid_spec=grid_spec,
        out_shape=[jax.ShapeDtypeStruct((m, w), BF16), jax.ShapeDtypeStruct((nb, t_new, w), F32)],
        compiler_params=pltpu.CompilerParams(dimension_semantics=("arbitrary", "arbitrary"),
                                             vmem_limit_bytes=LARGE_VMEM_LIMIT),
        name="attention",
    )(page_table, bias, qt, kt, vt, q_rows, bias_rows, kt_new, vt_new, cache_kt, cache_vt)


def _pool_tile(ext_ref, p_ref, tp, pos0):
    gc = ext_ref.shape[1] // POOL_GROUPS
    pos = pos0 + lax.broadcasted_iota(jnp.int32, (tp, 1), 0)
    for g, win in enumerate(POOL_WINDOWS):
        cols = slice(g * gc, (g + 1) * gc)
        u_new = ext_ref[pl.ds(POOL_HALO, tp), cols]
        win_sum = u_new
        for d in range(1, win):
            win_sum = win_sum + ext_ref[pl.ds(POOL_HALO - d, tp), cols]
        cnt = jnp.minimum(win, pos + 1).astype(F32)
        p_ref[:, cols] = (win_sum / cnt - u_new).astype(p_ref.dtype)


def _pool_sample_kernel(u_ref, halo_ref, p_ref, ext_ref, *, tp, pos0):
    ext_ref[pl.ds(0, POOL_HALO), :] = halo_ref[...]
    ext_ref[pl.ds(POOL_HALO, tp), :] = u_ref[...]
    _pool_tile(ext_ref, p_ref, tp, pos0)


def _pool_sample(u3, halo3, pos0):
    nb, tp, w = u3.shape
    return pl.pallas_call(
        functools.partial(_pool_sample_kernel, tp=tp, pos0=pos0),
        grid=(nb,),
        in_specs=[
            pl.BlockSpec((None, tp, w), lambda b: (b, 0, 0)),
            pl.BlockSpec((None, POOL_HALO, w), lambda b: (b, 0, 0)),
        ],
        out_specs=pl.BlockSpec((None, tp, w), lambda b: (b, 0, 0)),
        out_shape=jax.ShapeDtypeStruct((nb, tp, w), F32),
        scratch_shapes=[pltpu.VMEM((POOL_HALO + tp, w), F32)],
        compiler_params=_cparams("parallel"),
        name="pool_sample",
    )(u3, halo3)


def _inproj(xns, w_in, col0, ncols, *, mode, out_dtype, tn, head_gain=None, seq=0, norm_gain=None, tm_cap=1024,
            pool_seq=0, single_buffer_weights=False):
    d = w_in.shape[0]
    jb = col0 // tn
    assert col0 % tn == 0
    consts, epilogue = (), {"plain": _epi_plain, "sigmoid": _epi_sigmoid, "headnorm": _epi_headnorm}[mode]
    if mode == "headnorm":
        hid = jnp.arange(HEADNORM_GROUP, dtype=jnp.int32) // HEAD_DIM
        bd = (hid[:, None] == hid[None, :]).astype(BF16)
        gain = jnp.tile(head_gain.astype(F32), tn // HEAD_DIM).reshape(1, tn)
        consts = ((bd, bd.shape, lambda j: (0, 0)), (gain, (1, tn), lambda j: (0, 0)))
    row_sets = [_Rows(lhs=((xn, d, lambda j: 0),), head_major_seq=seq if s == 0 else 0)
                for s, xn in enumerate(xns)]
    return _ws_matmul("inproj_" + mode + ("_t" if seq else ""), ((w_in, d, lambda j: (0, j + jb)),), (0,),
                      consts, row_sets, epilogue, ncols, tn, out_dtype, tm_cap=tm_cap, norm_gain=norm_gain,
                      pool_seq=pool_seq, single_buffer_weights=single_buffer_weights)


def _project(xs, norm_g, wts, seq):
    w_in = wts["w_in"]
    d = w_in.shape[0]
    aw, pw = ATTN_WIDTH, d // 2
    tn = INPROJ_COLS
    assert aw == tn
    q, xns = _inproj(xs, w_in, 0, aw, mode="headnorm", out_dtype=BF16, tn=tn,
                     head_gain=wts["q_norm_g"] * (HEAD_DIM ** -0.5 * LOG2E), seq=seq, norm_gain=norm_g,
                     tm_cap=512)
    k = _inproj(xns, w_in, aw, aw, mode="headnorm", out_dtype=F32, tn=tn, head_gain=wts["k_norm_g"], seq=seq)
    v = _inproj(xns, w_in, 2 * aw, aw, mode="plain", out_dtype=F32, tn=tn, seq=seq)
    assert pw == tn
    u, p_prompt = _inproj(xns, w_in, 3 * aw, pw, mode="plain", out_dtype=F32, tn=tn, pool_seq=seq)
    gates = _inproj(xns, w_in, 3 * aw + pw, 2 * d, mode="sigmoid", out_dtype=BF16, tn=2 * tn,
                    single_buffer_weights=True)
    return q, k, v, u, gates, p_prompt


def _mix_kernel(o_ref, p_ref, ga_ref, gb_ref, x_ref, wap_ref, wp_ref, ps_ref, wo_ref, g2_ref, h_ref, hn_ref):
    a = jnp.dot(o_ref[...], wap_ref[...], preferred_element_type=F32)
    groups, gc, _ = wp_ref.shape
    op = jnp.concatenate(
        [jnp.dot(p_ref[:, g * gc:(g + 1) * gc], wp_ref[g], preferred_element_type=F32) for g in range(groups)],
        axis=1)
    merged = ga_ref[...].astype(F32) * a + gb_ref[...].astype(F32) * (op * ps_ref[...])
    h = x_ref[...] + jnp.dot(merged.astype(BF16), wo_ref[...], preferred_element_type=F32)
    h_ref[...] = h
    ms = jnp.mean(h * h, axis=-1, keepdims=True)
    hn_ref[...] = ((h * lax.rsqrt(ms + RMS_EPS)) * g2_ref[...]).astype(hn_ref.dtype)


def _mix(x, o_attn, p, gates, w_ap, w_pool, pool_scale, w_out, norm2_g, tm_cap=512):
    m, d = x.shape
    aw, pw = o_attn.shape[1], p.shape[1]
    tm = _row_tile(m, tm_cap)
    whole = lambda a: pl.BlockSpec(a.shape, lambda i: (0,) * a.ndim, pipeline_mode=pl.Buffered(1))
    ps2, g2 = pool_scale.reshape(1, d), norm2_g.reshape(1, d)
    return pl.pallas_call(
        _mix_kernel,
        grid=(m // tm,),
        in_specs=[
            pl.BlockSpec((tm, aw), lambda i: (i, 0)),
            pl.BlockSpec((tm, pw), lambda i: (i, 0)),
            pl.BlockSpec((tm, d), lambda i: (i, 0)),
            pl.BlockSpec((tm, d), lambda i: (i, 1)),
            pl.BlockSpec((tm, d), lambda i: (i, 0)),
            whole(w_ap), whole(w_pool), whole(ps2), whole(w_out), whole(g2),
        ],
        out_specs=[pl.BlockSpec((tm, d), lambda i: (i, 0)), pl.BlockSpec((tm, d), lambda i: (i, 0))],
        out_shape=[jax.ShapeDtypeStruct((m, d), F32), jax.ShapeDtypeStruct((m, d), BF16)],
        compiler_params=pltpu.CompilerParams(dimension_semantics=("parallel",), vmem_limit_bytes=LARGE_VMEM_LIMIT),
        name="mix",
    )(o_attn, p, gates, gates, x, w_ap, w_pool, ps2, w_out, g2)


def _mix_and_ffn(xs, o_attns, ps, gates, wts):
    w_gu, w_down = wts["w_gate_up"], wts["w_down"]
    dff, d = w_down.shape
    w_ap, w_pool, w_out = (wts[n].astype(BF16) for n in ("w_attn_proj", "w_pool", "w_out"))
    hs, hns = zip(*[_mix(x, o, p, gt, w_ap, w_pool, wts["pool_scale"], w_out, wts["norm2_g"])
                    for x, o, p, gt in zip(xs, o_attns, ps, gates)])
    nj = dff // 512
    acts = _ws_matmul(
        "gate_up", ((w_gu, d, lambda j: (0, j)), (w_gu, d, lambda j: (0, j + nj))), (0, 0), (),
        [_Rows(lhs=((hn, d, lambda j: 0),)) for hn in hns],
        _epi_swiglu, dff, 512, BF16, tm_cap=1024)
    return _ws_matmul(
        "down_proj", ((w_down, dff, lambda j: (0, j)),), (0,), (),
        [_Rows(lhs=((act, dff, lambda j: 0),), extras=((h, lambda j: j),)) for act, h in zip(acts, hs)],
        _epi_residual, d, 512, F32, tm_cap=1024, single_buffer_weights=True)


def kernel(x_prompt, x_sample, cache_k, cache_v, state_pool, page_table, norm1_g, w_in,
           q_norm_g, k_norm_g, sb_bias, w_attn_proj, w_pool, pool_scale, w_out, norm2_g, w_gate_up, w_down):
    b_p, seq, d = x_prompt.shape
    b_s, t_s = x_sample.shape[:2]
    depth, n_pool, page = cache_k.shape[:3]
    n_pages = page_table.shape[1]
    past_len = n_pages * page
    pw = d // 2

    xp = x_prompt.reshape(b_p * seq, d)
    xs = x_sample.reshape(b_s * t_s, d)
    outs = {name: [] for name in ("kp", "vp", "up", "ks", "vs", "us")}
    for l in range(depth):
        wts = {
            "w_in": w_in[l], "q_norm_g": q_norm_g[l], "k_norm_g": k_norm_g[l], "w_attn_proj": w_attn_proj[l],
            "w_pool": w_pool[l], "pool_scale": pool_scale[l], "w_out": w_out[l], "norm2_g": norm2_g[l],
            "w_gate_up": w_gate_up[l], "w_down": w_down[l],
        }
        bias = sb_bias[l].astype(F32) * LOG2E

        (qt_p, q_s), (kt_p, k_s), (vt_p, v_s), (u_p, u_s), (gates_p, gates_s), p_p = _project(
            [xp, xs], norm1_g[l], wts, seq)

        q4 = q_s.reshape(b_s, t_s, 1, N_HEADS, HEAD_DIM)
        eye = jnp.eye(N_HEADS, dtype=BF16).reshape(1, 1, N_HEADS, N_HEADS, 1)
        q_rows = (q4 * eye).reshape(b_s, t_s * N_HEADS, ATTN_WIDTH)
        pad = ((0, 0), (0, 0), (0, page - t_s))
        kt_new = jnp.pad(k_s.reshape(b_s, t_s, ATTN_WIDTH).transpose(0, 2, 1), pad)
        vt_new = jnp.pad(v_s.reshape(b_s, t_s, ATTN_WIDTH).transpose(0, 2, 1), pad)
        cache_kt = cache_k[l].transpose(0, 2, 3, 1).reshape(n_pool, ATTN_WIDTH, page)
        cache_vt = cache_v[l].transpose(0, 2, 3, 1).reshape(n_pool, ATTN_WIDTH, page)
        o_attn_p, o_attn_s = _attention(qt_p, kt_p, vt_p, bias, q_rows, jnp.tile(bias, t_s), kt_new, vt_new,
                                        cache_kt, cache_vt, page_table, t_s)
        o_attn_s = o_attn_s.reshape(b_s * t_s, ATTN_WIDTH).astype(BF16)
        outs["kp"].append(kt_p.transpose(0, 3, 1, 2))
        outs["vp"].append(vt_p.transpose(0, 3, 1, 2))

        outs["up"].append(u_p.reshape(b_p, seq, pw)[:, seq - POOL_STATE:])
        u3 = u_s.reshape(b_s, t_s, pw)
        halo = jnp.pad(state_pool[l], ((0, 0), (POOL_HALO - POOL_STATE, 0), (0, 0)))
        p_s = _pool_sample(u3, halo, past_len).reshape(b_s * t_s, pw).astype(BF16)
        outs["ks"].append(k_s.reshape(b_s, t_s, N_HEADS, HEAD_DIM))
        outs["vs"].append(v_s.reshape(b_s, t_s, N_HEADS, HEAD_DIM))
        outs["us"].append(jnp.concatenate([state_pool[l], u3], axis=1)[:, -POOL_STATE:])

        xp, xs = _mix_and_ffn([xp, xs], [o_attn_p, o_attn_s], [p_p, p_s], [gates_p, gates_s], wts)

    st = lambda name: jnp.stack(outs[name], axis=0)
    return (xp.reshape(b_p, seq, d), xs.reshape(b_s, t_s, d),
            st("kp"), st("vp"), st("up"), st("ks"), st("vs"), st("us"))
```

```python
import functools
from typing import NamedTuple

import jax
import jax.numpy as jnp
from jax import lax
from jax.experimental import pallas as pl
from jax.experimental.pallas import tpu as pltpu

F32 = jnp.float32
BF16 = jnp.bfloat16

N_HEADS = 16
HEAD_DIM = 64
ATTN_WIDTH = N_HEADS * HEAD_DIM
POOL_WINDOWS = (2, 4, 8, 16)
POOL_GROUPS = len(POOL_WINDOWS)
POOL_STATE = max(POOL_WINDOWS) - 1
POOL_HALO = 16
RMS_EPS = 1e-6

LANES = 128
SUBLANES = 8
HEADS_PER_BLOCK = LANES // HEAD_DIM
VMEM_LIMIT = 52 * 1024 * 1024
LARGE_VMEM_LIMIT = 57 * 1024 * 1024
ATTN_TILE = 512
CUMSUM_GROUP = 256
BIAS_TERMS = 3
BIAS_ROWS = 16
WEIGHTS_CHUNK = 32
DECODE_PAGES_PER_STEP = 8
INPROJ_COLS = 1024
HEADNORM_GROUP = 256
LOG2E = 1.4426950408889634

NT_DIMS = (((1,), (1,)), ((), ()))


def _cparams(*sem):
    return pltpu.CompilerParams(dimension_semantics=sem, vmem_limit_bytes=VMEM_LIMIT)


def _row_tile(m, cap):
    t = min(m, cap)
    while m % t:
        t //= 2
    return t


class _Rows(NamedTuple):
    lhs: tuple
    extras: tuple = ()
    head_major_seq: int = 0


def _ws_kernel(*refs, lhs_of_w, n_const, sets, epilogue, normalize, pool_seq):
    n_w = len(lhs_of_w)
    w_refs, refs = refs[:n_w], refs[n_w:]
    const_refs, refs = refs[:n_const], refs[n_const:]
    if normalize:
        gain_ref, refs = refs[0], refs[1:]
    set_refs = []
    for n_lhs, n_extra, _ in sets:
        set_refs.append((refs[:n_lhs], refs[n_lhs:n_lhs + n_extra]))
        refs = refs[n_lhs + n_extra:]
    out_refs, refs = refs[:len(sets)], refs[len(sets):]
    xn_refs, refs = (refs[:len(sets)], refs[len(sets):]) if normalize else ([None] * len(sets), refs)
    if pool_seq:
        p_ref, refs, ext_ref = refs[0], refs[1:-1], refs[-1]
    wb_refs = refs

    def pool(r):
        tm = r.shape[0]
        tile_in_seq = pl.program_id(1) % (pool_seq // tm)
        last_rows = ext_ref[pl.ds(tm, POOL_HALO), :]
        ext_ref[pl.ds(POOL_HALO, tm), :] = r
        ext_ref[pl.ds(0, POOL_HALO), :] = jnp.where(tile_in_seq == 0, 0.0, last_rows)
        _pool_tile(ext_ref, p_ref, tm, tile_in_seq * tm)

    def compute(lhs_refs, extra_refs, o_ref, head_major, xn_ref, pooled=False):
        lhs = [r[...] for r in lhs_refs]
        if normalize:
            x = lhs[0]
            ms = jnp.mean(x * x, axis=-1, keepdims=True)
            lhs = [((x * lax.rsqrt(ms + RMS_EPS)) * gain_ref[...]).astype(BF16)]
            xn_ref[...] = lhs[0]
        accs = [jnp.dot(lhs[lhs_of_w[k]], wb_refs[k][...], preferred_element_type=F32) for k in range(n_w)]
        r = epilogue(accs, [e[...] for e in extra_refs], [c[...] for c in const_refs])
        if pooled:
            pool(r)
        if head_major:
            r = r.T.reshape(o_ref.shape)
        o_ref[...] = r.astype(o_ref.dtype)

    @pl.when(pl.program_id(1) == 0)
    def _():
        for w_ref, wb_ref in zip(w_refs, wb_refs):
            wb_ref[...] = w_ref[...].astype(BF16)
        for s in range(1, len(sets)):
            compute(*set_refs[s], out_refs[s], sets[s][2], xn_refs[s])

    compute(*set_refs[0], out_refs[0], sets[0][2], xn_refs[0], pooled=bool(pool_seq))


def _ws_matmul(name, weights, lhs_of_w, consts, row_sets, epilogue, n_cols, tn, out_dtype, tm_cap,
               norm_gain=None, pool_seq=0):
    assert n_cols % tn == 0
    m0 = row_sets[0].lhs[0][0].shape[0]
    tm0 = _row_tile(row_sets[0].head_major_seq or m0, tm_cap)
    in_specs, args = [], []
    for w, kb, idx in weights:
        in_specs.append(pl.BlockSpec((kb, tn), lambda j, i, idx=idx: idx(j)))
        args.append(w)
    for c, shape, idx in consts:
        in_specs.append(pl.BlockSpec(shape, lambda j, i, idx=idx: idx(j)))
        args.append(c)
    if norm_gain is not None:
        assert n_cols == tn and all(len(rows.lhs) == 1 for rows in row_sets)
        in_specs.append(pl.BlockSpec((1, norm_gain.shape[0]), lambda j, i: (0, 0)))
        args.append(norm_gain.reshape(1, -1))
    out_specs, out_shapes, sets, xn_specs, xn_shapes = [], [], [], [], []
    for s, rows in enumerate(row_sets):
        m = rows.lhs[0][0].shape[0]
        tm = tm0 if s == 0 else m
        row = (lambda i: i) if s == 0 else (lambda i: 0)
        for arr, kb, kidx in rows.lhs:
            in_specs.append(pl.BlockSpec((tm, kb), lambda j, i, row=row, kidx=kidx: (row(i), kidx(j))))
            args.append(arr)
        for arr, cidx in rows.extras:
            in_specs.append(pl.BlockSpec((tm, tn), lambda j, i, row=row, cidx=cidx: (row(i), cidx(j))))
            args.append(arr)
        if rows.head_major_seq:
            assert s == 0
            seq = rows.head_major_seq
            nt, hpt = seq // tm, tn // HEAD_DIM
            out_specs.append(pl.BlockSpec((None, hpt, HEAD_DIM, tm), lambda j, i, nt=nt: (i // nt, j, 0, i % nt)))
            out_shapes.append(jax.ShapeDtypeStruct((m // seq, n_cols // HEAD_DIM, HEAD_DIM, seq), out_dtype))
        else:
            out_specs.append(pl.BlockSpec((tm, tn), lambda j, i, row=row: (row(i), j)))
            out_shapes.append(jax.ShapeDtypeStruct((m, n_cols), out_dtype))
        sets.append((len(rows.lhs), len(rows.extras), bool(rows.head_major_seq)))
        if norm_gain is not None:
            kd = rows.lhs[0][0].shape[1]
            xn_specs.append(pl.BlockSpec((tm, kd), lambda j, i, row=row: (row(i), 0)))
            xn_shapes.append(jax.ShapeDtypeStruct((m, kd), BF16))
    scratch = [pltpu.VMEM((kb, tn), BF16) for _, kb, _ in weights]
    if pool_seq:
        assert n_cols == tn and norm_gain is None and pool_seq % tm0 == 0 and not row_sets[0].head_major_seq
        xn_specs.append(pl.BlockSpec((tm0, tn), lambda j, i: (i, j)))
        xn_shapes.append(jax.ShapeDtypeStruct((m0, n_cols), BF16))
        scratch.append(pltpu.VMEM((POOL_HALO + tm0, tn), F32))
    outs = pl.pallas_call(
        functools.partial(_ws_kernel, lhs_of_w=tuple(lhs_of_w), n_const=len(consts), sets=tuple(sets),
                          epilogue=epilogue, normalize=norm_gain is not None, pool_seq=pool_seq),
        grid=(n_cols // tn, m0 // tm0),
        in_specs=in_specs,
        out_specs=out_specs + xn_specs,
        out_shape=out_shapes + xn_shapes,
        scratch_shapes=scratch,
        compiler_params=_cparams("parallel", "arbitrary"),
        name=name,
    )(*args)
    if pool_seq:
        return outs[:len(row_sets)], outs[len(row_sets)]
    return outs if norm_gain is None else (outs[:len(row_sets)], outs[len(row_sets):])


def _epi_plain(accs, extras, consts):
    return accs[0]


def _epi_headnorm(accs, extras, consts):
    h, (bd, gain) = accs[0], consts
    w = bd.shape[0]
    hh = (h * h).astype(BF16)
    ss = [jnp.dot(hh[:, c:c + w], bd, preferred_element_type=F32) for c in range(0, h.shape[1], w)]
    ss = ss[0] if len(ss) == 1 else jnp.concatenate(ss, axis=1)
    return (h * lax.rsqrt(ss * (1.0 / HEAD_DIM) + RMS_EPS)) * gain


def _epi_sigmoid(accs, extras, consts):
    return 1.0 / (1.0 + jnp.exp(-accs[0]))


def _epi_residual(accs, extras, consts):
    return extras[0] + accs[0]


def _epi_swiglu(accs, extras, consts):
    gate, up = accs
    return (gate / (1.0 + jnp.exp(-gate))) * up


def _cumsum_rhs(g):
    r = lax.broadcasted_iota(jnp.int32, (g, g + LANES), 0)
    c = lax.broadcasted_iota(jnp.int32, (g, g + LANES), 1)
    return jnp.where((r > c) | (c >= g), 1.0, 0.0).astype(BF16)


def _sb_weights(z, carry, u, mask):
    g = u.shape[0]
    neg_abs = lax.bitcast_convert_type(lax.bitcast_convert_type(z, jnp.uint32) | jnp.uint32(1 << 31), F32)
    sp = jnp.maximum(z, 0.0) + jnp.log(1.0 + jnp.exp2(neg_abs)) * LOG2E
    if mask is not None:
        sp = jnp.where(mask, sp, 0.0)
    log_beta = z - sp
    sp16 = sp.astype(BF16)
    parts = []
    for c in reversed(range(z.shape[1] // g)):
        cols = slice(c * g, (c + 1) * g)
        cum = jnp.dot(sp16[:, cols], u, preferred_element_type=F32)
        for l in reversed(range(0, g, LANES)):
            parts.append(jnp.exp2(log_beta[:, c * g + l:c * g + l + LANES] - cum[:, l:l + LANES] - carry))
        carry = carry + cum[:, g:]
    a = parts[0] if len(parts) == 1 else jnp.concatenate(parts[::-1], axis=1)
    if mask is not None:
        a = jnp.where(mask, a, 0.0)
    return carry, a.astype(BF16)


def _cumsum_lhs(g):
    r = lax.broadcasted_iota(jnp.int32, (g + 2 * SUBLANES, g), 0)
    c = lax.broadcasted_iota(jnp.int32, (g + 2 * SUBLANES, g), 1)
    return jnp.where((c > r) | (r >= g), 1.0, 0.0).astype(BF16)


def _softplus_t(z_ref, sp_ref, lb_ref, mask):
    ch = WEIGHTS_CHUNK
    for r0 in range(0, z_ref.shape[0], ch):
        rows = pl.ds(r0, ch)
        zt = z_ref[rows, :]
        neg_abs = lax.bitcast_convert_type(lax.bitcast_convert_type(zt, jnp.uint32) | jnp.uint32(1 << 31), F32)
        sp = jnp.maximum(zt, 0.0) + jnp.log(1.0 + jnp.exp2(neg_abs)) * LOG2E
        if mask is not None:
            sp = jnp.where(mask[r0:r0 + ch], sp, 0.0)
        sp_ref[rows, :] = sp.astype(BF16)
        lb_ref[rows, :] = zt - sp


def _stick_weights_t(lb_ref, cum_ref, carry, a_ref, mask):
    g, nq = lb_ref.shape
    ch = WEIGHTS_CHUNK
    for r0 in range(0, g, ch):
        rows = pl.ds(r0, ch)
        x = (lb_ref[rows, :] - cum_ref[rows, :]).reshape(ch // SUBLANES, SUBLANES, nq) - carry[None]
        a = jnp.exp2(x).reshape(ch, nq)
        if mask is not None:
            a = jnp.where(mask[r0:r0 + ch], a, 0.0)
        a_ref[rows, :] = a.astype(BF16)
    return carry + cum_ref[pl.ds(g, SUBLANES), :]


def _sb_block(z, carry, u, mask, vt):
    carry, a = _sb_weights(z, carry, u, mask)
    return carry, lax.dot_general(a, vt, NT_DIMS, preferred_element_type=F32)


class _DecodePlan(NamedTuple):
    n_batch: int
    t_new: int
    n_pages: int
    pages_per_group: int
    n_groups: int
    steps_per_batch: int
    slots_per_step: int
    groups_per_slot: int
    every_slot_full: bool


def _attn_kernel(pt_ref, bias_ref, qt_ref, kt_ref, vt_ref, qd_ref, biasd_ref, kn_ref, vn_ref, ck_hbm, cv_hbm,
                 o_ref, od_ref, ka_ref, qz_ref, vz_ref, lcum_ref, carry_ref, acc_ref, za_ref, zb_ref, aa_ref, ab_ref,
                 sp_ref, lb_ref, cum_ref, qs_ref, ks_ref, vs_ref, pk_ref, pv_ref, sem, kb_ref, vb_ref, dacc_ref, dcarry_ref, du_ref, *, seq, tile, plan):
    pair = pl.program_id(1)
    step = pl.program_id(0) * pl.num_programs(1) + pair
    kw = tile // 2
    nkb = seq // kw
    lcum_ref[...] = _cumsum_lhs(lcum_ref.shape[1])
    ka_ref[:, pl.ds(LANES, LANES)] = jnp.ones((seq, LANES), BF16)
    for cb in range(nkb):
        cols = pl.ds(cb * kw, kw)
        ka_ref[cols, pl.ds(0, LANES)] = kt_ref[:, :, cols].reshape(LANES, kw).T.astype(BF16)
    qz_ref[...] = jnp.zeros(qz_ref.shape, BF16)
    term_row = lax.broadcasted_iota(jnp.int32, (BIAS_ROWS, tile), 0)
    for h in range(HEADS_PER_BLOCK):
        rows = pl.ds(h * HEAD_DIM, HEAD_DIM)
        rest = jnp.full((BIAS_ROWS, tile), bias_ref[pair * HEADS_PER_BLOCK + h], F32)
        terms = jnp.zeros((BIAS_ROWS, tile), F32)
        for i in range(BIAS_TERMS):
            term = rest.astype(BF16).astype(F32)
            terms = jnp.where(term_row == i, term, terms)
            rest = rest - term
        for qb in range(seq // tile):
            cols = pl.ds(qb * tile, tile)
            qz_ref[h, qb, rows, :] = qt_ref[h, :, cols]
            qz_ref[h, qb, pl.ds(LANES, BIAS_ROWS), :] = terms.astype(BF16)
        for cb in range(nkb):
            vz_ref[h, cb] = vt_ref[h, :, pl.ds(cb * kw, kw)].astype(BF16)

    rk = lax.broadcasted_iota(jnp.int32, (tile, tile), 0)
    cq = lax.broadcasted_iota(jnp.int32, (tile, tile), 1)
    diag_mask = rk < cq

    heads = range(HEADS_PER_BLOCK)

    rows_d, slots = qd_ref.shape[0], kn_ref.shape[1]
    n_all_groups = plan.n_batch * plan.n_groups
    step_in_batch = step % plan.steps_per_batch
    batch_d = step // plan.steps_per_batch

    def group_copies(grp):
        bd, n, buf = grp // plan.n_groups, grp % plan.n_groups, grp % 2
        copies = []
        for r in range(plan.pages_per_group):
            page = pt_ref[bd, plan.n_pages - 1 - n * plan.pages_per_group - r]
            copies.append(pltpu.make_async_copy(ck_hbm.at[page], pk_ref.at[buf, r], sem.at[buf, 0, r]))
            copies.append(pltpu.make_async_copy(cv_hbm.at[page], pv_ref.at[buf, r], sem.at[buf, 1, r]))
        return copies

    def start_group(grp):
        for n, c in enumerate(group_copies(grp)):
            c.start(priority=n % 2)

    def decode_group(grp):
        buf = grp % 2
        for c in group_copies(grp):
            c.wait()

        @pl.when(grp + 1 < n_all_groups)
        def _():
            start_group(grp + 1)

        for r in range(plan.pages_per_group):
            cols = pl.ds((plan.pages_per_group - 1 - r) * slots, slots)
            kb_ref[:, cols] = pk_ref[buf, r].astype(BF16)
            vb_ref[:, cols] = pv_ref[buf, r].astype(BF16)
        s = jnp.dot(qd_ref[...], kb_ref[...], preferred_element_type=F32)
        dcarry_ref[...], pv = _sb_block(s + biasd_ref[...], dcarry_ref[...], du_ref[...], None, vb_ref[...])
        dacc_ref[...] += pv

    def decode_new_tokens():
        du_ref[...] = _cumsum_rhs(slots)
        r = lax.broadcasted_iota(jnp.int32, (rows_d, slots), 0)
        c = lax.broadcasted_iota(jnp.int32, (rows_d, slots), 1)
        s = jnp.dot(qd_ref[...], kn_ref[...].astype(BF16), preferred_element_type=F32)
        dcarry_ref[...], dacc_ref[...] = _sb_block(
            s + biasd_ref[:, pl.ds(0, slots)], jnp.zeros((rows_d, LANES), F32), du_ref[...], c < (r // N_HEADS),
            vn_ref[...].astype(BF16))

    def decode_finish():
        r = lax.broadcasted_iota(jnp.int32, dacc_ref.shape, 0)
        l = lax.broadcasted_iota(jnp.int32, dacc_ref.shape, 1)
        own = jnp.where((r % N_HEADS) == (l // HEAD_DIM), dacc_ref[...], 0.0)
        for t in range(plan.t_new):
            od_ref[pl.ds(t, 1), :] = jnp.sum(own[t * N_HEADS:(t + 1) * N_HEADS], axis=0, keepdims=True)

    def decode_slot(qb):
        slot = step_in_batch * plan.slots_per_step + qb

        @pl.when((step == 0) & (qb == 0))
        def _():
            start_group(0)

        pl.when(slot == 0)(decode_new_tokens)
        for t in range(plan.groups_per_slot):
            n = slot * plan.groups_per_slot + t
            grp = batch_d * plan.n_groups + n
            if plan.every_slot_full:
                decode_group(grp)
            else:
                pl.when(n < plan.n_groups)(functools.partial(decode_group, grp))
        if plan.every_slot_full:
            decode_finish()
        else:
            pl.when(slot == plan.steps_per_batch * plan.slots_per_step - 1)(decode_finish)

    def qbody(qb, _):
        decode_slot(qb)
        qoff = pl.multiple_of(qb * tile, tile)
        k0 = 2 * qb

        for h in heads:
            qs_ref[h] = qz_ref[h, qb]

        def stage_keys(slot, kb):
            ks_ref[slot] = ka_ref[pl.ds(pl.multiple_of(kb * kw, kw), kw), :]

        def stage_values(slot, kb):
            for h in heads:
                vs_ref[slot, h] = vz_ref[h, kb]

        def logits(slot, z_ref):
            for h in heads:
                z_ref[h] = jnp.dot(ks_ref[slot], qs_ref[h], preferred_element_type=F32)

        def softplus(z_ref, half, mask=None):
            for h in heads:
                _softplus_t(z_ref.at[h], sp_ref.at[half, h], lb_ref.at[half, h], mask)

        def cumsums(half):
            for h in heads:
                cum_ref[half, h] = jnp.dot(lcum_ref[...], sp_ref[half, h], preferred_element_type=F32)

        def stick(half, a_ref, mask=None, first=False):
            for h in heads:
                carry = jnp.zeros((SUBLANES, tile), F32) if first else carry_ref[h]
                carry_ref[h] = _stick_weights_t(lb_ref.at[half, h], cum_ref.at[half, h], carry, a_ref.at[h], mask)

        def values(a_ref, slot, first=False):
            for h in heads:
                pv = jnp.dot(vs_ref[slot, h], a_ref[h], preferred_element_type=F32)
                acc_ref[h] = pv if first else acc_ref[h] + pv


        def weights_and_next_logits(mask_a=None, mask_b=None, first=False):
            softplus(za_ref, 0, mask_a)
            cumsums(0)
            logits(0, za_ref)
            softplus(zb_ref, 1, mask_b)
            logits(1, zb_ref)
            cumsums(1)
            stick(0, aa_ref, mask_a, first)
            stick(1, ab_ref, mask_b)

        acc_ref[...] = jnp.zeros(acc_ref.shape, F32)
        stage_keys(2, k0 + 1)
        stage_keys(3, k0)
        stage_keys(0, jnp.maximum(k0 - 1, 0))
        stage_keys(1, jnp.maximum(k0 - 2, 0))
        logits(2, za_ref)
        logits(3, zb_ref)
        weights_and_next_logits(diag_mask[kw:], diag_mask[:kw], first=True)

        def kbody(i, _):
            ka = k0 - 1 - 2 * i
            stage_keys(0, jnp.maximum(ka - 2, 0))
            stage_keys(1, jnp.maximum(ka - 3, 0))
            stage_values(0, ka + 2)
            stage_values(1, ka + 1)
            values(aa_ref, 0)
            values(ab_ref, 1)
            weights_and_next_logits()
            return 0

        lax.fori_loop(0, qb, kbody, 0)
        stage_values(0, 1)
        stage_values(1, 0)
        values(aa_ref, 0)
        values(ab_ref, 1)
        out = jnp.concatenate([acc_ref[h] for h in heads], axis=0)
        o_ref[pl.ds(qoff, tile), :] = out.T.astype(o_ref.dtype)
        return 0

    lax.fori_loop(0, seq // tile, qbody, 0)


def _attention(qt, kt, vt, bias, q_rows, row_bias, kt_new, vt_new, cache_kt, cache_vt, page_table, t_new):
    batch, n_heads, _, seq = qt.shape
    m, w = batch * seq, n_heads * HEAD_DIM
    npairs = w // LANES
    tile = _row_tile(seq, ATTN_TILE)
    kw = tile // 2
    assert kw <= CUMSUM_GROUP
    group = kw
    nb, rows_d, _ = q_rows.shape
    n_pages, slots = page_table.shape[1], cache_kt.shape[2]
    ppg = max(p for p in (DECODE_PAGES_PER_STEP, 4, 2, 1) if n_pages % p == 0)
    n_steps = batch * npairs
    assert n_steps % nb == 0, (n_steps, nb)
    n_slots = (n_steps // nb) * (seq // tile)
    n_groups = n_pages // ppg
    gps = -(-n_groups // n_slots)
    plan = _DecodePlan(n_batch=nb, t_new=t_new, n_pages=n_pages, pages_per_group=ppg, n_groups=n_groups,
                       steps_per_batch=n_steps // nb, slots_per_step=seq // tile, groups_per_slot=gps,
                       every_slot_full=gps * n_slots == n_groups)
    bias_rows = jnp.broadcast_to(row_bias[:, None], (rows_d, ppg * slots))
    spb = plan.steps_per_batch
    kv_spec = pl.BlockSpec((None, HEADS_PER_BLOCK, HEAD_DIM, seq), lambda b, p, pt: (b, p, 0, 0))
    per_d = lambda b, p, pt: ((b * npairs + p) // spb, 0, 0)
    grid_spec = pltpu.PrefetchScalarGridSpec(
        num_scalar_prefetch=1,
        grid=(batch, npairs),
        in_specs=[
            pl.BlockSpec(memory_space=pltpu.SMEM),
            kv_spec,
            kv_spec,
            kv_spec,
            pl.BlockSpec((None, rows_d, w), per_d),
            pl.BlockSpec((rows_d, ppg * slots), lambda b, p, pt: (0, 0)),
            pl.BlockSpec((None, w, slots), per_d),
            pl.BlockSpec((None, w, slots), per_d),
            pl.BlockSpec(memory_space=pl.ANY),
            pl.BlockSpec(memory_space=pl.ANY),
        ],
        out_specs=[pl.BlockSpec((seq, LANES), lambda b, p, pt: (b, p)),
                   pl.BlockSpec((None, t_new, w), per_d)],
        scratch_shapes=[
            pltpu.VMEM((seq, 2 * LANES), BF16),
            pltpu.VMEM((HEADS_PER_BLOCK, seq // tile, 2 * LANES, tile), BF16),
            pltpu.VMEM((HEADS_PER_BLOCK, seq // kw, HEAD_DIM, kw), BF16),
            pltpu.VMEM((group + 2 * SUBLANES, group), BF16),
            pltpu.VMEM((HEADS_PER_BLOCK, SUBLANES, tile), F32),
            pltpu.VMEM((HEADS_PER_BLOCK, HEAD_DIM, tile), F32),
            pltpu.VMEM((HEADS_PER_BLOCK, kw, tile), F32),
            pltpu.VMEM((HEADS_PER_BLOCK, kw, tile), F32),
            pltpu.VMEM((HEADS_PER_BLOCK, kw, tile), BF16),
            pltpu.VMEM((HEADS_PER_BLOCK, kw, tile), BF16),
            pltpu.VMEM((2, HEADS_PER_BLOCK, kw, tile), BF16),
            pltpu.VMEM((2, HEADS_PER_BLOCK, kw, tile), F32),
            pltpu.VMEM((2, HEADS_PER_BLOCK, kw + 2 * SUBLANES, tile), F32),
            pltpu.VMEM((HEADS_PER_BLOCK, 2 * LANES, tile), BF16),
            pltpu.VMEM((4, kw, 2 * LANES), BF16),
            pltpu.VMEM((2, HEADS_PER_BLOCK, HEAD_DIM, kw), BF16),
            pltpu.VMEM((2, ppg, w, slots), F32),
            pltpu.VMEM((2, ppg, w, slots), F32),
            pltpu.SemaphoreType.DMA((2, 2, ppg)),
            pltpu.VMEM((w, ppg * slots), BF16),
            pltpu.VMEM((w, ppg * slots), BF16),
            pltpu.VMEM((rows_d, w), F32),
            pltpu.VMEM((rows_d, LANES), F32),
            pltpu.VMEM((slots, slots + LANES), BF16),
        ],
    )
    return pl.pallas_call(
        functools.partial(_attn_kernel, seq=seq, tile=tile, plan=plan),
        grid_spec=grid_spec,
        out_shape=[jax.ShapeDtypeStruct((m, w), BF16), jax.ShapeDtypeStruct((nb, t_new, w), F32)],
        compiler_params=pltpu.CompilerParams(dimension_semantics=("arbitrary", "arbitrary"),
                                             vmem_limit_bytes=LARGE_VMEM_LIMIT),
        name="attention",
    )(page_table, bias, qt, kt, vt, q_rows, bias_rows, kt_new, vt_new, cache_kt, cache_vt)


def _pool_tile(ext_ref, p_ref, tp, pos0):
    gc = ext_ref.shape[1] // POOL_GROUPS
    pos = pos0 + lax.broadcasted_iota(jnp.int32, (tp, 1), 0)
    for g, win in enumerate(POOL_WINDOWS):
        cols = slice(g * gc, (g + 1) * gc)
        u_new = ext_ref[pl.ds(POOL_HALO, tp), cols]
        win_sum = u_new
        for d in range(1, win):
            win_sum = win_sum + ext_ref[pl.ds(POOL_HALO - d, tp), cols]
        cnt = jnp.minimum(win, pos + 1).astype(F32)
        p_ref[:, cols] = (win_sum / cnt - u_new).astype(p_ref.dtype)


def _pool_sample_kernel(u_ref, halo_ref, p_ref, ext_ref, *, tp, pos0):
    ext_ref[pl.ds(0, POOL_HALO), :] = halo_ref[...]
    ext_ref[pl.ds(POOL_HALO, tp), :] = u_ref[...]
    _pool_tile(ext_ref, p_ref, tp, pos0)


def _pool_sample(u3, halo3, pos0):
    nb, tp, w = u3.shape
    return pl.pallas_call(
        functools.partial(_pool_sample_kernel, tp=tp, pos0=pos0),
        grid=(nb,),
        in_specs=[
            pl.BlockSpec((None, tp, w), lambda b: (b, 0, 0)),
            pl.BlockSpec((None, POOL_HALO, w), lambda b: (b, 0, 0)),
        ],
        out_specs=pl.BlockSpec((None, tp, w), lambda b: (b, 0, 0)),
        out_shape=jax.ShapeDtypeStruct((nb, tp, w), F32),
        scratch_shapes=[pltpu.VMEM((POOL_HALO + tp, w), F32)],
        compiler_params=_cparams("parallel"),
        name="pool_sample",
    )(u3, halo3)


def _inproj(xns, w_in, col0, ncols, *, mode, out_dtype, tn, head_gain=None, seq=0, norm_gain=None, tm_cap=1024,
            pool_seq=0):
    d = w_in.shape[0]
    jb = col0 // tn
    assert col0 % tn == 0
    consts, epilogue = (), {"plain": _epi_plain, "sigmoid": _epi_sigmoid, "headnorm": _epi_headnorm}[mode]
    if mode == "headnorm":
        hid = jnp.arange(HEADNORM_GROUP, dtype=jnp.int32) // HEAD_DIM
        bd = (hid[:, None] == hid[None, :]).astype(BF16)
        gain = jnp.tile(head_gain.astype(F32), tn // HEAD_DIM).reshape(1, tn)
        consts = ((bd, bd.shape, lambda j: (0, 0)), (gain, (1, tn), lambda j: (0, 0)))
    row_sets = [_Rows(lhs=((xn, d, lambda j: 0),), head_major_seq=seq if s == 0 else 0)
                for s, xn in enumerate(xns)]
    return _ws_matmul("inproj_" + mode + ("_t" if seq else ""), ((w_in, d, lambda j: (0, j + jb)),), (0,),
                      consts, row_sets, epilogue, ncols, tn, out_dtype, tm_cap=tm_cap, norm_gain=norm_gain,
                      pool_seq=pool_seq)


def _project(xs, norm_g, wts, seq):
    w_in = wts["w_in"]
    d = w_in.shape[0]
    aw, pw = ATTN_WIDTH, d // 2
    tn = INPROJ_COLS
    assert aw == tn
    q, xns = _inproj(xs, w_in, 0, aw, mode="headnorm", out_dtype=BF16, tn=tn,
                     head_gain=wts["q_norm_g"] * (HEAD_DIM ** -0.5 * LOG2E), seq=seq, norm_gain=norm_g,
                     tm_cap=512)
    k = _inproj(xns, w_in, aw, aw, mode="headnorm", out_dtype=F32, tn=tn, head_gain=wts["k_norm_g"], seq=seq)
    v = _inproj(xns, w_in, 2 * aw, aw, mode="plain", out_dtype=F32, tn=tn, seq=seq)
    assert pw == tn
    u, p_prompt = _inproj(xns, w_in, 3 * aw, pw, mode="plain", out_dtype=F32, tn=tn, pool_seq=seq)
    gates = _inproj(xns, w_in, 3 * aw + pw, 2 * d, mode="sigmoid", out_dtype=BF16, tn=tn)
    return q, k, v, u, gates, p_prompt


def _mix_kernel(o_ref, p_ref, ga_ref, gb_ref, x_ref, wap_ref, wp_ref, ps_ref, wo_ref, g2_ref, h_ref, hn_ref):
    a = jnp.dot(o_ref[...], wap_ref[...], preferred_element_type=F32)
    groups, gc, _ = wp_ref.shape
    op = jnp.concatenate(
        [jnp.dot(p_ref[:, g * gc:(g + 1) * gc], wp_ref[g], preferred_element_type=F32) for g in range(groups)],
        axis=1)
    merged = ga_ref[...].astype(F32) * a + gb_ref[...].astype(F32) * (op * ps_ref[...])
    h = x_ref[...] + jnp.dot(merged.astype(BF16), wo_ref[...], preferred_element_type=F32)
    h_ref[...] = h
    ms = jnp.mean(h * h, axis=-1, keepdims=True)
    hn_ref[...] = ((h * lax.rsqrt(ms + RMS_EPS)) * g2_ref[...]).astype(hn_ref.dtype)


def _mix(x, o_attn, p, gates, w_ap, w_pool, pool_scale, w_out, norm2_g, tm_cap=512):
    m, d = x.shape
    aw, pw = o_attn.shape[1], p.shape[1]
    tm = _row_tile(m, tm_cap)
    whole = lambda a: pl.BlockSpec(a.shape, lambda i: (0,) * a.ndim, pipeline_mode=pl.Buffered(1))
    ps2, g2 = pool_scale.reshape(1, d), norm2_g.reshape(1, d)
    return pl.pallas_call(
        _mix_kernel,
        grid=(m // tm,),
        in_specs=[
            pl.BlockSpec((tm, aw), lambda i: (i, 0)),
            pl.BlockSpec((tm, pw), lambda i: (i, 0)),
            pl.BlockSpec((tm, d), lambda i: (i, 0)),
            pl.BlockSpec((tm, d), lambda i: (i, 1)),
            pl.BlockSpec((tm, d), lambda i: (i, 0)),
            whole(w_ap), whole(w_pool), whole(ps2), whole(w_out), whole(g2),
        ],
        out_specs=[pl.BlockSpec((tm, d), lambda i: (i, 0)), pl.BlockSpec((tm, d), lambda i: (i, 0))],
        out_shape=[jax.ShapeDtypeStruct((m, d), F32), jax.ShapeDtypeStruct((m, d), BF16)],
        compiler_params=pltpu.CompilerParams(dimension_semantics=("parallel",), vmem_limit_bytes=LARGE_VMEM_LIMIT),
        name="mix",
    )(o_attn, p, gates, gates, x, w_ap, w_pool, ps2, w_out, g2)


def _mix_and_ffn(xs, o_attns, ps, gates, wts):
    w_gu, w_down = wts["w_gate_up"], wts["w_down"]
    dff, d = w_down.shape
    w_ap, w_pool, w_out = (wts[n].astype(BF16) for n in ("w_attn_proj", "w_pool", "w_out"))
    hs, hns = zip(*[_mix(x, o, p, gt, w_ap, w_pool, wts["pool_scale"], w_out, wts["norm2_g"])
                    for x, o, p, gt in zip(xs, o_attns, ps, gates)])
    nj = dff // 512
    acts = _ws_matmul(
        "gate_up", ((w_gu, d, lambda j: (0, j)), (w_gu, d, lambda j: (0, j + nj))), (0, 0), (),
        [_Rows(lhs=((hn, d, lambda j: 0),)) for hn in hns],
        _epi_swiglu, dff, 512, BF16, tm_cap=1024)
    return _ws_matmul(
        "down_proj", ((w_down, dff, lambda j: (0, j)),), (0,), (),
        [_Rows(lhs=((act, dff, lambda j: 0),), extras=((h, lambda j: j),)) for act, h in zip(acts, hs)],
        _epi_residual, d, 512, F32, tm_cap=512)


def kernel(x_prompt, x_sample, cache_k, cache_v, state_pool, page_table, norm1_g, w_in,
           q_norm_g, k_norm_g, sb_bias, w_attn_proj, w_pool, pool_scale, w_out, norm2_g, w_gate_up, w_down):
    b_p, seq, d = x_prompt.shape
    b_s, t_s = x_sample.shape[:2]
    depth, n_pool, page = cache_k.shape[:3]
    n_pages = page_table.shape[1]
    past_len = n_pages * page
    pw = d // 2

    xp = x_prompt.reshape(b_p * seq, d)
    xs = x_sample.reshape(b_s * t_s, d)
    outs = {name: [] for name in ("kp", "vp", "up", "ks", "vs", "us")}
    for l in range(depth):
        wts = {
            "w_in": w_in[l], "q_norm_g": q_norm_g[l], "k_norm_g": k_norm_g[l], "w_attn_proj": w_attn_proj[l],
            "w_pool": w_pool[l], "pool_scale": pool_scale[l], "w_out": w_out[l], "norm2_g": norm2_g[l],
            "w_gate_up": w_gate_up[l], "w_down": w_down[l],
        }
        bias = sb_bias[l].astype(F32) * LOG2E

        (qt_p, q_s), (kt_p, k_s), (vt_p, v_s), (u_p, u_s), (gates_p, gates_s), p_p = _project(
            [xp, xs], norm1_g[l], wts, seq)

        q4 = q_s.reshape(b_s, t_s, 1, N_HEADS, HEAD_DIM)
        eye = jnp.eye(N_HEADS, dtype=BF16).reshape(1, 1, N_HEADS, N_HEADS, 1)
        q_rows = (q4 * eye).reshape(b_s, t_s * N_HEADS, ATTN_WIDTH)
        pad = ((0, 0), (0, 0), (0, page - t_s))
        kt_new = jnp.pad(k_s.reshape(b_s, t_s, ATTN_WIDTH).transpose(0, 2, 1), pad)
        vt_new = jnp.pad(v_s.reshape(b_s, t_s, ATTN_WIDTH).transpose(0, 2, 1), pad)
        cache_kt = cache_k[l].transpose(0, 2, 3, 1).reshape(n_pool, ATTN_WIDTH, page)
        cache_vt = cache_v[l].transpose(0, 2, 3, 1).reshape(n_pool, ATTN_WIDTH, page)
        o_attn_p, o_attn_s = _attention(qt_p, kt_p, vt_p, bias, q_rows, jnp.tile(bias, t_s), kt_new, vt_new,
                                        cache_kt, cache_vt, page_table, t_s)
        o_attn_s = o_attn_s.reshape(b_s * t_s, ATTN_WIDTH).astype(BF16)
        outs["kp"].append(kt_p.transpose(0, 3, 1, 2))
        outs["vp"].append(vt_p.transpose(0, 3, 1, 2))

        outs["up"].append(u_p.reshape(b_p, seq, pw)[:, seq - POOL_STATE:])
        u3 = u_s.reshape(b_s, t_s, pw)
        halo = jnp.pad(state_pool[l], ((0, 0), (POOL_HALO - POOL_STATE, 0), (0, 0)))
        p_s = _pool_sample(u3, halo, past_len).reshape(b_s * t_s, pw).astype(BF16)
        outs["ks"].append(k_s.reshape(b_s, t_s, N_HEADS, HEAD_DIM))
        outs["vs"].append(v_s.reshape(b_s, t_s, N_HEADS, HEAD_DIM))
        outs["us"].append(jnp.concatenate([state_pool[l], u3], axis=1)[:, -POOL_STATE:])

        xp, xs = _mix_and_ffn([xp, xs], [o_attn_p, o_attn_s], [p_p, p_s], [gates_p, gates_s], wts)

    st = lambda name: jnp.stack(outs[name], axis=0)
    return (xp.reshape(b_p, seq, d), xs.reshape(b_s, t_s, d),
            st("kp"), st("vp"), st("up"), st("ks"), st("vs"), st("us"))
```
